```python
import math
import jax, jax.numpy as jnp
from jax import lax
import numpy as np

D_MODEL = 2048
BATCH = 4
SEQ = 2048
DEPTH = 4

GDN_HEADS = 8
GDN_HEAD_DIM = 128
GDN_WIDTH = GDN_HEADS * GDN_HEAD_DIM
GDN_CONV = 4
GDN_CHUNK = 64
RWKV_HEAD_DIM = 64
RWKV_HEADS = 16
RWKV_WIDTH = RWKV_HEADS * RWKV_HEAD_DIM
RWKV_DECAY_LORA = 96
RWKV_ICLR_LORA = 96
RWKV_SHIFT_COLS = 3 * RWKV_WIDTH + RWKV_DECAY_LORA + RWKV_ICLR_LORA
RWKV_LN_EPS = 64e-5
S5_GROUP = 16
S5_WIDTH = 1024
S5_GROUPS = S5_WIDTH // S5_GROUP
S5_STATE = 64
N_BRANCH = 3
BRANCH_WIDTH = 1024
NORM_EPS = 1e-6

SPLIT_SIZES = (3 * GDN_WIDTH, GDN_WIDTH, GDN_HEADS, GDN_HEADS,
               RWKV_SHIFT_COLS, RWKV_WIDTH,
               S5_WIDTH, S5_WIDTH,
               N_BRANCH * D_MODEL)
IN_COLS = sum(SPLIT_SIZES)

kernel_name = "hybrid_gdn_rwkv7_s5_gated_merge"


def _split_cols(t, sizes):
    out, start = [], 0
    for s in sizes:
        out.append(t[..., start:start + s])
        start += s
    return out


def rms_norm(x, w, eps=NORM_EPS):
    xf = x.astype(jnp.float32)
    y = xf * lax.rsqrt(jnp.mean(xf * xf, axis=-1, keepdims=True) + eps)
    return (y * w.astype(jnp.float32)).astype(x.dtype)


def l2_normalize(x, eps=1e-6):
    return x * lax.rsqrt(jnp.sum(x * x, axis=-1, keepdims=True) + eps)


def causal_depthwise_conv(x, w):
    k_width, ch = w.shape
    return lax.conv_general_dilated(x, w[:, None, :].astype(x.dtype), window_strides=(1,),
                                    padding=[(k_width - 1, 0)],
                                    dimension_numbers=('NWC', 'WIO', 'NWC'),
                                    feature_group_count=ch)


def _to_chunks(t, n):
    bsz, _, h = t.shape[:3]
    t = t.reshape((bsz, n, GDN_CHUNK, h) + t.shape[3:])
    return t.transpose((1, 0, 3, 2) + tuple(range(4, t.ndim)))


def gated_delta_rule_chunked(q, k, v, beta, g):
    bsz, seq, h, dk = q.shape
    dv = v.shape[-1]
    n = seq // GDN_CHUNK
    q, k, v, beta, g = (_to_chunks(t, n) for t in (q, k, v, beta, g))
    g_cum = jnp.cumsum(g, axis=-1)
    idx = jnp.arange(GDN_CHUNK)
    strict = idx[:, None] > idx[None, :]
    incl = idx[:, None] >= idx[None, :]
    diff = g_cum[..., :, None] - g_cum[..., None, :]
    decay_strict = jnp.exp(jnp.where(strict, diff, -jnp.inf))
    decay_incl = jnp.exp(jnp.where(incl, diff, -jnp.inf))
    k_beta = k * beta[..., None]
    lower = jnp.einsum('nbhcd,nbhsd->nbhcs', k_beta, k) * decay_strict
    eye = jnp.eye(GDN_CHUNK, dtype=q.dtype)
    t_mat = lax.linalg.triangular_solve(eye + lower, jnp.broadcast_to(eye, lower.shape),
                                        left_side=True, lower=True)
    u = jnp.einsum('nbhcs,nbhse->nbhce', t_mat, v * beta[..., None])
    w = jnp.einsum('nbhcs,nbhsd->nbhcd', t_mat, k_beta * jnp.exp(g_cum)[..., None])
    attn = jnp.einsum('nbhcd,nbhsd->nbhcs', q, k) * decay_incl
    q_dec = q * jnp.exp(g_cum)[..., None]
    g_last = g_cum[..., -1]
    k_dec = k * jnp.exp(g_last[..., None] - g_cum)[..., None]

    def step(state, xs):
        q_i, k_i, u_i, w_i, a_i, gl_i = xs
        v_new = u_i - jnp.einsum('bhcd,bhde->bhce', w_i, state)
        o_i = (jnp.einsum('bhcd,bhde->bhce', q_i, state)
               + jnp.einsum('bhcs,bhse->bhce', a_i, v_new))
        state = state * jnp.exp(gl_i)[..., None, None] + jnp.einsum('bhcd,bhce->bhde', k_i, v_new)
        return state, o_i

    s0 = jnp.zeros((bsz, h, dk, dv), jnp.float32)
    _, o = lax.scan(step, s0, (q_dec, k_dec, u, w, attn, g_last))
    return o.transpose(1, 0, 3, 2, 4).reshape(bsz, seq, h, dv)


def gdn_branch(qkv, z, b_logit, a_logit, conv_w, a_log, dt_bias, norm_w):
    bsz, seq, _ = qkv.shape
    qkv = jax.nn.silu(causal_depthwise_conv(qkv, conv_w)).astype(jnp.float32)
    q, k, v = jnp.split(qkv, 3, axis=-1)
    hs = (bsz, seq, GDN_HEADS, GDN_HEAD_DIM)
    q = l2_normalize(q.reshape(hs)) * (GDN_HEAD_DIM ** -0.5)
    k = l2_normalize(k.reshape(hs))
    v = v.reshape(hs)
    beta = jax.nn.sigmoid(b_logit.astype(jnp.float32))
    g = -jnp.exp(a_log.astype(jnp.float32)) * jax.nn.softplus(
        a_logit.astype(jnp.float32) + dt_bias.astype(jnp.float32))
    o = gated_delta_rule_chunked(q, k, v, beta, g)
    o = rms_norm(o, norm_w) * jax.nn.silu(z.astype(jnp.float32).reshape(hs))
    return o.reshape(bsz, seq, GDN_WIDTH)


def rwkv7_scan(r, w, k, v, kk, a):
    bsz, seq, h, n = r.shape

    def step(state, xs):
        r_t, w_t, k_t, v_t, kk_t, a_t = xs
        sa = jnp.einsum('bhvk,bhk->bhv', state, -kk_t)
        state = (state * w_t[:, :, None, :] + sa[..., None] * (kk_t * a_t)[:, :, None, :]
                 + v_t[..., None] * k_t[:, :, None, :])
        return state, jnp.einsum('bhvk,bhk->bhv', state, r_t)

    xs = tuple(t.transpose(1, 0, 2, 3) for t in (r, w, k, v, kk, a))
    s0 = jnp.zeros((bsz, h, n, n), jnp.float32)
    _, y = lax.scan(step, s0, xs)
    return y.transpose(1, 0, 2, 3)


def rwkv7_branch(feat, z, mu, w0, w_up, a0, a_up, k_k, k_a, r_k, lnx_w, lnx_b):
    bsz, seq, _ = feat.shape
    f32 = jnp.float32
    feat = feat.astype(f32)
    prev = jnp.pad(feat[:, :-1], ((0, 0), (1, 0), (0, 0)))
    feat = feat + (prev - feat) * mu.astype(f32)
    r, k, v, wl, al = _split_cols(feat, (RWKV_WIDTH,) * 3 + (RWKV_DECAY_LORA, RWKV_ICLR_LORA))
    w_pre = w0.astype(f32) + jnp.tanh(wl) @ w_up.astype(f32)
    decay = jnp.exp(-jnp.exp(-jax.nn.softplus(-w_pre) - 0.5))
    a = jax.nn.sigmoid(a0.astype(f32) + al @ a_up.astype(f32))
    hs = (bsz, seq, RWKV_HEADS, RWKV_HEAD_DIM)
    kk = l2_normalize((k * k_k.astype(f32)).reshape(hs))
    k = k * (1.0 + (a - 1.0) * k_a.astype(f32))
    r, k, v, a, decay = (t.reshape(hs) for t in (r, k, v, a, decay))
    y = rwkv7_scan(r, decay, k, v, kk, a)
    mean = jnp.mean(y, axis=-1, keepdims=True)
    var = jnp.mean(jnp.square(y - mean), axis=-1, keepdims=True)
    y = ((y - mean) * lax.rsqrt(var + RWKV_LN_EPS) * lnx_w.astype(f32).reshape(RWKV_HEADS, RWKV_HEAD_DIM)
         + lnx_b.astype(f32).reshape(RWKV_HEADS, RWKV_HEAD_DIM))
    y = y + jnp.sum(r * k * r_k.astype(f32), axis=-1, keepdims=True) * v
    y = y * jax.nn.silu(z.astype(f32).reshape(hs))
    return y.reshape(bsz, seq, RWKV_WIDTH)


def s5_branch(u, z, a_re, a_im, log_dt, b_re, b_im, c_re, c_im, d_skip, glu_w, glu_b):
    bsz, seq, _ = u.shape
    f32 = jnp.float32
    u = u.astype(f32)
    ug = u.reshape(bsz, seq, S5_GROUPS, S5_GROUP)
    a_re, a_im, b_re, b_im, c_re, c_im = (t.astype(f32) for t in (a_re, a_im, b_re, b_im, c_re, c_im))
    dt = jnp.exp(log_dt.astype(f32))[:, None]
    mag = jnp.exp(a_re * dt)
    ab_re, ab_im = mag * jnp.cos(a_im * dt), mag * jnp.sin(a_im * dt)
    den = a_re * a_re + a_im * a_im
    coef_re = ((ab_re - 1.0) * a_re + ab_im * a_im) / den
    coef_im = (ab_im * a_re - (ab_re - 1.0) * a_im) / den
    bb_re = coef_re[..., None] * b_re - coef_im[..., None] * b_im
    bb_im = coef_re[..., None] * b_im + coef_im[..., None] * b_re
    bu_re = jnp.einsum('gpc,blgc->blgp', bb_re, ug)
    bu_im = jnp.einsum('gpc,blgc->blgp', bb_im, ug)
    shp = (1, seq, S5_GROUPS, S5_STATE)
    a_re_t = jnp.broadcast_to(ab_re, shp)
    a_im_t = jnp.broadcast_to(ab_im, shp)

    def combine(e1, e2):
        a1r, a1i, b1r, b1i = e1
        a2r, a2i, b2r, b2i = e2
        return (a1r * a2r - a1i * a2i, a1r * a2i + a1i * a2r,
                a2r * b1r - a2i * b1i + b2r, a2r * b1i + a2i * b1r + b2i)

    _, _, s_re, s_im = lax.associative_scan(combine, (a_re_t, a_im_t, bu_re, bu_im), axis=1)
    y = jnp.einsum('gcp,blgp->blgc', c_re, s_re) - jnp.einsum('gcp,blgp->blgc', c_im, s_im)
    y = y.reshape(bsz, seq, S5_WIDTH) + d_skip.astype(f32) * u
    y = jax.nn.gelu(y)
    y = y * jax.nn.sigmoid(y @ glu_w.astype(f32) + glu_b.astype(f32))
    return y * jax.nn.silu(z.astype(f32))


def setup_inputs(seed: int = 0) -> dict:
    key = jax.random.key(seed)
    ks = iter(jax.random.split(key, 40))
    f32 = jnp.float32

    def nrm(shape, s):
        return s * jax.random.normal(next(ks), shape, f32)

    def unif(shape, lo, hi):
        return jax.random.uniform(next(ks), shape, f32, lo, hi)

    gdn_dt = jnp.exp(unif((DEPTH, GDN_HEADS), math.log(1e-3), math.log(1e-1)))
    return {
        "x": nrm((BATCH, SEQ, D_MODEL), 1.0),
        "norm_w": 1.0 + nrm((DEPTH, D_MODEL), 0.02),
        "w_in": nrm((DEPTH, D_MODEL, IN_COLS), D_MODEL ** -0.5),
        "gdn_conv_w": nrm((DEPTH, GDN_CONV, 3 * GDN_WIDTH), GDN_CONV ** -0.5),
        "gdn_a_log": jnp.log(unif((DEPTH, GDN_HEADS), 1.0, 16.0)),
        "gdn_dt_bias": jnp.log(jnp.expm1(gdn_dt)),
        "gdn_norm_w": 1.0 + nrm((DEPTH, GDN_HEAD_DIM), 0.02),
        "rwkv_mu": unif((DEPTH, RWKV_SHIFT_COLS), 0.0, 1.0),
        "rwkv_w0": unif((DEPTH, RWKV_WIDTH), -6.5, -1.5),
        "rwkv_w_up": nrm((DEPTH, RWKV_DECAY_LORA, RWKV_WIDTH), 0.5 * RWKV_DECAY_LORA ** -0.5),
        "rwkv_a0": nrm((DEPTH, RWKV_WIDTH), 0.1),
        "rwkv_a_up": nrm((DEPTH, RWKV_ICLR_LORA, RWKV_WIDTH), 0.5 * RWKV_ICLR_LORA ** -0.5),
        "rwkv_k_k": 0.85 + nrm((DEPTH, RWKV_WIDTH), 0.05),
        "rwkv_k_a": 1.0 + nrm((DEPTH, RWKV_WIDTH), 0.05),
        "rwkv_r_k": nrm((DEPTH, RWKV_HEADS, RWKV_HEAD_DIM), 0.1),
        "rwkv_lnx_w": 1.0 + nrm((DEPTH, RWKV_WIDTH), 0.02),
        "rwkv_lnx_b": nrm((DEPTH, RWKV_WIDTH), 0.02),
        "s5_a_re": -0.5 + nrm((DEPTH, S5_GROUPS, S5_STATE), 0.01),
        "s5_a_im": jnp.pi * jnp.arange(S5_STATE, dtype=f32)[None, None, :] + nrm((DEPTH, S5_GROUPS, S5_STATE), 0.01),
        "s5_log_dt": unif((DEPTH, S5_GROUPS), math.log(1e-3), math.log(1e-1)),
        "s5_b_re": nrm((DEPTH, S5_GROUPS, S5_STATE, S5_GROUP), (2 * S5_GROUP) ** -0.5),
        "s5_b_im": nrm((DEPTH, S5_GROUPS, S5_STATE, S5_GROUP), (2 * S5_GROUP) ** -0.5),
        "s5_c_re": nrm((DEPTH, S5_GROUPS, S5_GROUP, S5_STATE), (2 * S5_STATE) ** -0.5),
        "s5_c_im": nrm((DEPTH, S5_GROUPS, S5_GROUP, S5_STATE), (2 * S5_STATE) ** -0.5),
        "s5_d": nrm((DEPTH, S5_WIDTH), 1.0),
        "s5_glu_w": nrm((DEPTH, S5_WIDTH, S5_WIDTH), S5_WIDTH ** -0.5),
        "s5_glu_b": nrm((DEPTH, S5_WIDTH), 0.02),
        "gate_b": nrm((DEPTH, N_BRANCH, D_MODEL), 0.02),
        "w_branch": nrm((DEPTH, N_BRANCH, BRANCH_WIDTH, D_MODEL), BRANCH_WIDTH ** -0.5),
        "w_out": nrm((DEPTH, D_MODEL, D_MODEL), D_MODEL ** -0.5),
        "final_norm_w": 1.0 + nrm((D_MODEL,), 0.02),
    }


def reference(x, norm_w, w_in, gdn_conv_w, gdn_a_log, gdn_dt_bias, gdn_norm_w,
              rwkv_mu, rwkv_w0, rwkv_w_up, rwkv_a0, rwkv_a_up, rwkv_k_k, rwkv_k_a, rwkv_r_k,
              rwkv_lnx_w, rwkv_lnx_b,
              s5_a_re, s5_a_im, s5_log_dt, s5_b_re, s5_b_im, s5_c_re, s5_c_im, s5_d,
              s5_glu_w, s5_glu_b,
              gate_b, w_branch, w_out, final_norm_w):
    bsz, seq, _ = x.shape
    for i in range(DEPTH):
        h = rms_norm(x, norm_w[i])
        proj = h @ w_in[i]
        (g_qkv, g_z, g_b, g_a, r_feat, r_z, s_u, s_z, gate_logits) = _split_cols(proj, SPLIT_SIZES)
        o_a = gdn_branch(g_qkv, g_z, g_b, g_a, gdn_conv_w[i], gdn_a_log[i], gdn_dt_bias[i], gdn_norm_w[i])
        o_b = rwkv7_branch(r_feat, r_z, rwkv_mu[i], rwkv_w0[i], rwkv_w_up[i], rwkv_a0[i], rwkv_a_up[i],
                           rwkv_k_k[i], rwkv_k_a[i], rwkv_r_k[i], rwkv_lnx_w[i], rwkv_lnx_b[i])
        o_c = s5_branch(s_u, s_z, s5_a_re[i], s5_a_im[i], s5_log_dt[i], s5_b_re[i], s5_b_im[i],
                        s5_c_re[i], s5_c_im[i], s5_d[i], s5_glu_w[i], s5_glu_b[i])
        branches = jnp.stack([o_a, o_b, o_c], axis=2)
        branch_proj = jnp.einsum('blnc,ncd->blnd', branches, w_branch[i].astype(jnp.float32))
        gates = jax.nn.sigmoid(gate_logits.astype(jnp.float32).reshape(bsz, seq, N_BRANCH, D_MODEL)
                               + gate_b[i].astype(jnp.float32))
        merged = jnp.sum(gates * branch_proj, axis=2)
        x = x + (merged @ w_out[i].astype(jnp.float32)).astype(x.dtype)
    return rms_norm(x, final_norm_w)
```

```python
import functools
import math

import jax
import jax.numpy as jnp
from jax import lax
from jax.experimental import pallas as pl
from jax.experimental.pallas import tpu as pltpu

F32 = jnp.float32
BF16 = jnp.bfloat16

D_MODEL = 2048
GDN_HEADS = 8
GDN_HEAD_DIM = 128
GDN_WIDTH = 1024
RWKV_HEAD_DIM = 64
RWKV_WIDTH = 1024
RWKV_LORA = 96
RWKV_LN_EPS = 64e-5
S5_GROUP = 16
S5_STATE = 64
S5_WIDTH = 1024
N_BRANCH = 3
NORM_EPS = 1e-6

LANE = 128
MXU_TILE = 256
CHUNK = 64
S5_TILE = 128
VMEM_LIMIT = 52 * 1024 * 1024

COL_QKV = 0
COL_GZ = 24
COL_GBA = 32
COL_WL = 33
COL_AL = 34
COL_RKV = 36
COL_RZ = 60
COL_SU = 68
COL_SZ = 76
COL_GATE = 84
N_COLBLK = 132
NP = N_COLBLK * LANE


def _sigmoid(x):
    return 1.0 / (1.0 + jnp.exp(-x))


def _silu(x):
    return x * _sigmoid(x)


def _softplus(x):
    return jnp.maximum(x, 0.0) + jnp.log1p(jnp.exp(-jnp.abs(x)))


def _mm(a, b):
    return jnp.dot(a.astype(BF16), b.astype(BF16), preferred_element_type=F32)


def _mm_nt(a, b):
    return lax.dot_general(a.astype(BF16), b.astype(BF16), (((1,), (1,)), ((), ())),
                           preferred_element_type=F32)


def _mm_tn(a, b):
    return lax.dot_general(a.astype(BF16), b.astype(BF16), (((0,), (0,)), ((), ())),
                           preferred_element_type=F32)


def _split3(x):
    x1 = x.astype(BF16)
    r1 = x - x1.astype(F32)
    x2 = r1.astype(BF16)
    x3 = (r1 - x2.astype(F32)).astype(BF16)
    return x1, x2, x3


def _mm_sel_lhs(sel, x):
    s = sel.astype(BF16)
    out = None
    for part in _split3(x):
        t = jnp.dot(s, part, preferred_element_type=F32)
        out = t if out is None else out + t
    return out


def _mm_sel_rhs(x, sel):
    s = sel.astype(BF16)
    out = None
    for part in _split3(x):
        t = jnp.dot(part, s, preferred_element_type=F32)
        out = t if out is None else out + t
    return out


def _unit_lower_inverse(x, eye, size):
    p = eye + x
    y = x
    span = 2
    while span < size:
        y = _mm(y, y)
        p = p + _mm(p, y)
        span *= 2
    return p


def _iota2(shape, dim):
    return lax.broadcasted_iota(jnp.int32, shape, dim)


def _shift_rows_halo(ref, r0, rows, keep, shift):
    h0 = pl.multiple_of(jnp.maximum(r0 - 8, 0), 8)
    halo = ref[pl.ds(h0, 8), :] * keep
    cur = ref[pl.ds(r0, rows), :]
    xc = jnp.concatenate([halo, cur], axis=0)
    return cur, [pltpu.roll(xc, s, 0)[8:, :] for s in range(1, shift + 1)]


def _inproj_kernel(x_ref, nw_ref, w_ref, o_ref, h_ref):
    @pl.when(pl.program_id(1) == 0)
    def _():
        x = x_ref[...]
        ms = jnp.mean(x * x, axis=-1, keepdims=True)
        h_ref[...] = (x * lax.rsqrt(ms + NORM_EPS) * nw_ref[...]).astype(BF16)

    o_ref[...] = jnp.dot(h_ref[...], w_ref[...], preferred_element_type=F32)


def _inproj(x2, norm_w_row, w_pad, tm=1024, tn=512):
    t = x2.shape[0]
    tm = min(tm, t)
    return pl.pallas_call(
        _inproj_kernel,
        grid=(t // tm, NP // tn),
        in_specs=[pl.BlockSpec((tm, D_MODEL), lambda i, j: (i, 0)),
                  pl.BlockSpec((1, D_MODEL), lambda i, j: (0, 0)),
                  pl.BlockSpec((D_MODEL, tn), lambda i, j: (0, j))],
        out_specs=pl.BlockSpec((tm, tn), lambda i, j: (i, j)),
        out_shape=jax.ShapeDtypeStruct((t, NP), F32),
        scratch_shapes=[pltpu.VMEM((tm, D_MODEL), BF16)],
        compiler_params=pltpu.CompilerParams(
            dimension_semantics=("arbitrary", "arbitrary"), vmem_limit_bytes=VMEM_LIMIT),
        name="inproj",
    )(x2, norm_w_row, w_pad)


def _gdn_kernel(q_ref, k_ref, v_ref, z_ref, ba_ref, cq_ref, ck_ref, cv_ref,
                alog_ref, dtb_ref, nw_ref, o_ref, s_ref, *, seq):
    hp = pl.program_id(1)
    c_len = CHUNK
    rows = 2 * c_len
    width = 2 * GDN_HEAD_DIM

    lane_w = _iota2((1, width), 1)
    lane_b = _iota2((1, LANE), 1)
    ri = _iota2((rows, rows), 0)
    ci = _iota2((rows, rows), 1)
    same_head = (ri // c_len) == (ci // c_len)
    strict = same_head & (ci < ri)
    incl = same_head & (ci <= ri)
    tril_sel = jnp.where(incl, 1.0, 0.0)
    triu_sel = jnp.where(same_head & (ri <= ci), 1.0, 0.0)
    head_sel = jnp.where(same_head, 1.0, 0.0)
    ones_sel = jnp.ones((rows, rows), F32)
    eye = jnp.where(ri == ci, 1.0, 0.0)
    head0_lane = lane_w < GDN_HEAD_DIM

    s_ref[...] = jnp.zeros_like(s_ref)
    alog_row = alog_ref[...]
    dtb_row = dtb_ref[...]
    nw_row = nw_ref[...]

    def stack(x):
        return jnp.concatenate([jnp.where(head0_lane, x, 0.0), jnp.where(head0_lane, 0.0, x)], axis=0)

    def per_head(x, fn):
        return jnp.concatenate([fn(x[:, :GDN_HEAD_DIM]), fn(x[:, GDN_HEAD_DIM:])], axis=1)

    def l2n(xh):
        return xh * lax.rsqrt(jnp.sum(xh * xh, axis=-1, keepdims=True) + 1e-6)

    def rms(oh):
        return oh * lax.rsqrt(jnp.mean(oh * oh, axis=-1, keepdims=True) + NORM_EPS) * nw_row

    def body(c, carry):
        r0 = pl.multiple_of(c * c_len, c_len)
        keep = jnp.where(c > 0, 1.0, 0.0)

        def conv(x_ref, cw_ref):
            cur, sh = _shift_rows_halo(x_ref, r0, c_len, keep, 3)
            cw = cw_ref[...]
            acc = cur * cw[3:4, :]
            for s in range(1, 4):
                acc = acc + sh[s - 1] * cw[3 - s:4 - s, :]
            return _silu(acc)

        q = per_head(conv(q_ref, cq_ref), l2n) * (GDN_HEAD_DIM ** -0.5)
        k = per_head(conv(k_ref, ck_ref), l2n)
        v = conv(v_ref, cv_ref)

        ba = ba_ref[pl.ds(r0, c_len), :]
        g_all = -jnp.exp(alog_row) * _softplus(ba + dtb_row)
        betas, gs = [], []
        for h in range(2):
            hid = 2 * hp + h
            betas.append(_sigmoid(jnp.sum(jnp.where(lane_b == hid, ba, 0.0), axis=-1, keepdims=True)))
            gs.append(jnp.sum(jnp.where(lane_b == GDN_HEADS + hid, g_all, 0.0), axis=-1, keepdims=True))
        beta = jnp.concatenate(betas, axis=0)
        gb = jnp.broadcast_to(jnp.concatenate(gs, axis=0), (rows, width))

        gcol = _mm_sel_lhs(tril_sel, gb)
        grow = _mm_sel_lhs(ones_sel, gb[:, :rows] * triu_sel)
        glast = _mm_sel_lhs(head_sel, gb)
        diff = gcol[:, :rows] - grow
        d_strict = jnp.where(strict, jnp.exp(jnp.where(strict, diff, 0.0)), 0.0)
        d_incl = jnp.where(incl, jnp.exp(jnp.where(incl, diff, 0.0)), 0.0)
        egc = jnp.exp(gcol)

        sq, sk, sv = stack(q), stack(k), stack(v)
        skb = sk * beta
        lower = _mm_nt(skb, sk) * d_strict
        attn = _mm_nt(sq, sk) * d_incl
        t_mat = _unit_lower_inverse(-lower, eye, c_len)
        u = _mm(t_mat, sv * beta)
        w = _mm(t_mat, skb * egc)

        state = s_ref[...]
        v_new = u - _mm(w, state)
        o_st = _mm(sq * egc, state) + _mm(attn, v_new)
        gl_lane = jnp.where(head0_lane, glast[0:1, :], glast[c_len:c_len + 1, :])
        s_ref[...] = state * jnp.exp(gl_lane) + _mm_tn(sk * jnp.exp(glast - gcol), v_new)

        o = o_st[:c_len, :] + o_st[c_len:, :]
        z = z_ref[pl.ds(r0, c_len), :]
        o_ref[pl.ds(r0, c_len), :] = (per_head(o, rms) * _silu(z)).astype(o_ref.dtype)
        return carry

    lax.fori_loop(0, seq // c_len, body, 0)


def _gdn(proj, conv_w, alog_row, dtb_row, nw_row, batch, seq):
    w2 = 2 * GDN_HEAD_DIM
    colspec = lambda base: pl.BlockSpec((seq, w2), lambda b, h, base=base: (b, base + h))
    cwspec = lambda base: pl.BlockSpec((4, w2), lambda b, h, base=base: (0, base + h))
    rowspec = pl.BlockSpec((1, LANE), lambda b, h: (0, 0))
    return pl.pallas_call(
        functools.partial(_gdn_kernel, seq=seq),
        grid=(batch, GDN_HEADS // 2),
        in_specs=[colspec(0), colspec(4), colspec(8), colspec(COL_GZ // 2),
                  pl.BlockSpec((seq, LANE), lambda b, h: (b, COL_GBA)),
                  cwspec(0), cwspec(4), cwspec(8), rowspec, rowspec, rowspec],
        out_specs=pl.BlockSpec((seq, w2), lambda b, h: (b, h)),
        out_shape=jax.ShapeDtypeStruct((batch * seq, GDN_WIDTH), BF16),
        scratch_shapes=[pltpu.VMEM((w2, w2), F32)],
        compiler_params=pltpu.CompilerParams(
            dimension_semantics=("arbitrary", "arbitrary"), vmem_limit_bytes=VMEM_LIMIT),
        name="gdn",
    )(proj, proj, proj, proj, proj, conv_w, conv_w, conv_w, alog_row, dtb_row, nw_row)


def _rwkv_kernel(r_ref, k_ref, v_ref, wl_ref, al_ref, z_ref,
                 mur_ref, muk_ref, muv_ref, muwl_ref, mual_ref,
                 w0_ref, wup_ref, a0_ref, aup_ref, kk_ref, ka_ref, rk_ref, lnw_ref, lnb_ref,
                 o_ref, n_ref, *, seq):
    c_len = CHUNK
    nh = MXU_TILE // RWKV_HEAD_DIM
    width = MXU_TILE
    rows = nh * c_len

    ri = _iota2((rows, rows), 0)
    ci = _iota2((rows, rows), 1)
    same_head = (ri // c_len) == (ci // c_len)
    strict = same_head & (ci < ri)
    incl = same_head & (ci <= ri)
    eye = jnp.where(ri == ci, 1.0, 0.0)
    head_sum = jnp.where(same_head, 1.0, 0.0)
    ti = _iota2((c_len, c_len), 0)
    tj = _iota2((c_len, c_len), 1)
    tril_t = jnp.where(tj <= ti, 1.0, 0.0)
    lane_head = _iota2((1, width), 1) // RWKV_HEAD_DIM

    n_ref[...] = jnp.zeros_like(n_ref)

    def stack(x):
        return jnp.concatenate([jnp.where(lane_head == h, x, 0.0) for h in range(nh)], axis=0)

    def unstack(y):
        out = y[:c_len, :]
        for h in range(1, nh):
            out = out + y[h * c_len:(h + 1) * c_len, :]
        return out

    def body(c, carry):
        r0 = pl.multiple_of(c * c_len, c_len)
        keep = jnp.where(c > 0, 1.0, 0.0)

        def shifted(x_ref, mu_ref):
            cur, sh = _shift_rows_halo(x_ref, r0, c_len, keep, 1)
            return cur + (sh[0] - cur) * mu_ref[...]

        r = shifted(r_ref, mur_ref)
        k = shifted(k_ref, muk_ref)
        v = shifted(v_ref, muv_ref)
        wl = shifted(wl_ref, muwl_ref)
        al = shifted(al_ref, mual_ref)

        w_pre = w0_ref[...] + _mm(jnp.tanh(wl), wup_ref[...])
        logw = -jnp.exp(-_softplus(-w_pre) - 0.5)
        a = _sigmoid(a0_ref[...] + _mm(al, aup_ref[...]))
        kkx = k * kk_ref[...]
        kk = kkx * lax.rsqrt(_mm_sel_rhs(kkx * kkx, head_sum) + 1e-6)
        k2 = k * (1.0 + (a - 1.0) * ka_ref[...])

        lc = _mm_sel_lhs(tril_t, logw)
        lc_last = lc[c_len - 1:c_len, :]
        e_neg = jnp.exp(-lc)
        e_rem = jnp.exp(lc_last - lc)
        kka = kk * a
        a_t = stack(-kk * jnp.exp(lc - logw))
        r_t = stack(r * jnp.exp(lc))
        b_t = stack(kka * e_neg)
        k_t = stack(k2 * e_neg)
        b_g = stack(kka * e_rem)
        k_g = stack(k2 * e_rem)
        sv = stack(v)

        prod = _mm_nt(jnp.concatenate([a_t, r_t], axis=0), jnp.concatenate([b_t, k_t], axis=0))
        a_ab = jnp.where(strict, prod[:rows, :rows], 0.0)
        a_ak = jnp.where(strict, prod[:rows, rows:], 0.0)
        a_rb = jnp.where(incl, prod[rows:, :rows], 0.0)
        a_rk = jnp.where(incl, prod[rows:, rows:], 0.0)

        t_mat = _unit_lower_inverse(a_ab, eye, c_len)
        w_p = _mm(t_mat, a_t)
        v_p = _mm(t_mat, _mm(a_ak, sv))

        state = n_ref[...]
        x0 = _mm(jnp.concatenate([w_p, r_t], axis=0), state)
        u = x0[:rows, :] + v_p
        y_st = x0[rows:, :] + _mm(a_rb, u) + _mm(a_rk, sv)
        gam_col = jnp.transpose(jnp.broadcast_to(jnp.exp(lc_last), (width, width)))
        n_ref[...] = gam_col * state + _mm_tn(b_g, u) + _mm_tn(k_g, sv)

        y = unstack(y_st)
        inv_n = 1.0 / RWKV_HEAD_DIM
        mean = _mm_sel_rhs(y, head_sum) * inv_n
        dlt = y - mean
        var = _mm_sel_rhs(dlt * dlt, head_sum) * inv_n
        y = dlt * lax.rsqrt(var + RWKV_LN_EPS) * lnw_ref[...] + lnb_ref[...]
        y = y + _mm_sel_rhs(r * k2 * rk_ref[...], head_sum) * v
        z = z_ref[pl.ds(r0, c_len), :]
        o_ref[pl.ds(r0, c_len), :] = (y * _silu(z)).astype(o_ref.dtype)
        return carry

    lax.fori_loop(0, seq // c_len, body, 0)


def _rwkv(proj, p, batch, seq):
    w = MXU_TILE
    n_grp = RWKV_WIDTH // w
    colspec = lambda base: pl.BlockSpec((seq, w), lambda b, h, base=base: (b, base + h))
    lspec = lambda col: pl.BlockSpec((seq, LANE), lambda b, h, col=col: (b, col))
    grow = pl.BlockSpec((1, w), lambda b, h: (0, h))
    lrow = pl.BlockSpec((1, LANE), lambda b, h: (0, 0))
    upspec = pl.BlockSpec((LANE, w), lambda b, h: (0, h))
    base = COL_RKV // 2
    return pl.pallas_call(
        functools.partial(_rwkv_kernel, seq=seq),
        grid=(batch, n_grp),
        in_specs=[colspec(base), colspec(base + 4), colspec(base + 8), lspec(COL_WL), lspec(COL_AL),
                  colspec(COL_RZ // 2),
                  grow, grow, grow, lrow, lrow,
                  grow, upspec, grow, upspec, grow, grow, grow, grow, grow],
        out_specs=pl.BlockSpec((seq, w), lambda b, h: (b, h)),
        out_shape=jax.ShapeDtypeStruct((batch * seq, RWKV_WIDTH), BF16),
        scratch_shapes=[pltpu.VMEM((w, w), F32)],
        compiler_params=pltpu.CompilerParams(
            dimension_semantics=("arbitrary", "arbitrary"), vmem_limit_bytes=VMEM_LIMIT),
        name="rwkv",
    )(proj, proj, proj, proj, proj, proj,
      p["mu_r"], p["mu_k"], p["mu_v"], p["mu_wl"], p["mu_al"],
      p["w0"], p["w_up"], p["a0"], p["a_up"], p["k_k"], p["k_a"], p["r_k"], p["lnx_w"], p["lnx_b"])


def _s5_kernel(u_ref, lre_ref, lim_ref, ldt_ref, bre_ref, bim_ref, cre_ref, cim_ref, d_ref,
               o_ref, pre_ref, pim_ref, *, seq):
    tile = S5_TILE
    n_pair = LANE // (2 * S5_GROUP)
    row = _iota2((tile, LANE), 0)

    def cmul(ar, ai, br, bi):
        return ar * br - ai * bi, ar * bi + ai * br

    def shift(x, d):
        if d % 8 == 0:
            return jnp.concatenate([jnp.zeros((d, LANE), F32), x[:tile - d, :]], axis=0)
        return jnp.where(row >= d, pltpu.roll(x, d, 0), 0.0)

    def scan(sr, si, ar, ai):
        d = 1
        while d < tile:
            tr, ti = cmul(ar, ai, shift(sr, d), shift(si, d))
            sr, si = sr + tr, si + ti
            ar, ai = cmul(ar, ai, ar, ai)
            d *= 2
        return sr, si

    def pair_body(p, carry):
        lre = lre_ref[0, pl.ds(p, 1), :]
        lim = lim_ref[0, pl.ds(p, 1), :]
        dt = jnp.exp(ldt_ref[0, pl.ds(p, 1), :])
        mag = jnp.exp(lre * dt)
        ab_re = mag * jnp.cos(lim * dt)
        ab_im = mag * jnp.sin(lim * dt)
        den = lre * lre + lim * lim
        coef_re = ((ab_re - 1.0) * lre + ab_im * lim) / den
        coef_im = (ab_im * lre - (ab_re - 1.0) * lim) / den
        b_re = bre_ref[0, p]
        b_im = bim_ref[0, p]
        bb_re = (coef_re * b_re - coef_im * b_im).astype(BF16)
        bb_im = (coef_re * b_im + coef_im * b_re).astype(BF16)
        c_re = cre_ref[0, p].astype(BF16)
        c_im = cim_ref[0, p].astype(BF16)

        imp_re = jnp.where(row == 0, jnp.broadcast_to(ab_re, (tile, LANE)), 0.0)
        imp_im = jnp.where(row == 0, jnp.broadcast_to(ab_im, (tile, LANE)), 0.0)
        pw_re, pw_im = scan(imp_re, imp_im, ab_re, ab_im)
        pre_ref[...] = pw_re
        pim_ref[...] = pw_im

        def tile_body(i, st):
            cr, cim_ = st
            t0 = pl.multiple_of(i * tile, tile)
            ub = u_ref[pl.ds(t0, tile), :].astype(BF16)
            sr = jnp.dot(ub, bb_re, preferred_element_type=F32)
            si = jnp.dot(ub, bb_im, preferred_element_type=F32)
            sr, si = scan(sr, si, ab_re, ab_im)
            tr, ti = cmul(pre_ref[...], pim_ref[...], cr, cim_)
            sr, si = sr + tr, si + ti
            y = _mm(sr, c_re) - _mm(si, c_im)

            @pl.when(p == 0)
            def _():
                o_ref[pl.ds(t0, tile), :] = y

            @pl.when(p > 0)
            def _():
                o_ref[pl.ds(t0, tile), :] += y

            return sr[tile - 1:tile, :], si[tile - 1:tile, :]

        zero = jnp.zeros((1, LANE), F32)
        lax.fori_loop(0, seq // tile, tile_body, (zero, zero))
        return carry

    lax.fori_loop(0, n_pair, pair_body, 0)

    def fin(i, carry):
        t0 = pl.multiple_of(i * tile, tile)
        y = o_ref[pl.ds(t0, tile), :] + d_ref[0] * u_ref[pl.ds(t0, tile), :]
        inner = math.sqrt(2.0 / math.pi) * (y + 0.044715 * (y * y * y))
        o_ref[pl.ds(t0, tile), :] = 0.5 * y * (1.0 + jnp.tanh(inner))
        return carry

    lax.fori_loop(0, seq // tile, fin, 0)


def _s5(proj, p, batch, seq):
    n_blk = S5_WIDTH // LANE
    n_pair = LANE // (2 * S5_GROUP)
    rowspec = pl.BlockSpec((1, n_pair, LANE), lambda b, j: (j, 0, 0))
    matspec = pl.BlockSpec((1, n_pair, LANE, LANE), lambda b, j: (j, 0, 0, 0))
    return pl.pallas_call(
        functools.partial(_s5_kernel, seq=seq),
        grid=(batch, n_blk),
        in_specs=[pl.BlockSpec((seq, LANE), lambda b, j: (b, COL_SU + j)),
                  rowspec, rowspec, rowspec, matspec, matspec, matspec, matspec,
                  pl.BlockSpec((1, 1, LANE), lambda b, j: (j, 0, 0))],
        out_specs=pl.BlockSpec((seq, LANE), lambda b, j: (b, j)),
        out_shape=jax.ShapeDtypeStruct((batch * seq, S5_WIDTH), F32),
        scratch_shapes=[pltpu.VMEM((S5_TILE, LANE), F32), pltpu.VMEM((S5_TILE, LANE), F32)],
        compiler_params=pltpu.CompilerParams(
            dimension_semantics=("arbitrary", "arbitrary"), vmem_limit_bytes=VMEM_LIMIT),
        name="s5_scan",
    )(proj, p["lam_re"], p["lam_im"], p["log_dt"], p["b_re"], p["b_im"], p["c_re"], p["c_im"], p["d"])


def _s5_glu_kernel(y_ref, w_ref, b_ref, z0_ref, z1_ref, o_ref):
    y = y_ref[...]
    gate = _sigmoid(_mm(y, w_ref[...]) + b_ref[...])
    z = jnp.concatenate([z0_ref[...], z1_ref[...]], axis=1)
    o_ref[...] = (y * gate * _silu(z)).astype(o_ref.dtype)


def _s5_glu(yc, proj, glu_w, glu_b, tm=512):
    t = yc.shape[0]
    tm = min(tm, t)
    zb = COL_SZ * LANE // 512
    return pl.pallas_call(
        _s5_glu_kernel,
        grid=(t // tm,),
        in_specs=[pl.BlockSpec((tm, S5_WIDTH), lambda i: (i, 0)),
                  pl.BlockSpec((S5_WIDTH, S5_WIDTH), lambda i: (0, 0)),
                  pl.BlockSpec((1, S5_WIDTH), lambda i: (0, 0)),
                  pl.BlockSpec((tm, 512), lambda i: (i, zb)),
                  pl.BlockSpec((tm, 512), lambda i: (i, zb + 1))],
        out_specs=pl.BlockSpec((tm, S5_WIDTH), lambda i: (i, 0)),
        out_shape=jax.ShapeDtypeStruct((t, S5_WIDTH), BF16),
        compiler_params=pltpu.CompilerParams(
            dimension_semantics=("arbitrary",), vmem_limit_bytes=VMEM_LIMIT),
        name="s5_glu",
    )(yc, glu_w, glu_b, proj, proj)


def _merge_kernel(oa_ref, ob_ref, oc_ref, wb_ref, ga_ref, gb_ref, gc_ref, gbias_ref, o_ref):
    acc = None
    for i, (o_r, g_r) in enumerate(((oa_ref, ga_ref), (ob_ref, gb_ref), (oc_ref, gc_ref))):
        proj = jnp.dot(o_r[...], wb_ref[i], preferred_element_type=F32)
        term = _sigmoid(g_r[...] + gbias_ref[i]) * proj
        acc = term if acc is None else acc + term
    o_ref[...] = acc.astype(o_ref.dtype)


def _merge(oa, ob, oc, proj, w_branch, gate_b, tm=512, tn=512):
    t = oa.shape[0]
    tm = min(tm, t)
    gbase = COL_GATE * LANE // tn
    per = D_MODEL // tn
    ospec = pl.BlockSpec((tm, 1024), lambda i, j: (i, 0))
    gspec = lambda br: pl.BlockSpec((tm, tn), lambda i, j, br=br: (i, gbase + br * per + j))
    return pl.pallas_call(
        _merge_kernel,
        grid=(t // tm, per),
        in_specs=[ospec, ospec, ospec,
                  pl.BlockSpec((N_BRANCH, 1024, tn), lambda i, j: (0, 0, j)),
                  gspec(0), gspec(1), gspec(2),
                  pl.BlockSpec((N_BRANCH, 1, tn), lambda i, j: (0, 0, j))],
        out_specs=pl.BlockSpec((tm, tn), lambda i, j: (i, j)),
        out_shape=jax.ShapeDtypeStruct((t, D_MODEL), BF16),
        compiler_params=pltpu.CompilerParams(
            dimension_semantics=("arbitrary", "arbitrary"), vmem_limit_bytes=VMEM_LIMIT),
        name="merge",
    )(oa, ob, oc, w_branch, proj, proj, proj, gate_b)


def _outproj_kernel(m_ref, w_ref, x_ref, fw_ref, o_ref, *, final_norm):
    x = x_ref[...] + jnp.dot(m_ref[...], w_ref[...], preferred_element_type=F32)
    if final_norm:
        ms = jnp.mean(x * x, axis=-1, keepdims=True)
        x = x * lax.rsqrt(ms + NORM_EPS) * fw_ref[...]
    o_ref[...] = x


def _outproj(merged, w_out, x2, final_w_row, final_norm, tm=512):
    t = x2.shape[0]
    tm = min(tm, t)
    return pl.pallas_call(
        functools.partial(_outproj_kernel, final_norm=final_norm),
        grid=(t // tm,),
        in_specs=[pl.BlockSpec((tm, D_MODEL), lambda i: (i, 0)),
                  pl.BlockSpec((D_MODEL, D_MODEL), lambda i: (0, 0)),
                  pl.BlockSpec((tm, D_MODEL), lambda i: (i, 0)),
                  pl.BlockSpec((1, D_MODEL), lambda i: (0, 0))],
        out_specs=pl.BlockSpec((tm, D_MODEL), lambda i: (i, 0)),
        out_shape=jax.ShapeDtypeStruct((t, D_MODEL), F32),
        compiler_params=pltpu.CompilerParams(
            dimension_semantics=("arbitrary",), vmem_limit_bytes=VMEM_LIMIT),
        name="outproj",
    )(merged, w_out, x2, final_w_row)


def _pad_cols(a, width):
    return jnp.pad(a, [(0, 0)] * (a.ndim - 1) + [(0, width - a.shape[-1])])


def _pad_w_in(w):
    o_gba, o_rf = 4096, 4112
    o_wl, o_al, o_rz = o_rf + 3072, o_rf + 3072 + RWKV_LORA, o_rf + 3072 + 2 * RWKV_LORA
    pieces = [w[:, :o_gba],
              _pad_cols(w[:, o_gba:o_rf], LANE),
              _pad_cols(w[:, o_wl:o_al], LANE),
              _pad_cols(w[:, o_al:o_rz], LANE),
              jnp.zeros((w.shape[0], LANE), w.dtype),
              w[:, o_rf:o_wl],
              w[:, o_rz:]]
    return jnp.concatenate(pieces, axis=1).astype(BF16)


def _place_s5(a, rows_are_channels):
    g = a.shape[0]
    n_blk, n_pair = S5_WIDTH // LANE, LANE // (2 * S5_GROUP)
    grp_in_blk = LANE // S5_GROUP
    cs = a if not rows_are_channels else jnp.swapaxes(a, 1, 2)
    cs = cs.reshape(n_blk, n_pair, 2, S5_STATE, S5_GROUP)
    slot = jnp.arange(grp_in_blk)[None, :, None] == (2 * jnp.arange(n_pair)[:, None, None]
                                                      + jnp.arange(2)[None, None, :])
    placed = jnp.where(slot[None, :, :, None, :, None],
                       jnp.transpose(cs, (0, 1, 4, 2, 3))[:, :, None, :, :, :], 0.0)
    placed = placed.reshape(n_blk, n_pair, LANE, LANE)
    return placed if not rows_are_channels else jnp.swapaxes(placed, 2, 3)


def _s5_rows(a):
    return a.reshape(S5_WIDTH // LANE, LANE // (2 * S5_GROUP), 2 * S5_STATE)


def kernel(x, norm_w, w_in, gdn_conv_w, gdn_a_log, gdn_dt_bias, gdn_norm_w, rwkv_mu, rwkv_w0, rwkv_w_up,
           rwkv_a0, rwkv_a_up, rwkv_k_k, rwkv_k_a, rwkv_r_k, rwkv_lnx_w, rwkv_lnx_b, s5_a_re, s5_a_im,
           s5_log_dt, s5_b_re, s5_b_im, s5_c_re, s5_c_im, s5_d, s5_glu_w, s5_glu_b, gate_b, w_branch,
           w_out, final_norm_w):
    batch, seq, _ = x.shape
    depth = w_in.shape[0]
    x2 = x.reshape(batch * seq, D_MODEL)
    final_row = final_norm_w.reshape(1, D_MODEL)

    for i in range(depth):
        proj = _inproj(x2, norm_w[i].reshape(1, D_MODEL), _pad_w_in(w_in[i]))

        head_row = lambda a: _pad_cols(jnp.pad(a, (GDN_HEADS, 0)).reshape(1, 2 * GDN_HEADS), LANE)
        o_a = _gdn(proj, gdn_conv_w[i], head_row(gdn_a_log[i]), head_row(gdn_dt_bias[i]),
                   gdn_norm_w[i].reshape(1, GDN_HEAD_DIM), batch, seq)

        mu = rwkv_mu[i]
        row = lambda a: a.reshape(1, -1)
        lora = lambda a: jnp.pad(a, ((0, LANE - RWKV_LORA), (0, 0))).astype(BF16)
        rp = dict(mu_r=row(mu[:1024]), mu_k=row(mu[1024:2048]), mu_v=row(mu[2048:3072]),
                  mu_wl=_pad_cols(row(mu[3072:3072 + RWKV_LORA]), LANE),
                  mu_al=_pad_cols(row(mu[3072 + RWKV_LORA:]), LANE),
                  w0=row(rwkv_w0[i]), w_up=lora(rwkv_w_up[i]), a0=row(rwkv_a0[i]), a_up=lora(rwkv_a_up[i]),
                  k_k=row(rwkv_k_k[i]), k_a=row(rwkv_k_a[i]), r_k=row(rwkv_r_k[i]),
                  lnx_w=row(rwkv_lnx_w[i]), lnx_b=row(rwkv_lnx_b[i]))
        o_b = _rwkv(proj, rp, batch, seq)

        sp = dict(lam_re=_s5_rows(s5_a_re[i]), lam_im=_s5_rows(s5_a_im[i]),
                  log_dt=_s5_rows(jnp.broadcast_to(s5_log_dt[i][:, None], (S5_WIDTH // S5_GROUP, S5_STATE))),
                  b_re=_place_s5(s5_b_re[i], False), b_im=_place_s5(s5_b_im[i], False),
                  c_re=_place_s5(s5_c_re[i], True), c_im=_place_s5(s5_c_im[i], True),
                  d=s5_d[i].reshape(S5_WIDTH // LANE, 1, LANE))
        y_c = _s5(proj, sp, batch, seq)
        o_c = _s5_glu(y_c, proj, s5_glu_w[i].astype(BF16), s5_glu_b[i].reshape(1, S5_WIDTH))

        merged = _merge(o_a, o_b, o_c, proj, w_branch[i].astype(BF16),
                        gate_b[i].reshape(N_BRANCH, 1, D_MODEL))
        x2 = _outproj(merged, w_out[i].astype(BF16), x2, final_row, i == depth - 1)

    return x2.reshape(batch, seq, D_MODEL)
```

```python
import functools
import math

import jax
import jax.numpy as jnp
from jax import lax
from jax.experimental import pallas as pl
from jax.experimental.pallas import tpu as pltpu

F32 = jnp.float32
BF16 = jnp.bfloat16

D_MODEL = 2048
GDN_HEADS = 8
GDN_HEAD_DIM = 128
GDN_WIDTH = 1024
RWKV_HEAD_DIM = 64
RWKV_WIDTH = 1024
RWKV_LORA = 96
RWKV_LN_EPS = 64e-5
S5_GROUP = 16
S5_STATE = 64
S5_WIDTH = 1024
N_BRANCH = 3
NORM_EPS = 1e-6

LANE = 128
MXU_TILE = 256
HALO = 8
CHUNK = 64
SEQ_BLOCK = 512
S5_TILE = 128
VMEM_LIMIT = 52 * 1024 * 1024

COL_QKV = 0
COL_GZ = 24
COL_RKV = 32
COL_RZ = 56
COL_SU = 64
COL_SZ = 72
COL_GATE = 80
COL_GBA = 128
COL_WL = 129
COL_AL = 130
N_COLBLK = 132
NP = N_COLBLK * LANE


def _sigmoid(x):
    return 1.0 / (1.0 + jnp.exp(-x))


def _silu(x):
    return x * _sigmoid(x)


def _softplus(x):
    return jnp.maximum(x, 0.0) + jnp.log1p(jnp.exp(-jnp.abs(x)))


def _mm(a, b):
    return jnp.dot(a.astype(BF16), b.astype(BF16), preferred_element_type=F32)


def _mm_nt(a, b):
    return lax.dot_general(a.astype(BF16), b.astype(BF16), (((1,), (1,)), ((), ())),
                           preferred_element_type=F32)


def _mm_tn(a, b):
    return lax.dot_general(a.astype(BF16), b.astype(BF16), (((0,), (0,)), ((), ())),
                           preferred_element_type=F32)


def _split3(x):
    x1 = x.astype(BF16)
    r1 = x - x1.astype(F32)
    x2 = r1.astype(BF16)
    x3 = (r1 - x2.astype(F32)).astype(BF16)
    return x1, x2, x3


def _mm_sel_lhs(sel, x):
    s = sel.astype(BF16)
    out = None
    for part in _split3(x):
        t = jnp.dot(s, part, preferred_element_type=F32)
        out = t if out is None else out + t
    return out


def _mm_sel_rhs(x, sel):
    s = sel.astype(BF16)
    out = None
    for part in _split3(x):
        t = jnp.dot(part, s, preferred_element_type=F32)
        out = t if out is None else out + t
    return out


def _unit_lower_inverse(x, eye, size, out):
    p = eye + x
    y = x
    span = 2
    while span < size:
        y = _mm(y, y)
        yield
        p = p + _mm(p, y)
        yield
        span *= 2
    out.append(p)


def _run_lockstep(chains):
    live = list(chains)
    while live:
        for g in list(live):
            try:
                next(g)
            except StopIteration:
                live.remove(g)


def _iota2(shape, dim):
    return lax.broadcasted_iota(jnp.int32, shape, dim)


def _chunk_and_shifts(ref, halo_ref, c, rows, shift):
    r0 = pl.multiple_of(c * rows, rows)
    h0 = pl.multiple_of(jnp.maximum(r0 - HALO, 0), HALO)
    halo = jnp.where(c == 0, halo_ref[...], ref[pl.ds(h0, HALO), :])
    cur = ref[pl.ds(r0, rows), :]
    xc = jnp.concatenate([halo, cur], axis=0)
    return cur, [pltpu.roll(xc, s, 0)[HALO:, :] for s in range(1, shift + 1)]


def _recurrence_grid_setup(state_ref, halo_refs):
    @pl.when(pl.program_id(2) == 0)
    def _():
        state_ref[...] = jnp.zeros_like(state_ref)
        for h in halo_refs:
            h[...] = jnp.zeros_like(h)


def _save_halos(pairs, rows):
    for src, dst in pairs:
        dst[...] = src[rows - HALO:rows, :]


def _inproj_kernel(x_ref, nw_ref, w_ref, o_ref, h_ref):
    @pl.when(pl.program_id(1) == 0)
    def _():
        x = x_ref[...]
        ms = jnp.mean(x * x, axis=-1, keepdims=True)
        h_ref[...] = (x * lax.rsqrt(ms + NORM_EPS) * nw_ref[...]).astype(BF16)

    o_ref[...] = jnp.dot(h_ref[...], w_ref[...], preferred_element_type=F32)


def _inproj(x2, norm_w_row, w_pad, tm=1024, tn=512):
    t = x2.shape[0]
    tm = min(tm, t)
    return pl.pallas_call(
        _inproj_kernel,
        grid=(t // tm, NP // tn),
        in_specs=[pl.BlockSpec((tm, D_MODEL), lambda i, j: (i, 0)),
                  pl.BlockSpec((1, D_MODEL), lambda i, j: (0, 0)),
                  pl.BlockSpec((D_MODEL, tn), lambda i, j: (0, j))],
        out_specs=pl.BlockSpec((tm, tn), lambda i, j: (i, j)),
        out_shape=jax.ShapeDtypeStruct((t, NP), F32),
        scratch_shapes=[pltpu.VMEM((tm, D_MODEL), BF16)],
        compiler_params=pltpu.CompilerParams(
            dimension_semantics=("arbitrary", "arbitrary"), vmem_limit_bytes=VMEM_LIMIT),
        name="inproj",
    )(x2, norm_w_row, w_pad)


def _gdn_kernel(q_ref, k_ref, v_ref, z_ref, ba_ref, cq_ref, ck_ref, cv_ref,
                alog_ref, dtb_ref, nw_ref, o_ref, s_ref, hq_ref, hk_ref, hv_ref, *, blk_rows, n_tile):
    c_len = CHUNK
    rows = 2 * c_len
    width = MXU_TILE
    tile0 = pl.program_id(1) * n_tile

    lane_w = _iota2((1, width), 1)
    lane_b = _iota2((1, LANE), 1)
    ri = _iota2((rows, rows), 0)
    ci = _iota2((rows, rows), 1)
    same_head = (ri // c_len) == (ci // c_len)
    strict = same_head & (ci < ri)
    incl = same_head & (ci <= ri)
    tril_sel = jnp.where(incl, 1.0, 0.0)
    triu_sel = jnp.where(same_head & (ri <= ci), 1.0, 0.0)
    head_sel = jnp.where(same_head, 1.0, 0.0)
    ones_sel = jnp.ones((rows, rows), F32)
    eye = jnp.where(ri == ci, 1.0, 0.0)
    head0_lane = lane_w < GDN_HEAD_DIM

    _recurrence_grid_setup(s_ref, (hq_ref, hk_ref, hv_ref))
    alog_row = alog_ref[...]
    dtb_row = dtb_ref[...]
    nw_row = nw_ref[...]

    def stack(x):
        return jnp.concatenate([jnp.where(head0_lane, x, 0.0), jnp.where(head0_lane, 0.0, x)], axis=0)

    def per_head(x, fn):
        return jnp.concatenate([fn(x[:, :GDN_HEAD_DIM]), fn(x[:, GDN_HEAD_DIM:])], axis=1)

    def l2n(xh):
        return xh * lax.rsqrt(jnp.sum(xh * xh, axis=-1, keepdims=True) + 1e-6)

    def rms(oh):
        return oh * lax.rsqrt(jnp.mean(oh * oh, axis=-1, keepdims=True) + NORM_EPS) * nw_row

    def body(c, carry):
        r0 = pl.multiple_of(c * c_len, c_len)

        def conv(x_ref, h_ref, cw_ref):
            cur, sh = _chunk_and_shifts(x_ref, h_ref, c, c_len, 3)
            cw = cw_ref[...]
            acc = cur * cw[3:4, :]
            for s in range(1, 4):
                acc = acc + sh[s - 1] * cw[3 - s:4 - s, :]
            return _silu(acc)

        q_all = conv(q_ref, hq_ref, cq_ref)
        k_all = conv(k_ref, hk_ref, ck_ref)
        v_all = conv(v_ref, hv_ref, cv_ref)
        z_all = z_ref[pl.ds(r0, c_len), :]
        ba = ba_ref[pl.ds(r0, c_len), :]
        g_all = -jnp.exp(alog_row) * _softplus(ba + dtb_row)

        def chain(t):
            sl = slice(t * width, (t + 1) * width)
            q = per_head(q_all[:, sl], l2n) * (GDN_HEAD_DIM ** -0.5)
            k = per_head(k_all[:, sl], l2n)
            v = v_all[:, sl]
            betas, gs = [], []
            for h in range(2):
                hid = 2 * (tile0 + t) + h
                betas.append(_sigmoid(jnp.sum(jnp.where(lane_b == hid, ba, 0.0), axis=-1, keepdims=True)))
                gs.append(jnp.sum(jnp.where(lane_b == GDN_HEADS + hid, g_all, 0.0), axis=-1, keepdims=True))
            beta = jnp.concatenate(betas, axis=0)
            gb = jnp.broadcast_to(jnp.concatenate(gs, axis=0), (rows, width))

            gcol = _mm_sel_lhs(tril_sel, gb)
            grow = _mm_sel_lhs(ones_sel, gb[:, :rows] * triu_sel)
            glast = _mm_sel_lhs(head_sel, gb)
            yield
            diff = gcol[:, :rows] - grow
            d_strict = jnp.where(strict, jnp.exp(jnp.where(strict, diff, 0.0)), 0.0)
            d_incl = jnp.where(incl, jnp.exp(jnp.where(incl, diff, 0.0)), 0.0)
            egc = jnp.exp(gcol)

            sq, sk, sv = stack(q), stack(k), stack(v)
            skb = sk * beta
            lower = _mm_nt(skb, sk) * d_strict
            attn = _mm_nt(sq, sk) * d_incl
            yield
            inv = []
            yield from _unit_lower_inverse(-lower, eye, c_len, inv)
            t_mat = inv[0]
            u = _mm(t_mat, sv * beta)
            w = _mm(t_mat, skb * egc)
            yield

            state = s_ref[t]
            v_new = u - _mm(w, state)
            yield
            o_st = _mm(sq * egc, state) + _mm(attn, v_new)
            gl_lane = jnp.where(head0_lane, glast[0:1, :], glast[c_len:c_len + 1, :])
            s_ref[t] = state * jnp.exp(gl_lane) + _mm_tn(sk * jnp.exp(glast - gcol), v_new)
            yield

            o = o_st[:c_len, :] + o_st[c_len:, :]
            o_ref[pl.ds(r0, c_len), sl] = (per_head(o, rms) * _silu(z_all[:, sl])).astype(o_ref.dtype)

        _run_lockstep([chain(t) for t in range(n_tile)])
        return carry

    lax.fori_loop(0, blk_rows // c_len, body, 0)
    _save_halos(((q_ref, hq_ref), (k_ref, hk_ref), (v_ref, hv_ref)), blk_rows)


def _gdn(proj, conv_w, alog_row, dtb_row, nw_row, batch, seq, n_tile=4):
    w = n_tile * MXU_TILE
    n_grp = GDN_WIDTH // w
    blk = min(SEQ_BLOCK, seq)
    nsb = seq // blk
    colspec = lambda base: pl.BlockSpec((blk, w), lambda b, g, s, base=base: (b * nsb + s, base * n_grp + g))
    cwspec = lambda base: pl.BlockSpec((4, w), lambda b, g, s, base=base: (0, base * n_grp + g))
    rowspec = pl.BlockSpec((1, LANE), lambda b, g, s: (0, 0))
    return pl.pallas_call(
        functools.partial(_gdn_kernel, blk_rows=blk, n_tile=n_tile),
        grid=(batch, n_grp, nsb),
        in_specs=[colspec(0), colspec(1), colspec(2), colspec(COL_GZ * LANE // GDN_WIDTH),
                  pl.BlockSpec((blk, LANE), lambda b, g, s: (b * nsb + s, COL_GBA)),
                  cwspec(0), cwspec(1), cwspec(2), rowspec, rowspec, rowspec],
        out_specs=pl.BlockSpec((blk, w), lambda b, g, s: (b * nsb + s, g)),
        out_shape=jax.ShapeDtypeStruct((batch * seq, GDN_WIDTH), BF16),
        scratch_shapes=[pltpu.VMEM((n_tile, MXU_TILE, MXU_TILE), F32)] + [pltpu.VMEM((HALO, w), F32)] * 3,
        compiler_params=pltpu.CompilerParams(
            dimension_semantics=("arbitrary", "arbitrary", "arbitrary"), vmem_limit_bytes=VMEM_LIMIT),
        name="gdn",
    )(proj, proj, proj, proj, proj, conv_w, conv_w, conv_w, alog_row, dtb_row, nw_row)


def _rwkv_kernel(r_ref, k_ref, v_ref, wl_ref, al_ref, z_ref,
                 mur_ref, muk_ref, muv_ref, muwl_ref, mual_ref,
                 w0_ref, wup_ref, a0_ref, aup_ref, kk_ref, ka_ref, rk_ref, lnw_ref, lnb_ref,
                 o_ref, n_ref, hr_ref, hk_ref, hv_ref, hwl_ref, hal_ref, *, blk_rows, n_tile):
    c_len = CHUNK
    nh = MXU_TILE // RWKV_HEAD_DIM
    width = MXU_TILE
    rows = nh * c_len

    ri = _iota2((rows, rows), 0)
    ci = _iota2((rows, rows), 1)
    same_head = (ri // c_len) == (ci // c_len)
    strict = same_head & (ci < ri)
    incl = same_head & (ci <= ri)
    eye = jnp.where(ri == ci, 1.0, 0.0)
    head_sum = jnp.where(same_head, 1.0, 0.0)
    ti = _iota2((c_len, c_len), 0)
    tj = _iota2((c_len, c_len), 1)
    tril_t = jnp.where(tj <= ti, 1.0, 0.0)
    lane_head = _iota2((1, width), 1) // RWKV_HEAD_DIM

    _recurrence_grid_setup(n_ref, (hr_ref, hk_ref, hv_ref, hwl_ref, hal_ref))

    def stack(x):
        return jnp.concatenate([jnp.where(lane_head == h, x, 0.0) for h in range(nh)], axis=0)

    def unstack(y):
        out = y[:c_len, :]
        for h in range(1, nh):
            out = out + y[h * c_len:(h + 1) * c_len, :]
        return out

    def body(c, carry):
        r0 = pl.multiple_of(c * c_len, c_len)

        def shifted(x_ref, h_ref, mu_ref):
            cur, sh = _chunk_and_shifts(x_ref, h_ref, c, c_len, 1)
            return cur + (sh[0] - cur) * mu_ref[...]

        r_all = shifted(r_ref, hr_ref, mur_ref)
        k_all = shifted(k_ref, hk_ref, muk_ref)
        v_all = shifted(v_ref, hv_ref, muv_ref)
        wl = shifted(wl_ref, hwl_ref, muwl_ref)
        al = shifted(al_ref, hal_ref, mual_ref)
        z_all = z_ref[pl.ds(r0, c_len), :]

        w_pre = w0_ref[...] + _mm(jnp.tanh(wl), wup_ref[...])
        logw_all = -jnp.exp(-_softplus(-w_pre) - 0.5)
        a_all = _sigmoid(a0_ref[...] + _mm(al, aup_ref[...]))
        kkx_all = k_all * kk_ref[...]
        k2_all = k_all * (1.0 + (a_all - 1.0) * ka_ref[...])
        lc_all = _mm_sel_lhs(tril_t, logw_all)
        bonus_all = r_all * k2_all * rk_ref[...]

        def chain(t):
            sl = slice(t * width, (t + 1) * width)
            r, v, a, k2, logw, lc = r_all[:, sl], v_all[:, sl], a_all[:, sl], k2_all[:, sl], logw_all[:, sl], lc_all[:, sl]
            kkx = kkx_all[:, sl]
            kk = kkx * lax.rsqrt(_mm_sel_rhs(kkx * kkx, head_sum) + 1e-6)
            yield

            lc_last = lc[c_len - 1:c_len, :]
            e_neg = jnp.exp(-lc)
            e_rem = jnp.exp(lc_last - lc)
            kka = kk * a
            a_t = stack(-kk * jnp.exp(lc - logw))
            r_t = stack(r * jnp.exp(lc))
            b_t = stack(kka * e_neg)
            k_t = stack(k2 * e_neg)
            b_g = stack(kka * e_rem)
            k_g = stack(k2 * e_rem)
            sv = stack(v)

            prod = _mm_nt(jnp.concatenate([a_t, r_t], axis=0), jnp.concatenate([b_t, k_t], axis=0))
            yield
            a_ab = jnp.where(strict, prod[:rows, :rows], 0.0)
            a_ak = jnp.where(strict, prod[:rows, rows:], 0.0)
            a_rb = jnp.where(incl, prod[rows:, :rows], 0.0)
            a_rk = jnp.where(incl, prod[rows:, rows:], 0.0)

            inv = []
            yield from _unit_lower_inverse(a_ab, eye, c_len, inv)
            t_mat = inv[0]
            w_p = _mm(t_mat, a_t)
            aks = _mm(a_ak, sv)
            yield
            v_p = _mm(t_mat, aks)
            yield

            state = n_ref[t]
            x0 = _mm(jnp.concatenate([w_p, r_t], axis=0), state)
            yield
            u = x0[:rows, :] + v_p
            y_st = x0[rows:, :] + _mm(a_rb, u) + _mm(a_rk, sv)
            gam_col = jnp.transpose(jnp.broadcast_to(jnp.exp(lc_last), (width, width)))
            n_ref[t] = gam_col * state + _mm_tn(b_g, u) + _mm_tn(k_g, sv)
            yield

            y = unstack(y_st)
            inv_n = 1.0 / RWKV_HEAD_DIM
            mean = _mm_sel_rhs(y, head_sum) * inv_n
            yield
            dlt = y - mean
            var = _mm_sel_rhs(dlt * dlt, head_sum) * inv_n
            yield
            y = dlt * lax.rsqrt(var + RWKV_LN_EPS) * lnw_ref[:, sl] + lnb_ref[:, sl]
            y = y + _mm_sel_rhs(bonus_all[:, sl], head_sum) * v
            o_ref[pl.ds(r0, c_len), sl] = (y * _silu(z_all[:, sl])).astype(o_ref.dtype)

        _run_lockstep([chain(t) for t in range(n_tile)])
        return carry

    lax.fori_loop(0, blk_rows // c_len, body, 0)
    _save_halos(((r_ref, hr_ref), (k_ref, hk_ref), (v_ref, hv_ref), (wl_ref, hwl_ref), (al_ref, hal_ref)),
                blk_rows)


def _rwkv(proj, p, batch, seq, n_tile=4):
    w = n_tile * MXU_TILE
    n_grp = RWKV_WIDTH // w
    blk = min(SEQ_BLOCK, seq)
    nsb = seq // blk
    colspec = lambda base: pl.BlockSpec((blk, w), lambda b, g, s, base=base: (b * nsb + s, base * n_grp + g))
    lspec = lambda col: pl.BlockSpec((blk, LANE), lambda b, g, s, col=col: (b * nsb + s, col))
    grow = pl.BlockSpec((1, w), lambda b, g, s: (0, g))
    lrow = pl.BlockSpec((1, LANE), lambda b, g, s: (0, 0))
    upspec = pl.BlockSpec((LANE, w), lambda b, g, s: (0, g))
    base = COL_RKV * LANE // RWKV_WIDTH
    return pl.pallas_call(
        functools.partial(_rwkv_kernel, blk_rows=blk, n_tile=n_tile),
        grid=(batch, n_grp, nsb),
        in_specs=[colspec(base), colspec(base + 1), colspec(base + 2), lspec(COL_WL), lspec(COL_AL),
                  colspec(COL_RZ * LANE // RWKV_WIDTH),
                  grow, grow, grow, lrow, lrow,
                  grow, upspec, grow, upspec, grow, grow, grow, grow, grow],
        out_specs=pl.BlockSpec((blk, w), lambda b, g, s: (b * nsb + s, g)),
        out_shape=jax.ShapeDtypeStruct((batch * seq, RWKV_WIDTH), BF16),
        scratch_shapes=([pltpu.VMEM((n_tile, MXU_TILE, MXU_TILE), F32)] + [pltpu.VMEM((HALO, w), F32)] * 3
                        + [pltpu.VMEM((HALO, LANE), F32)] * 2),
        compiler_params=pltpu.CompilerParams(
            dimension_semantics=("arbitrary", "arbitrary", "arbitrary"), vmem_limit_bytes=VMEM_LIMIT),
        name="rwkv",
    )(proj, proj, proj, proj, proj, proj,
      p["mu_r"], p["mu_k"], p["mu_v"], p["mu_wl"], p["mu_al"],
      p["w0"], p["w_up"], p["a0"], p["a_up"], p["k_k"], p["k_a"], p["r_k"], p["lnx_w"], p["lnx_b"])


def _s5_kernel(u_ref, lre_ref, lim_ref, ldt_ref, bre_ref, bim_ref, cre_ref, cim_ref, d_ref,
               o_ref, pre_ref, pim_ref, w_ref, *, seq):
    tile = S5_TILE
    n_pair = LANE // (2 * S5_GROUP)
    row = _iota2((tile, LANE), 0)

    def cmul(ar, ai, br, bi):
        return ar * br - ai * bi, ar * bi + ai * br

    def shift(x, d):
        if d % 8 == 0:
            return jnp.concatenate([jnp.zeros((d, LANE), F32), x[:tile - d, :]], axis=0)
        return jnp.where(row >= d, pltpu.roll(x, d, 0), 0.0)

    def scan(sr, si, ar, ai, out):
        d = 1
        while d < tile:
            tr, ti = cmul(ar, ai, shift(sr, d), shift(si, d))
            sr, si = sr + tr, si + ti
            ar, ai = cmul(ar, ai, ar, ai)
            d *= 2
            yield
        out.extend((sr, si))

    ab = []
    for p in range(n_pair):
        lre = lre_ref[0, p:p + 1, :]
        lim = lim_ref[0, p:p + 1, :]
        dt = jnp.exp(ldt_ref[0, p:p + 1, :])
        mag = jnp.exp(lre * dt)
        ab_re = mag * jnp.cos(lim * dt)
        ab_im = mag * jnp.sin(lim * dt)
        den = lre * lre + lim * lim
        coef_re = ((ab_re - 1.0) * lre + ab_im * lim) / den
        coef_im = (ab_im * lre - (ab_re - 1.0) * lim) / den
        b_re = bre_ref[0, p]
        b_im = bim_ref[0, p]
        w_ref[p, 0] = (coef_re * b_re - coef_im * b_im).astype(BF16)
        w_ref[p, 1] = (coef_re * b_im + coef_im * b_re).astype(BF16)
        w_ref[p, 2] = cre_ref[0, p].astype(BF16)
        w_ref[p, 3] = cim_ref[0, p].astype(BF16)
        imp_re = jnp.where(row == 0, jnp.broadcast_to(ab_re, (tile, LANE)), 0.0)
        imp_im = jnp.where(row == 0, jnp.broadcast_to(ab_im, (tile, LANE)), 0.0)
        pw = []
        for _ in scan(imp_re, imp_im, ab_re, ab_im, pw):
            pass
        pre_ref[p] = pw[0]
        pim_ref[p] = pw[1]
        ab.append((ab_re, ab_im))

    def tile_body(i, st):
        t0 = pl.multiple_of(i * tile, tile)
        u = u_ref[pl.ds(t0, tile), :]
        ub = u.astype(BF16)
        ys, new_st = [None] * n_pair, [None] * n_pair

        def chain(p):
            cr, ci = st[p]
            sr = jnp.dot(ub, w_ref[p, 0], preferred_element_type=F32)
            si = jnp.dot(ub, w_ref[p, 1], preferred_element_type=F32)
            yield
            res = []
            yield from scan(sr, si, ab[p][0], ab[p][1], res)
            tr, ti = cmul(pre_ref[p], pim_ref[p], cr, ci)
            sr, si = res[0] + tr, res[1] + ti
            new_st[p] = (sr[tile - 1:tile, :], si[tile - 1:tile, :])
            ys[p] = jnp.dot(sr.astype(BF16), w_ref[p, 2], preferred_element_type=F32) - jnp.dot(
                si.astype(BF16), w_ref[p, 3], preferred_element_type=F32)

        _run_lockstep([chain(p) for p in range(n_pair)])
        y = d_ref[0] * u
        for p in range(n_pair):
            y = y + ys[p]
        inner = math.sqrt(2.0 / math.pi) * (y + 0.044715 * (y * y * y))
        o_ref[pl.ds(t0, tile), :] = 0.5 * y * (1.0 + jnp.tanh(inner))
        return tuple(new_st)

    zero = jnp.zeros((1, LANE), F32)
    lax.fori_loop(0, seq // tile, tile_body, tuple((zero, zero) for _ in range(n_pair)))


def _s5(proj, p, batch, seq):
    n_blk = S5_WIDTH // LANE
    n_pair = LANE // (2 * S5_GROUP)
    rowspec = pl.BlockSpec((1, n_pair, LANE), lambda b, j: (j, 0, 0))
    matspec = pl.BlockSpec((1, n_pair, LANE, LANE), lambda b, j: (j, 0, 0, 0))
    return pl.pallas_call(
        functools.partial(_s5_kernel, seq=seq),
        grid=(batch, n_blk),
        in_specs=[pl.BlockSpec((seq, LANE), lambda b, j: (b, COL_SU + j)),
                  rowspec, rowspec, rowspec, matspec, matspec, matspec, matspec,
                  pl.BlockSpec((1, 1, LANE), lambda b, j: (j, 0, 0))],
        out_specs=pl.BlockSpec((seq, LANE), lambda b, j: (b, j)),
        out_shape=jax.ShapeDtypeStruct((batch * seq, S5_WIDTH), F32),
        scratch_shapes=[pltpu.VMEM((n_pair, S5_TILE, LANE), F32), pltpu.VMEM((n_pair, S5_TILE, LANE), F32),
                        pltpu.VMEM((n_pair, 4, LANE, LANE), BF16)],
        compiler_params=pltpu.CompilerParams(
            dimension_semantics=("arbitrary", "arbitrary"), vmem_limit_bytes=VMEM_LIMIT),
        name="s5_scan",
    )(proj, p["lam_re"], p["lam_im"], p["log_dt"], p["b_re"], p["b_im"], p["c_re"], p["c_im"], p["d"])


def _s5_glu_kernel(y_ref, w_ref, b_ref, z_ref, o_ref):
    y = y_ref[...]
    gate = _sigmoid(_mm(y, w_ref[...]) + b_ref[...])
    o_ref[...] = (y * gate * _silu(z_ref[...])).astype(o_ref.dtype)


def _s5_glu(yc, proj, glu_w, glu_b, tm=512):
    t = yc.shape[0]
    tm = min(tm, t)
    return pl.pallas_call(
        _s5_glu_kernel,
        grid=(t // tm,),
        in_specs=[pl.BlockSpec((tm, S5_WIDTH), lambda i: (i, 0)),
                  pl.BlockSpec((S5_WIDTH, S5_WIDTH), lambda i: (0, 0)),
                  pl.BlockSpec((1, S5_WIDTH), lambda i: (0, 0)),
                  pl.BlockSpec((tm, S5_WIDTH), lambda i: (i, COL_SZ * LANE // S5_WIDTH))],
        out_specs=pl.BlockSpec((tm, S5_WIDTH), lambda i: (i, 0)),
        out_shape=jax.ShapeDtypeStruct((t, S5_WIDTH), BF16),
        compiler_params=pltpu.CompilerParams(
            dimension_semantics=("arbitrary",), vmem_limit_bytes=VMEM_LIMIT),
        name="s5_glu",
    )(yc, glu_w, glu_b, proj)


def _merge_kernel(oa_ref, ob_ref, oc_ref, wb_ref, ga_ref, gb_ref, gc_ref, gbias_ref, o_ref):
    acc = None
    for i, (o_r, g_r) in enumerate(((oa_ref, ga_ref), (ob_ref, gb_ref), (oc_ref, gc_ref))):
        proj = jnp.dot(o_r[...], wb_ref[i], preferred_element_type=F32)
        term = _sigmoid(g_r[...] + gbias_ref[i]) * proj
        acc = term if acc is None else acc + term
    o_ref[...] = acc.astype(o_ref.dtype)


def _merge(oa, ob, oc, proj, w_branch, gate_b, tm=512, tn=512):
    t = oa.shape[0]
    tm = min(tm, t)
    gbase = COL_GATE * LANE // tn
    per = D_MODEL // tn
    ospec = pl.BlockSpec((tm, 1024), lambda i, j: (i, 0))
    gspec = lambda br: pl.BlockSpec((tm, tn), lambda i, j, br=br: (i, gbase + br * per + j))
    return pl.pallas_call(
        _merge_kernel,
        grid=(t // tm, per),
        in_specs=[ospec, ospec, ospec,
                  pl.BlockSpec((N_BRANCH, 1024, tn), lambda i, j: (0, 0, j)),
                  gspec(0), gspec(1), gspec(2),
                  pl.BlockSpec((N_BRANCH, 1, tn), lambda i, j: (0, 0, j))],
        out_specs=pl.BlockSpec((tm, tn), lambda i, j: (i, j)),
        out_shape=jax.ShapeDtypeStruct((t, D_MODEL), BF16),
        compiler_params=pltpu.CompilerParams(
            dimension_semantics=("arbitrary", "arbitrary"), vmem_limit_bytes=VMEM_LIMIT),
        name="merge",
    )(oa, ob, oc, w_branch, proj, proj, proj, gate_b)


def _outproj_kernel(m_ref, w_ref, x_ref, fw_ref, o_ref, *, final_norm):
    x = x_ref[...] + jnp.dot(m_ref[...], w_ref[...], preferred_element_type=F32)
    if final_norm:
        ms = jnp.mean(x * x, axis=-1, keepdims=True)
        x = x * lax.rsqrt(ms + NORM_EPS) * fw_ref[...]
    o_ref[...] = x


def _outproj(merged, w_out, x2, final_w_row, final_norm, tm=512):
    t = x2.shape[0]
    tm = min(tm, t)
    return pl.pallas_call(
        functools.partial(_outproj_kernel, final_norm=final_norm),
        grid=(t // tm,),
        in_specs=[pl.BlockSpec((tm, D_MODEL), lambda i: (i, 0)),
                  pl.BlockSpec((D_MODEL, D_MODEL), lambda i: (0, 0)),
                  pl.BlockSpec((tm, D_MODEL), lambda i: (i, 0)),
                  pl.BlockSpec((1, D_MODEL), lambda i: (0, 0))],
        out_specs=pl.BlockSpec((tm, D_MODEL), lambda i: (i, 0)),
        out_shape=jax.ShapeDtypeStruct((t, D_MODEL), F32),
        compiler_params=pltpu.CompilerParams(
            dimension_semantics=("arbitrary",), vmem_limit_bytes=VMEM_LIMIT),
        name="outproj",
    )(merged, w_out, x2, final_w_row)


def _pad_cols(a, width):
    return jnp.pad(a, [(0, 0)] * (a.ndim - 1) + [(0, width - a.shape[-1])])


def _pad_proj_cols(w):
    o_gba, o_rf = 4096, 4112
    o_wl, o_al, o_rz = o_rf + 3072, o_rf + 3072 + RWKV_LORA, o_rf + 3072 + 2 * RWKV_LORA
    pieces = [w[:, :o_gba],
              w[:, o_rf:o_wl],
              w[:, o_rz:],
              _pad_cols(w[:, o_gba:o_rf], LANE),
              _pad_cols(w[:, o_wl:o_al], LANE),
              _pad_cols(w[:, o_al:o_rz], LANE),
              jnp.zeros((w.shape[0], LANE), w.dtype)]
    return jnp.concatenate(pieces, axis=1)


def _place_s5(a, rows_are_channels):
    n_blk, n_pair = S5_WIDTH // LANE, LANE // (2 * S5_GROUP)
    grp_in_blk = LANE // S5_GROUP
    cs = a if not rows_are_channels else jnp.swapaxes(a, 1, 2)
    cs = cs.reshape(n_blk, n_pair, 2, S5_STATE, S5_GROUP)
    slot = jnp.arange(grp_in_blk)[None, :, None] == (2 * jnp.arange(n_pair)[:, None, None]
                                                      + jnp.arange(2)[None, None, :])
    placed = jnp.where(slot[None, :, :, None, :, None],
                       jnp.transpose(cs, (0, 1, 4, 2, 3))[:, :, None, :, :, :], 0.0)
    placed = placed.reshape(n_blk, n_pair, LANE, LANE)
    return placed if not rows_are_channels else jnp.swapaxes(placed, 2, 3)


def _s5_rows(a):
    return a.reshape(S5_WIDTH // LANE, LANE // (2 * S5_GROUP), 2 * S5_STATE)


def _gdn_head_row(a):
    return _pad_cols(jnp.pad(a, (GDN_HEADS, 0)).reshape(1, 2 * GDN_HEADS), LANE)


def _rwkv_params(mu, w0, w_up, a0, a_up, k_k, k_a, r_k, lnx_w, lnx_b):
    row = lambda a: a.reshape(1, -1)
    lora = lambda a: jnp.pad(a, ((0, LANE - RWKV_LORA), (0, 0))).astype(BF16)
    w3 = 3 * RWKV_WIDTH
    return dict(mu_r=row(mu[:RWKV_WIDTH]), mu_k=row(mu[RWKV_WIDTH:2 * RWKV_WIDTH]), mu_v=row(mu[2 * RWKV_WIDTH:w3]),
                mu_wl=_pad_cols(row(mu[w3:w3 + RWKV_LORA]), LANE), mu_al=_pad_cols(row(mu[w3 + RWKV_LORA:]), LANE),
                w0=row(w0), w_up=lora(w_up), a0=row(a0), a_up=lora(a_up), k_k=row(k_k), k_a=row(k_a),
                r_k=row(r_k), lnx_w=row(lnx_w), lnx_b=row(lnx_b))


def _s5_params(a_re, a_im, log_dt, b_re, b_im, c_re, c_im, d):
    return dict(lam_re=_s5_rows(a_re), lam_im=_s5_rows(a_im),
                log_dt=_s5_rows(jnp.broadcast_to(log_dt[:, None], (S5_WIDTH // S5_GROUP, S5_STATE))),
                b_re=_place_s5(b_re, False), b_im=_place_s5(b_im, False),
                c_re=_place_s5(c_re, True), c_im=_place_s5(c_im, True),
                d=d.reshape(S5_WIDTH // LANE, 1, LANE))


def kernel(x, norm_w, w_in, gdn_conv_w, gdn_a_log, gdn_dt_bias, gdn_norm_w, rwkv_mu, rwkv_w0, rwkv_w_up,
           rwkv_a0, rwkv_a_up, rwkv_k_k, rwkv_k_a, rwkv_r_k, rwkv_lnx_w, rwkv_lnx_b, s5_a_re, s5_a_im,
           s5_log_dt, s5_b_re, s5_b_im, s5_c_re, s5_c_im, s5_d, s5_glu_w, s5_glu_b, gate_b, w_branch,
           w_out, final_norm_w):
    batch, seq, _ = x.shape
    depth = w_in.shape[0]
    x2 = x.reshape(batch * seq, D_MODEL)
    final_row = final_norm_w.reshape(1, D_MODEL)

    for i in range(depth):
        proj = _inproj(x2, norm_w[i].reshape(1, D_MODEL), _pad_proj_cols(w_in[i]).astype(BF16))

        o_a = _gdn(proj, gdn_conv_w[i], _gdn_head_row(gdn_a_log[i]), _gdn_head_row(gdn_dt_bias[i]),
                   gdn_norm_w[i].reshape(1, GDN_HEAD_DIM), batch, seq)

        rp = _rwkv_params(rwkv_mu[i], rwkv_w0[i], rwkv_w_up[i], rwkv_a0[i], rwkv_a_up[i], rwkv_k_k[i],
                          rwkv_k_a[i], rwkv_r_k[i], rwkv_lnx_w[i], rwkv_lnx_b[i])
        o_b = _rwkv(proj, rp, batch, seq)

        sp = _s5_params(s5_a_re[i], s5_a_im[i], s5_log_dt[i], s5_b_re[i], s5_b_im[i], s5_c_re[i], s5_c_im[i],
                        s5_d[i])
        y_c = _s5(proj, sp, batch, seq)
        o_c = _s5_glu(y_c, proj, s5_glu_w[i].astype(BF16), s5_glu_b[i].reshape(1, S5_WIDTH))

        merged = _merge(o_a, o_b, o_c, proj, w_branch[i].astype(BF16),
                        gate_b[i].reshape(N_BRANCH, 1, D_MODEL))
        x2 = _outproj(merged, w_out[i].astype(BF16), x2, final_row, i == depth - 1)

    return x2.reshape(batch, seq, D_MODEL)
```

```python
import functools
import math

import jax
import jax.numpy as jnp
from jax import lax
from jax.experimental import pallas as pl
from jax.experimental.pallas import tpu as pltpu

F32 = jnp.float32
BF16 = jnp.bfloat16

D_MODEL = 2048
GDN_HEADS = 8
GDN_HEAD_DIM = 128
GDN_WIDTH = 1024
RWKV_HEAD_DIM = 64
RWKV_WIDTH = 1024
RWKV_LORA = 96
RWKV_LN_EPS = 64e-5
S5_GROUP = 16
S5_STATE = 64
S5_WIDTH = 1024
N_BRANCH = 3
NORM_EPS = 1e-6

LANE = 128
MXU_TILE = 256
HALO = 8
CHUNK = 64
SEQ_BLOCK = 512
S5_TILE = 128
VMEM_LIMIT = 52 * 1024 * 1024

COL_QKV = 0
COL_GZ = 24
COL_RKV = 32
COL_RZ = 56
COL_SU = 64
COL_SZ = 72
COL_GATE = 80
COL_GBA = 128
COL_WL = 129
COL_AL = 130
N_COLBLK = 132
NP = N_COLBLK * LANE


def _sigmoid(x):
    return 1.0 / (1.0 + jnp.exp(-x))


def _silu(x):
    return x * _sigmoid(x)


def _softplus(x):
    return jnp.maximum(x, 0.0) + jnp.log1p(jnp.exp(-jnp.abs(x)))


def _mm(a, b):
    return jnp.dot(a.astype(BF16), b.astype(BF16), preferred_element_type=F32)


def _mm_nt(a, b):
    return lax.dot_general(a.astype(BF16), b.astype(BF16), (((1,), (1,)), ((), ())),
                           preferred_element_type=F32)


def _mm_tn(a, b):
    return lax.dot_general(a.astype(BF16), b.astype(BF16), (((0,), (0,)), ((), ())),
                           preferred_element_type=F32)


def _split_bf16(x, parts):
    out = []
    for _ in range(parts - 1):
        hi = x.astype(BF16)
        out.append(hi)
        x = x - hi.astype(F32)
    out.append(x.astype(BF16))
    return out


def _mm_sel_lhs(sel, x, parts=3):
    cols = x.shape[1]
    res = jnp.dot(sel.astype(BF16), jnp.concatenate(_split_bf16(x, parts), axis=1), preferred_element_type=F32)
    out = res[:, :cols]
    for i in range(1, parts):
        out = out + res[:, i * cols:(i + 1) * cols]
    return out


def _mm_sel_rhs(x, sel, parts=2):
    rows = x.shape[0]
    res = jnp.dot(jnp.concatenate(_split_bf16(x, parts), axis=0), sel.astype(BF16), preferred_element_type=F32)
    out = res[:rows, :]
    for i in range(1, parts):
        out = out + res[i * rows:(i + 1) * rows, :]
    return out


def _unit_lower_inverse(x, eye, size, stack, out):
    rows = x.shape[0]
    y = x
    q = eye + x
    y = jnp.dot(x.astype(BF16), stack(x.astype(BF16)), preferred_element_type=F32)
    yield
    span = 4
    while span < size:
        res = jnp.dot(jnp.concatenate([y, q], axis=0).astype(BF16), stack(y.astype(BF16)),
                      preferred_element_type=F32)
        yield
        y, q = res[:rows, :], q + res[rows:, :]
        span *= 2
    q = q + jnp.dot(q.astype(BF16), stack(y.astype(BF16)), preferred_element_type=F32)
    yield
    out.append(q)


def _run_lockstep(chains):
    live = list(chains)
    while live:
        for g in list(live):
            try:
                next(g)
            except StopIteration:
                live.remove(g)


def _iota2(shape, dim):
    return lax.broadcasted_iota(jnp.int32, shape, dim)


def _chunk_and_shifts(ref, halo_ref, c, rows, shift):
    r0 = pl.multiple_of(c * rows, rows)
    h0 = pl.multiple_of(jnp.maximum(r0 - HALO, 0), HALO)
    halo = jnp.where(c == 0, halo_ref[...], ref[pl.ds(h0, HALO), :])
    cur = ref[pl.ds(r0, rows), :]
    xc = jnp.concatenate([halo, cur], axis=0)
    return cur, [pltpu.roll(xc, s, 0)[HALO:, :] for s in range(1, shift + 1)]


def _recurrence_grid_setup(state_ref, halo_refs):
    @pl.when(pl.program_id(2) == 0)
    def _():
        state_ref[...] = jnp.zeros_like(state_ref)
        for h in halo_refs:
            h[...] = jnp.zeros_like(h)


def _save_halos(pairs, rows):
    for src, dst in pairs:
        dst[...] = src[rows - HALO:rows, :]


def _rms(x, w_row):
    return x * lax.rsqrt(jnp.mean(x * x, axis=-1, keepdims=True) + NORM_EPS) * w_row


def _rmsnorm_kernel(x_ref, nw_ref, o_ref):
    o_ref[...] = _rms(x_ref[...], nw_ref[...]).astype(o_ref.dtype)


def _rmsnorm(x2, norm_w_row, tm=512):
    t = x2.shape[0]
    tm = min(tm, t)
    return pl.pallas_call(
        _rmsnorm_kernel,
        grid=(t // tm,),
        in_specs=[pl.BlockSpec((tm, D_MODEL), lambda i: (i, 0)),
                  pl.BlockSpec((1, D_MODEL), lambda i: (0, 0))],
        out_specs=pl.BlockSpec((tm, D_MODEL), lambda i: (i, 0)),
        out_shape=jax.ShapeDtypeStruct((t, D_MODEL), BF16),
        compiler_params=pltpu.CompilerParams(dimension_semantics=("arbitrary",), vmem_limit_bytes=VMEM_LIMIT),
        name="rmsnorm",
    )(x2, norm_w_row)


def _inproj_kernel(h_ref, w_ref, o_ref):
    o_ref[...] = jnp.dot(h_ref[...], w_ref[...], preferred_element_type=F32)


def _inproj(h, w_pad, tm=2048, tn=768):
    t = h.shape[0]
    tm = min(tm, t)
    return pl.pallas_call(
        _inproj_kernel,
        grid=(t // tm, NP // tn),
        in_specs=[pl.BlockSpec((tm, D_MODEL), lambda i, j: (i, 0)),
                  pl.BlockSpec((D_MODEL, tn), lambda i, j: (0, j))],
        out_specs=pl.BlockSpec((tm, tn), lambda i, j: (i, j)),
        out_shape=jax.ShapeDtypeStruct((t, NP), F32),
        compiler_params=pltpu.CompilerParams(
            dimension_semantics=("arbitrary", "arbitrary"), vmem_limit_bytes=VMEM_LIMIT),
        name="inproj",
    )(h, w_pad)


def _gdn_kernel(q_ref, k_ref, v_ref, z_ref, ba_ref, cq_ref, ck_ref, cv_ref,
                alog_ref, dtb_ref, nw_ref, o_ref, s_ref, hq_ref, hk_ref, hv_ref, *, blk_rows, n_tile):
    c_len = CHUNK
    rows = 2 * c_len
    width = MXU_TILE
    tile0 = pl.program_id(1) * n_tile

    lane_w = _iota2((1, width), 1)
    lane_b = _iota2((1, LANE), 1)
    t_w = _iota2((c_len, rows), 0)
    s_w = _iota2((c_len, rows), 1) % c_len
    strict_w = s_w < t_w
    incl_w = s_w <= t_w
    eye_w = jnp.where(s_w == t_w, 1.0, 0.0)
    triu_w = jnp.where(t_w <= s_w, 1.0, 0.0)
    ti = _iota2((c_len, c_len), 0)
    tj = _iota2((c_len, c_len), 1)
    tril_t = jnp.where(tj <= ti, 1.0, 0.0)
    ones_t = jnp.ones((c_len, c_len), F32)
    same_head = (_iota2((width, width), 0) // GDN_HEAD_DIM) == (_iota2((width, width), 1) // GDN_HEAD_DIM)
    head0_lane = lane_w < GDN_HEAD_DIM
    head0_wide = _iota2((1, rows), 1) < c_len
    zero_b = jnp.zeros((), BF16)

    _recurrence_grid_setup(s_ref, (hq_ref, hk_ref, hv_ref))
    alog_row = alog_ref[...]
    dtb_row = dtb_ref[...]
    nw_row = nw_ref[...]

    def stack(xb):
        return jnp.concatenate([jnp.where(head0_lane, xb, zero_b), jnp.where(head0_lane, zero_b, xb)], axis=0)

    def stack_w(xb):
        return jnp.concatenate([jnp.where(head0_wide, xb, zero_b), jnp.where(head0_wide, zero_b, xb)], axis=0)

    def per_head(x, fn):
        return jnp.concatenate([fn(x[:, :GDN_HEAD_DIM]), fn(x[:, GDN_HEAD_DIM:])], axis=1)

    def l2n(xh):
        return xh * lax.rsqrt(jnp.sum(xh * xh, axis=-1, keepdims=True) + 1e-6)

    def rms(oh):
        return oh * lax.rsqrt(jnp.mean(oh * oh, axis=-1, keepdims=True) + NORM_EPS) * nw_row

    def body(c, carry):
        r0 = pl.multiple_of(c * c_len, c_len)

        def conv(x_ref, h_ref, cw_ref):
            cur, sh = _chunk_and_shifts(x_ref, h_ref, c, c_len, 3)
            cw = cw_ref[...]
            acc = cur * cw[3:4, :]
            for s in range(1, 4):
                acc = acc + sh[s - 1] * cw[3 - s:4 - s, :]
            return _silu(acc)

        q_all = conv(q_ref, hq_ref, cq_ref)
        k_all = conv(k_ref, hk_ref, ck_ref)
        v_all = conv(v_ref, hv_ref, cv_ref)
        z_all = z_ref[pl.ds(r0, c_len), :]
        ba = ba_ref[pl.ds(r0, c_len), :]
        g_all = -jnp.exp(alog_row) * _softplus(ba + dtb_row)

        def chain(t):
            sl = slice(t * width, (t + 1) * width)
            q = per_head(q_all[:, sl], l2n) * (GDN_HEAD_DIM ** -0.5)
            k = per_head(k_all[:, sl], l2n)
            v = v_all[:, sl]
            betas, gs = [], []
            for h in range(2):
                hid = 2 * (tile0 + t) + h
                betas.append(_sigmoid(jnp.sum(jnp.where(lane_b == hid, ba, 0.0), axis=-1, keepdims=True)))
                gs.append(jnp.sum(jnp.where(lane_b == GDN_HEADS + hid, g_all, 0.0), axis=-1, keepdims=True))
            beta = jnp.where(head0_lane, betas[0], betas[1])
            g_n = jnp.where(head0_lane, gs[0], gs[1])
            g_w = jnp.where(head0_wide, gs[0], gs[1])

            gcol = _mm_sel_lhs(tril_t, g_n)
            grow = _mm_sel_lhs(ones_t, g_w * triu_w)
            yield
            gcol_w = jnp.where(head0_wide, gcol[:, :rows], gcol[:, GDN_HEAD_DIM:GDN_HEAD_DIM + rows])
            diff = gcol_w - grow
            d_strict = jnp.where(strict_w, jnp.exp(jnp.where(strict_w, diff, 0.0)), 0.0)
            d_incl = jnp.where(incl_w, jnp.exp(jnp.where(incl_w, diff, 0.0)), 0.0)
            egc = jnp.exp(gcol)
            gl_lane = gcol[c_len - 1:c_len, :]

            kb = k * beta
            prod = lax.dot_general(jnp.concatenate([kb, q], axis=0).astype(BF16), stack(k.astype(BF16)),
                                   (((1,), (1,)), ((), ())), preferred_element_type=F32)
            yield
            lower = prod[:c_len, :] * d_strict
            attn = prod[c_len:, :] * d_incl
            inv = []
            yield from _unit_lower_inverse(-lower, eye_w, c_len, stack_w, inv)
            rhs = jnp.concatenate([stack((v * beta).astype(BF16)), stack((kb * egc).astype(BF16))], axis=1)
            uw = jnp.dot(inv[0].astype(BF16), rhs, preferred_element_type=F32)
            yield
            u, w = uw[:, :width], uw[:, width:]

            state = s_ref[t]
            x0 = _mm(jnp.concatenate([w, q * egc], axis=0), state)
            yield
            v_new = u - x0[:c_len, :]
            o = x0[c_len:, :] + jnp.dot(attn.astype(BF16), stack(v_new.astype(BF16)), preferred_element_type=F32)
            ds = _mm_tn(k * jnp.exp(gl_lane - gcol), v_new)
            s_ref[t] = state * jnp.exp(gl_lane) + jnp.where(same_head, ds, 0.0)
            yield

            o_ref[pl.ds(r0, c_len), sl] = (per_head(o, rms) * _silu(z_all[:, sl])).astype(o_ref.dtype)

        _run_lockstep([chain(t) for t in range(n_tile)])
        return carry

    lax.fori_loop(0, blk_rows // c_len, body, 0)
    _save_halos(((q_ref, hq_ref), (k_ref, hk_ref), (v_ref, hv_ref)), blk_rows)


def _gdn(proj, conv_w, alog_row, dtb_row, nw_row, batch, seq, n_tile=4):
    w = n_tile * MXU_TILE
    n_grp = GDN_WIDTH // w
    blk = min(SEQ_BLOCK, seq)
    nsb = seq // blk
    colspec = lambda base: pl.BlockSpec((blk, w), lambda b, g, s, base=base: (b * nsb + s, base * n_grp + g))
    cwspec = lambda base: pl.BlockSpec((4, w), lambda b, g, s, base=base: (0, base * n_grp + g))
    rowspec = pl.BlockSpec((1, LANE), lambda b, g, s: (0, 0))
    return pl.pallas_call(
        functools.partial(_gdn_kernel, blk_rows=blk, n_tile=n_tile),
        grid=(batch, n_grp, nsb),
        in_specs=[colspec(0), colspec(1), colspec(2), colspec(COL_GZ * LANE // GDN_WIDTH),
                  pl.BlockSpec((blk, LANE), lambda b, g, s: (b * nsb + s, COL_GBA)),
                  cwspec(0), cwspec(1), cwspec(2), rowspec, rowspec, rowspec],
        out_specs=pl.BlockSpec((blk, w), lambda b, g, s: (b * nsb + s, g)),
        out_shape=jax.ShapeDtypeStruct((batch * seq, GDN_WIDTH), BF16),
        scratch_shapes=[pltpu.VMEM((n_tile, MXU_TILE, MXU_TILE), F32)] + [pltpu.VMEM((HALO, w), F32)] * 3,
        compiler_params=pltpu.CompilerParams(
            dimension_semantics=("arbitrary", "arbitrary", "arbitrary"), vmem_limit_bytes=VMEM_LIMIT),
        name="gdn",
    )(proj, proj, proj, proj, proj, conv_w, conv_w, conv_w, alog_row, dtb_row, nw_row)


def _rwkv_kernel(r_ref, k_ref, v_ref, wl_ref, al_ref, z_ref,
                 mur_ref, muk_ref, muv_ref, muwl_ref, mual_ref,
                 w0_ref, wup_ref, a0_ref, aup_ref, kk_ref, ka_ref, rk_ref, lnw_ref, lnb_ref,
                 o_ref, n_ref, hr_ref, hk_ref, hv_ref, hwl_ref, hal_ref, *, blk_rows, n_tile):
    c_len = CHUNK
    nh = MXU_TILE // RWKV_HEAD_DIM
    width = MXU_TILE
    assert c_len == RWKV_HEAD_DIM

    same_head = (_iota2((width, width), 0) // RWKV_HEAD_DIM) == (_iota2((width, width), 1) // RWKV_HEAD_DIM)
    head_sum = jnp.where(same_head, 1.0, 0.0)
    t_w = _iota2((c_len, width), 0)
    s_w = _iota2((c_len, width), 1) % c_len
    strict_w = s_w < t_w
    incl_w = s_w <= t_w
    eye_w = jnp.where(s_w == t_w, 1.0, 0.0)
    ti = _iota2((c_len, c_len), 0)
    tj = _iota2((c_len, c_len), 1)
    tril_t = jnp.where(tj <= ti, 1.0, 0.0)
    lane_head = _iota2((1, width), 1) // RWKV_HEAD_DIM
    zero_b = jnp.zeros((), BF16)

    _recurrence_grid_setup(n_ref, (hr_ref, hk_ref, hv_ref, hwl_ref, hal_ref))

    def stack(xb):
        return jnp.concatenate([jnp.where(lane_head == h, xb, zero_b) for h in range(nh)], axis=0)

    def body(c, carry):
        r0 = pl.multiple_of(c * c_len, c_len)

        def shifted(x_ref, h_ref, mu_ref):
            cur, sh = _chunk_and_shifts(x_ref, h_ref, c, c_len, 1)
            return cur + (sh[0] - cur) * mu_ref[...]

        r_all = shifted(r_ref, hr_ref, mur_ref)
        k_all = shifted(k_ref, hk_ref, muk_ref)
        v_all = shifted(v_ref, hv_ref, muv_ref)
        wl = shifted(wl_ref, hwl_ref, muwl_ref)
        al = shifted(al_ref, hal_ref, mual_ref)
        z_all = z_ref[pl.ds(r0, c_len), :]

        w_pre = w0_ref[...] + _mm(jnp.tanh(wl), wup_ref[...])
        logw_all = -jnp.exp(-_softplus(-w_pre) - 0.5)
        a_all = _sigmoid(a0_ref[...] + _mm(al, aup_ref[...]))
        kkx_all = k_all * kk_ref[...]
        k2_all = k_all * (1.0 + (a_all - 1.0) * ka_ref[...])
        lc_all = _mm_sel_lhs(tril_t, logw_all)
        bonus_all = r_all * k2_all * rk_ref[...]

        def chain(t):
            sl = slice(t * width, (t + 1) * width)
            r, v, a, k2, logw, lc = r_all[:, sl], v_all[:, sl], a_all[:, sl], k2_all[:, sl], logw_all[:, sl], lc_all[:, sl]
            kkx = kkx_all[:, sl]
            sums = _mm_sel_rhs(jnp.concatenate([kkx * kkx, bonus_all[:, sl]], axis=0), head_sum)
            yield
            kk = kkx * lax.rsqrt(sums[:c_len, :] + 1e-6)
            bonus = sums[c_len:, :]

            lc_last = lc[c_len - 1:c_len, :]
            e_neg = jnp.exp(-lc)
            e_rem = jnp.exp(lc_last - lc)
            kka = kk * a
            a_n = -kk * jnp.exp(lc - logw)
            r_n = r * jnp.exp(lc)
            sv = stack(v.astype(BF16))

            prod = lax.dot_general(
                jnp.concatenate([a_n, r_n], axis=0).astype(BF16),
                jnp.concatenate([stack((kka * e_neg).astype(BF16)), stack((k2 * e_neg).astype(BF16))], axis=0),
                (((1,), (1,)), ((), ())), preferred_element_type=F32)
            yield
            a_ab = jnp.where(strict_w, prod[:c_len, :width], 0.0)
            a_ak = jnp.where(strict_w, prod[:c_len, width:], 0.0)
            a_rb = jnp.where(incl_w, prod[c_len:, :width], 0.0)
            a_rk = jnp.where(incl_w, prod[c_len:, width:], 0.0)

            inv = []
            yield from _unit_lower_inverse(a_ab, eye_w, c_len, stack, inv)
            t_mat = inv[0].astype(BF16)
            w_p = jnp.dot(t_mat, stack(a_n.astype(BF16)), preferred_element_type=F32)
            aks = jnp.dot(a_ak.astype(BF16), sv, preferred_element_type=F32)
            yield
            v_p = jnp.dot(t_mat, stack(aks.astype(BF16)), preferred_element_type=F32)
            yield

            state = n_ref[t]
            x0 = _mm(jnp.concatenate([w_p, r_n], axis=0), state)
            yield
            u = x0[:c_len, :] + v_p
            y = x0[c_len:, :] + jnp.dot(jnp.concatenate([a_rb, a_rk], axis=1).astype(BF16),
                                        jnp.concatenate([stack(u.astype(BF16)), sv], axis=0),
                                        preferred_element_type=F32)
            dn = _mm_tn(jnp.concatenate([kka * e_rem, k2 * e_rem], axis=0), jnp.concatenate([u, v], axis=0))
            gam_col = jnp.transpose(jnp.broadcast_to(jnp.exp(lc_last), (width, width)))
            n_ref[t] = gam_col * state + jnp.where(same_head, dn, 0.0)
            yield

            inv_n = 1.0 / RWKV_HEAD_DIM
            mean = _mm_sel_rhs(y, head_sum) * inv_n
            yield
            dlt = y - mean
            var = _mm_sel_rhs(dlt * dlt, head_sum) * inv_n
            yield
            y = dlt * lax.rsqrt(var + RWKV_LN_EPS) * lnw_ref[:, sl] + lnb_ref[:, sl]
            y = y + bonus * v
            o_ref[pl.ds(r0, c_len), sl] = (y * _silu(z_all[:, sl])).astype(o_ref.dtype)

        _run_lockstep([chain(t) for t in range(n_tile)])
        return carry

    lax.fori_loop(0, blk_rows // c_len, body, 0)
    _save_halos(((r_ref, hr_ref), (k_ref, hk_ref), (v_ref, hv_ref), (wl_ref, hwl_ref), (al_ref, hal_ref)),
                blk_rows)


def _rwkv(proj, p, batch, seq, n_tile=4):
    w = n_tile * MXU_TILE
    n_grp = RWKV_WIDTH // w
    blk = min(SEQ_BLOCK, seq)
    nsb = seq // blk
    colspec = lambda base: pl.BlockSpec((blk, w), lambda b, g, s, base=base: (b * nsb + s, base * n_grp + g))
    lspec = lambda col: pl.BlockSpec((blk, LANE), lambda b, g, s, col=col: (b * nsb + s, col))
    grow = pl.BlockSpec((1, w), lambda b, g, s: (0, g))
    lrow = pl.BlockSpec((1, LANE), lambda b, g, s: (0, 0))
    upspec = pl.BlockSpec((LANE, w), lambda b, g, s: (0, g))
    base = COL_RKV * LANE // RWKV_WIDTH
    return pl.pallas_call(
        functools.partial(_rwkv_kernel, blk_rows=blk, n_tile=n_tile),
        grid=(batch, n_grp, nsb),
        in_specs=[colspec(base), colspec(base + 1), colspec(base + 2), lspec(COL_WL), lspec(COL_AL),
                  colspec(COL_RZ * LANE // RWKV_WIDTH),
                  grow, grow, grow, lrow, lrow,
                  grow, upspec, grow, upspec, grow, grow, grow, grow, grow],
        out_specs=pl.BlockSpec((blk, w), lambda b, g, s: (b * nsb + s, g)),
        out_shape=jax.ShapeDtypeStruct((batch * seq, RWKV_WIDTH), BF16),
        scratch_shapes=([pltpu.VMEM((n_tile, MXU_TILE, MXU_TILE), F32)] + [pltpu.VMEM((HALO, w), F32)] * 3
                        + [pltpu.VMEM((HALO, LANE), F32)] * 2),
        compiler_params=pltpu.CompilerParams(
            dimension_semantics=("arbitrary", "arbitrary", "arbitrary"), vmem_limit_bytes=VMEM_LIMIT),
        name="rwkv",
    )(proj, proj, proj, proj, proj, proj,
      p["mu_r"], p["mu_k"], p["mu_v"], p["mu_wl"], p["mu_al"],
      p["w0"], p["w_up"], p["a0"], p["a_up"], p["k_k"], p["k_a"], p["r_k"], p["lnx_w"], p["lnx_b"])


def _s5_kernel(u_ref, lre_ref, lim_ref, ldt_ref, bre_ref, bim_ref, cre_ref, cim_ref, d_ref,
               o_ref, pre_ref, pim_ref, w_ref, *, seq):
    tile = S5_TILE
    n_pair = LANE // (2 * S5_GROUP)
    sub = HALO
    n_sub = tile // sub

    def cmul(ar, ai, br, bi):
        return ar * br - ai * bi, ar * bi + ai * br

    def block_scan(sr, si, ar, ai, out):
        in_blk = _iota2(sr.shape, 0) % sub
        d = 1
        while d < sub:
            keep = in_blk >= d
            tr, ti = cmul(ar, ai, jnp.where(keep, pltpu.roll(sr, d, 0), 0.0),
                          jnp.where(keep, pltpu.roll(si, d, 0), 0.0))
            sr, si = sr + tr, si + ti
            ar, ai = cmul(ar, ai, ar, ai)
            d *= 2
            yield
        out.extend((sr, si))

    ab = []
    for p in range(n_pair):
        lre = lre_ref[0, p:p + 1, :]
        lim = lim_ref[0, p:p + 1, :]
        dt = jnp.exp(ldt_ref[0, p:p + 1, :])
        mag = jnp.exp(lre * dt)
        ab_re = mag * jnp.cos(lim * dt)
        ab_im = mag * jnp.sin(lim * dt)
        den = lre * lre + lim * lim
        coef_re = ((ab_re - 1.0) * lre + ab_im * lim) / den
        coef_im = (ab_im * lre - (ab_re - 1.0) * lim) / den
        b_re = bre_ref[0, p]
        b_im = bim_ref[0, p]
        w_ref[p, 0] = (coef_re * b_re - coef_im * b_im).astype(BF16)
        w_ref[p, 1] = (coef_re * b_im + coef_im * b_re).astype(BF16)
        w_ref[p, 2] = cre_ref[0, p].astype(BF16)
        w_ref[p, 3] = cim_ref[0, p].astype(BF16)
        first = _iota2((sub, LANE), 0) == 0
        imp_re = jnp.where(first, jnp.broadcast_to(ab_re, (sub, LANE)), 0.0)
        imp_im = jnp.where(first, jnp.broadcast_to(ab_im, (sub, LANE)), 0.0)
        pw = []
        for _ in block_scan(imp_re, imp_im, ab_re, ab_im, pw):
            pass
        pre_ref[p] = pw[0]
        pim_ref[p] = pw[1]
        ab.append((ab_re, ab_im))

    def tile_body(i, st):
        t0 = pl.multiple_of(i * tile, tile)
        u = u_ref[pl.ds(t0, tile), :]
        ub = u.astype(BF16)
        ys, new_st = [None] * n_pair, [None] * n_pair

        def chain(p):
            cr, ci = st[p]
            sr = jnp.dot(ub, w_ref[p, 0], preferred_element_type=F32)
            si = jnp.dot(ub, w_ref[p, 1], preferred_element_type=F32)
            yield
            res = []
            yield from block_scan(sr, si, ab[p][0], ab[p][1], res)
            pr, pi = pre_ref[p], pim_ref[p]
            out_r, out_i = [], []
            for j in range(n_sub):
                tr, ti = cmul(pr, pi, cr, ci)
                out_r.append(res[0][j * sub:(j + 1) * sub, :] + tr)
                out_i.append(res[1][j * sub:(j + 1) * sub, :] + ti)
                cr, ci = out_r[-1][sub - 1:sub, :], out_i[-1][sub - 1:sub, :]
            sr, si = jnp.concatenate(out_r, axis=0), jnp.concatenate(out_i, axis=0)
            new_st[p] = (cr, ci)
            ys[p] = jnp.dot(sr.astype(BF16), w_ref[p, 2], preferred_element_type=F32) - jnp.dot(
                si.astype(BF16), w_ref[p, 3], preferred_element_type=F32)

        _run_lockstep([chain(p) for p in range(n_pair)])
        y = d_ref[0] * u
        for p in range(n_pair):
            y = y + ys[p]
        inner = math.sqrt(2.0 / math.pi) * (y + 0.044715 * (y * y * y))
        o_ref[pl.ds(t0, tile), :] = 0.5 * y * (1.0 + jnp.tanh(inner))
        return tuple(new_st)

    zero = jnp.zeros((1, LANE), F32)
    lax.fori_loop(0, seq // tile, tile_body, tuple((zero, zero) for _ in range(n_pair)))


def _s5(proj, p, batch, seq):
    n_blk = S5_WIDTH // LANE
    n_pair = LANE // (2 * S5_GROUP)
    rowspec = pl.BlockSpec((1, n_pair, LANE), lambda b, j: (j, 0, 0))
    matspec = pl.BlockSpec((1, n_pair, LANE, LANE), lambda b, j: (j, 0, 0, 0))
    return pl.pallas_call(
        functools.partial(_s5_kernel, seq=seq),
        grid=(batch, n_blk),
        in_specs=[pl.BlockSpec((seq, LANE), lambda b, j: (b, COL_SU + j)),
                  rowspec, rowspec, rowspec, matspec, matspec, matspec, matspec,
                  pl.BlockSpec((1, 1, LANE), lambda b, j: (j, 0, 0))],
        out_specs=pl.BlockSpec((seq, LANE), lambda b, j: (b, j)),
        out_shape=jax.ShapeDtypeStruct((batch * seq, S5_WIDTH), F32),
        scratch_shapes=[pltpu.VMEM((n_pair, HALO, LANE), F32), pltpu.VMEM((n_pair, HALO, LANE), F32),
                        pltpu.VMEM((n_pair, 4, LANE, LANE), BF16)],
        compiler_params=pltpu.CompilerParams(
            dimension_semantics=("arbitrary", "arbitrary"), vmem_limit_bytes=VMEM_LIMIT),
        name="s5_scan",
    )(proj, p["lam_re"], p["lam_im"], p["log_dt"], p["b_re"], p["b_im"], p["c_re"], p["c_im"], p["d"])


def _s5_glu_kernel(y_ref, w_ref, b_ref, z_ref, o_ref):
    y = y_ref[...]
    gate = _sigmoid(_mm(y, w_ref[...]) + b_ref[...])
    o_ref[...] = (y * gate * _silu(z_ref[...])).astype(o_ref.dtype)


def _s5_glu(yc, proj, glu_w, glu_b, tm=512):
    t = yc.shape[0]
    tm = min(tm, t)
    return pl.pallas_call(
        _s5_glu_kernel,
        grid=(t // tm,),
        in_specs=[pl.BlockSpec((tm, S5_WIDTH), lambda i: (i, 0)),
                  pl.BlockSpec((S5_WIDTH, S5_WIDTH), lambda i: (0, 0)),
                  pl.BlockSpec((1, S5_WIDTH), lambda i: (0, 0)),
                  pl.BlockSpec((tm, S5_WIDTH), lambda i: (i, COL_SZ * LANE // S5_WIDTH))],
        out_specs=pl.BlockSpec((tm, S5_WIDTH), lambda i: (i, 0)),
        out_shape=jax.ShapeDtypeStruct((t, S5_WIDTH), BF16),
        compiler_params=pltpu.CompilerParams(
            dimension_semantics=("arbitrary",), vmem_limit_bytes=VMEM_LIMIT),
        name="s5_glu",
    )(yc, glu_w, glu_b, proj)


def _merge_kernel(oa_ref, ob_ref, oc_ref, wb_ref, ga_ref, gb_ref, gc_ref, gbias_ref, o_ref):
    acc = None
    for i, (o_r, g_r) in enumerate(((oa_ref, ga_ref), (ob_ref, gb_ref), (oc_ref, gc_ref))):
        proj = jnp.dot(o_r[...], wb_ref[i], preferred_element_type=F32)
        term = _sigmoid(g_r[...] + gbias_ref[i]) * proj
        acc = term if acc is None else acc + term
    o_ref[...] = acc.astype(o_ref.dtype)


def _merge(oa, ob, oc, proj, w_branch, gate_b, tm=512, tn=512):
    t = oa.shape[0]
    tm = min(tm, t)
    gbase = COL_GATE * LANE // tn
    per = D_MODEL // tn
    ospec = pl.BlockSpec((tm, 1024), lambda i, j: (i, 0))
    gspec = lambda br: pl.BlockSpec((tm, tn), lambda i, j, br=br: (i, gbase + br * per + j))
    return pl.pallas_call(
        _merge_kernel,
        grid=(t // tm, per),
        in_specs=[ospec, ospec, ospec,
                  pl.BlockSpec((N_BRANCH, 1024, tn), lambda i, j: (0, 0, j)),
                  gspec(0), gspec(1), gspec(2),
                  pl.BlockSpec((N_BRANCH, 1, tn), lambda i, j: (0, 0, j))],
        out_specs=pl.BlockSpec((tm, tn), lambda i, j: (i, j)),
        out_shape=jax.ShapeDtypeStruct((t, D_MODEL), BF16),
        compiler_params=pltpu.CompilerParams(
            dimension_semantics=("arbitrary", "arbitrary"), vmem_limit_bytes=VMEM_LIMIT),
        name="merge",
    )(oa, ob, oc, w_branch, proj, proj, proj, gate_b)


def _outproj_kernel(m_ref, w_ref, x_ref, nw_ref, *o_refs):
    x = x_ref[...] + jnp.dot(m_ref[...], w_ref[...], preferred_element_type=F32)
    normed = _rms(x, nw_ref[...])
    if len(o_refs) == 2:
        o_refs[0][...] = x
    o_refs[-1][...] = normed.astype(o_refs[-1].dtype)


def _outproj(merged, w_out, x2, next_norm_row, last, tm=512):
    t = x2.shape[0]
    tm = min(tm, t)
    row_blk = pl.BlockSpec((tm, D_MODEL), lambda i: (i, 0))
    if last:
        out_specs, out_shape = row_blk, jax.ShapeDtypeStruct((t, D_MODEL), F32)
    else:
        out_specs = (row_blk, row_blk)
        out_shape = (jax.ShapeDtypeStruct((t, D_MODEL), F32), jax.ShapeDtypeStruct((t, D_MODEL), BF16))
    return pl.pallas_call(
        _outproj_kernel,
        grid=(t // tm,),
        in_specs=[row_blk,
                  pl.BlockSpec((D_MODEL, D_MODEL), lambda i: (0, 0)),
                  row_blk,
                  pl.BlockSpec((1, D_MODEL), lambda i: (0, 0))],
        out_specs=out_specs,
        out_shape=out_shape,
        compiler_params=pltpu.CompilerParams(
            dimension_semantics=("arbitrary",), vmem_limit_bytes=VMEM_LIMIT),
        name="outproj",
    )(merged, w_out, x2, next_norm_row)


def _pad_cols(a, width):
    return jnp.pad(a, [(0, 0)] * (a.ndim - 1) + [(0, width - a.shape[-1])])


def _pad_proj_cols(w, dtype=None):
    w = w if dtype is None else w.astype(dtype)
    o_gba, o_rf = 4096, 4112
    o_wl, o_al, o_rz = o_rf + 3072, o_rf + 3072 + RWKV_LORA, o_rf + 3072 + 2 * RWKV_LORA
    pieces = [w[:, :o_gba],
              w[:, o_rf:o_wl],
              w[:, o_rz:],
              _pad_cols(w[:, o_gba:o_rf], LANE),
              _pad_cols(w[:, o_wl:o_al], LANE),
              _pad_cols(w[:, o_al:o_rz], LANE),
              jnp.zeros((w.shape[0], LANE), w.dtype)]
    return jnp.concatenate(pieces, axis=1)


def _place_s5(a, rows_are_channels):
    n_blk, n_pair = S5_WIDTH // LANE, LANE // (2 * S5_GROUP)
    grp_in_blk = LANE // S5_GROUP
    cs = a if not rows_are_channels else jnp.swapaxes(a, 1, 2)
    cs = cs.reshape(n_blk, n_pair, 2, S5_STATE, S5_GROUP)
    slot = jnp.arange(grp_in_blk)[None, :, None] == (2 * jnp.arange(n_pair)[:, None, None]
                                                      + jnp.arange(2)[None, None, :])
    placed = jnp.where(slot[None, :, :, None, :, None],
                       jnp.transpose(cs, (0, 1, 4, 2, 3))[:, :, None, :, :, :], 0.0)
    placed = placed.reshape(n_blk, n_pair, LANE, LANE)
    return placed if not rows_are_channels else jnp.swapaxes(placed, 2, 3)


def _s5_rows(a):
    return a.reshape(S5_WIDTH // LANE, LANE // (2 * S5_GROUP), 2 * S5_STATE)


def _gdn_head_row(a):
    return _pad_cols(jnp.pad(a, (GDN_HEADS, 0)).reshape(1, 2 * GDN_HEADS), LANE)


def _rwkv_params(mu, w0, w_up, a0, a_up, k_k, k_a, r_k, lnx_w, lnx_b):
    row = lambda a: a.reshape(1, -1)
    lora = lambda a: jnp.pad(a, ((0, LANE - RWKV_LORA), (0, 0))).astype(BF16)
    w3 = 3 * RWKV_WIDTH
    return dict(mu_r=row(mu[:RWKV_WIDTH]), mu_k=row(mu[RWKV_WIDTH:2 * RWKV_WIDTH]), mu_v=row(mu[2 * RWKV_WIDTH:w3]),
                mu_wl=_pad_cols(row(mu[w3:w3 + RWKV_LORA]), LANE), mu_al=_pad_cols(row(mu[w3 + RWKV_LORA:]), LANE),
                w0=row(w0), w_up=lora(w_up), a0=row(a0), a_up=lora(a_up), k_k=row(k_k), k_a=row(k_a),
                r_k=row(r_k), lnx_w=row(lnx_w), lnx_b=row(lnx_b))


def _s5_params(a_re, a_im, log_dt, b_re, b_im, c_re, c_im, d):
    return dict(lam_re=_s5_rows(a_re), lam_im=_s5_rows(a_im),
                log_dt=_s5_rows(jnp.broadcast_to(log_dt[:, None], (S5_WIDTH // S5_GROUP, S5_STATE))),
                b_re=_place_s5(b_re, False), b_im=_place_s5(b_im, False),
                c_re=_place_s5(c_re, True), c_im=_place_s5(c_im, True),
                d=d.reshape(S5_WIDTH // LANE, 1, LANE))


def kernel(x, norm_w, w_in, gdn_conv_w, gdn_a_log, gdn_dt_bias, gdn_norm_w, rwkv_mu, rwkv_w0, rwkv_w_up,
           rwkv_a0, rwkv_a_up, rwkv_k_k, rwkv_k_a, rwkv_r_k, rwkv_lnx_w, rwkv_lnx_b, s5_a_re, s5_a_im,
           s5_log_dt, s5_b_re, s5_b_im, s5_c_re, s5_c_im, s5_d, s5_glu_w, s5_glu_b, gate_b, w_branch,
           w_out, final_norm_w):
    batch, seq, _ = x.shape
    depth = w_in.shape[0]
    x2 = x.reshape(batch * seq, D_MODEL)
    h = _rmsnorm(x2, norm_w[0].reshape(1, D_MODEL))

    for i in range(depth):
        proj = _inproj(h, _pad_proj_cols(w_in[i], BF16))

        o_a = _gdn(proj, gdn_conv_w[i], _gdn_head_row(gdn_a_log[i]), _gdn_head_row(gdn_dt_bias[i]),
                   gdn_norm_w[i].reshape(1, GDN_HEAD_DIM), batch, seq)

        rp = _rwkv_params(rwkv_mu[i], rwkv_w0[i], rwkv_w_up[i], rwkv_a0[i], rwkv_a_up[i], rwkv_k_k[i],
                          rwkv_k_a[i], rwkv_r_k[i], rwkv_lnx_w[i], rwkv_lnx_b[i])
        o_b = _rwkv(proj, rp, batch, seq)

        sp = _s5_params(s5_a_re[i], s5_a_im[i], s5_log_dt[i], s5_b_re[i], s5_b_im[i], s5_c_re[i], s5_c_im[i],
                        s5_d[i])
        y_c = _s5(proj, sp, batch, seq)
        o_c = _s5_glu(y_c, proj, s5_glu_w[i].astype(BF16), s5_glu_b[i].reshape(1, S5_WIDTH))

        merged = _merge(o_a, o_b, o_c, proj, w_branch[i].astype(BF16),
                        gate_b[i].reshape(N_BRANCH, 1, D_MODEL))
        if i == depth - 1:
            return _outproj(merged, w_out[i].astype(BF16), x2, final_norm_w.reshape(1, D_MODEL), True
                            ).reshape(batch, seq, D_MODEL)
        x2, h = _outproj(merged, w_out[i].astype(BF16), x2, norm_w[i + 1].reshape(1, D_MODEL), False)
```

```python
import functools
import math

import jax
import jax.numpy as jnp
from jax import lax
from jax.experimental import pallas as pl
from jax.experimental.pallas import tpu as pltpu

F32 = jnp.float32
BF16 = jnp.bfloat16

D_MODEL = 2048
GDN_HEADS = 8
GDN_HEAD_DIM = 128
GDN_WIDTH = 1024
RWKV_HEAD_DIM = 64
RWKV_WIDTH = 1024
RWKV_LORA = 96
RWKV_LN_EPS = 64e-5
S5_GROUP = 16
S5_STATE = 64
S5_WIDTH = 1024
N_BRANCH = 3
NORM_EPS = 1e-6

LANE = 128
MXU_TILE = 256
HALO = 8
CHUNK = 64
SEQ_BLOCK = 512
S5_TILE = 128
VMEM_LIMIT = 52 * 1024 * 1024

COL_QKV = 0
COL_GZ = 24
COL_RKV = 32
COL_RZ = 56
COL_SU = 64
COL_SZ = 72
COL_GATE = 80
COL_GBA = 128
COL_WL = 129
COL_AL = 130
N_COLBLK = 132
NP = N_COLBLK * LANE


def _sigmoid(x):
    return 1.0 / (1.0 + jnp.exp(-x))


def _silu(x):
    return x * _sigmoid(x)


def _softplus(x):
    return jnp.maximum(x, 0.0) + jnp.log1p(jnp.exp(-jnp.abs(x)))


def _mm(a, b):
    return jnp.dot(a.astype(BF16), b.astype(BF16), preferred_element_type=F32)


def _mm_nt(a, b):
    return lax.dot_general(a.astype(BF16), b.astype(BF16), (((1,), (1,)), ((), ())),
                           preferred_element_type=F32)


def _mm_tn(a, b):
    return lax.dot_general(a.astype(BF16), b.astype(BF16), (((0,), (0,)), ((), ())),
                           preferred_element_type=F32)


def _split_bf16(x, parts):
    out = []
    for _ in range(parts - 1):
        hi = x.astype(BF16)
        out.append(hi)
        x = x - hi.astype(F32)
    out.append(x.astype(BF16))
    return out


def _mm_sel_lhs(sel, x, parts=3):
    cols = x.shape[1]
    res = jnp.dot(sel.astype(BF16), jnp.concatenate(_split_bf16(x, parts), axis=1), preferred_element_type=F32)
    out = res[:, :cols]
    for i in range(1, parts):
        out = out + res[:, i * cols:(i + 1) * cols]
    return out


def _mm_sel_rhs(x, sel, parts=2):
    rows = x.shape[0]
    res = jnp.dot(jnp.concatenate(_split_bf16(x, parts), axis=0), sel.astype(BF16), preferred_element_type=F32)
    out = res[:rows, :]
    for i in range(1, parts):
        out = out + res[i * rows:(i + 1) * rows, :]
    return out


def _unit_lower_inverse(x, eye, size, stack, out):
    rows = x.shape[0]
    y = x
    q = eye + x
    y = jnp.dot(x.astype(BF16), stack(x.astype(BF16)), preferred_element_type=F32)
    yield
    span = 4
    while span < size:
        res = jnp.dot(jnp.concatenate([y, q], axis=0).astype(BF16), stack(y.astype(BF16)),
                      preferred_element_type=F32)
        yield
        y, q = res[:rows, :], q + res[rows:, :]
        span *= 2
    q = q + jnp.dot(q.astype(BF16), stack(y.astype(BF16)), preferred_element_type=F32)
    yield
    out.append(q)


def _run_lockstep(chains):
    live = list(chains)
    while live:
        for g in list(live):
            try:
                next(g)
            except StopIteration:
                live.remove(g)


def _iota2(shape, dim):
    return lax.broadcasted_iota(jnp.int32, shape, dim)


def _chunk_and_shifts(ref, halo_ref, c, rows, shift):
    r0 = pl.multiple_of(c * rows, rows)
    h0 = pl.multiple_of(jnp.maximum(r0 - HALO, 0), HALO)
    halo = jnp.where(c == 0, halo_ref[...], ref[pl.ds(h0, HALO), :])
    cur = ref[pl.ds(r0, rows), :]
    xc = jnp.concatenate([halo, cur], axis=0)
    return cur, [pltpu.roll(xc, s, 0)[HALO:, :] for s in range(1, shift + 1)]


def _recurrence_grid_setup(state_ref, halo_refs):
    @pl.when(pl.program_id(2) == 0)
    def _():
        state_ref[...] = jnp.zeros_like(state_ref)
        for h in halo_refs:
            h[...] = jnp.zeros_like(h)


def _save_halos(pairs, rows):
    for src, dst in pairs:
        dst[...] = src[rows - HALO:rows, :]


def _rms(x, w_row):
    return x * lax.rsqrt(jnp.mean(x * x, axis=-1, keepdims=True) + NORM_EPS) * w_row


def _rmsnorm_kernel(x_ref, nw_ref, o_ref):
    o_ref[...] = _rms(x_ref[...], nw_ref[...]).astype(o_ref.dtype)


def _rmsnorm(x2, norm_w_row, tm=512):
    t = x2.shape[0]
    tm = min(tm, t)
    return pl.pallas_call(
        _rmsnorm_kernel,
        grid=(t // tm,),
        in_specs=[pl.BlockSpec((tm, D_MODEL), lambda i: (i, 0)),
                  pl.BlockSpec((1, D_MODEL), lambda i: (0, 0))],
        out_specs=pl.BlockSpec((tm, D_MODEL), lambda i: (i, 0)),
        out_shape=jax.ShapeDtypeStruct((t, D_MODEL), BF16),
        compiler_params=pltpu.CompilerParams(dimension_semantics=("arbitrary",), vmem_limit_bytes=VMEM_LIMIT),
        name="rmsnorm",
    )(x2, norm_w_row)


def _inproj_kernel(h_ref, w_ref, o_ref):
    o_ref[...] = jnp.dot(h_ref[...], w_ref[...], preferred_element_type=F32)


def _inproj(h, w_pad, tm=2048, tn=768):
    t = h.shape[0]
    tm = min(tm, t)
    return pl.pallas_call(
        _inproj_kernel,
        grid=(t // tm, NP // tn),
        in_specs=[pl.BlockSpec((tm, D_MODEL), lambda i, j: (i, 0)),
                  pl.BlockSpec((D_MODEL, tn), lambda i, j: (0, j))],
        out_specs=pl.BlockSpec((tm, tn), lambda i, j: (i, j)),
        out_shape=jax.ShapeDtypeStruct((t, NP), F32),
        compiler_params=pltpu.CompilerParams(
            dimension_semantics=("arbitrary", "arbitrary"), vmem_limit_bytes=VMEM_LIMIT),
        name="inproj",
    )(h, w_pad)


def _gdn_kernel(q_ref, k_ref, v_ref, z_ref, ba_ref, cq_ref, ck_ref, cv_ref,
                alog_ref, dtb_ref, nw_ref, o_ref, s_ref, hq_ref, hk_ref, hv_ref, *, blk_rows, n_tile):
    c_len = CHUNK
    rows = 2 * c_len
    width = MXU_TILE
    tile0 = pl.program_id(1) * n_tile

    lane_w = _iota2((1, width), 1)
    lane_b = _iota2((1, LANE), 1)
    t_w = _iota2((c_len, rows), 0)
    s_w = _iota2((c_len, rows), 1) % c_len
    strict_w = s_w < t_w
    incl_w = s_w <= t_w
    eye_w = jnp.where(s_w == t_w, 1.0, 0.0)
    triu_w = jnp.where(t_w <= s_w, 1.0, 0.0)
    ti = _iota2((c_len, c_len), 0)
    tj = _iota2((c_len, c_len), 1)
    tril_t = jnp.where(tj <= ti, 1.0, 0.0)
    ones_t = jnp.ones((c_len, c_len), F32)
    same_head = (_iota2((width, width), 0) // GDN_HEAD_DIM) == (_iota2((width, width), 1) // GDN_HEAD_DIM)
    head0_lane = lane_w < GDN_HEAD_DIM
    head0_wide = _iota2((1, rows), 1) < c_len
    zero_b = jnp.zeros((), BF16)

    _recurrence_grid_setup(s_ref, (hq_ref, hk_ref, hv_ref))
    alog_row = alog_ref[...]
    dtb_row = dtb_ref[...]
    nw_row = nw_ref[...]

    def stack(xb):
        return jnp.concatenate([jnp.where(head0_lane, xb, zero_b), jnp.where(head0_lane, zero_b, xb)], axis=0)

    def stack_w(xb):
        return jnp.concatenate([jnp.where(head0_wide, xb, zero_b), jnp.where(head0_wide, zero_b, xb)], axis=0)

    def per_head(x, fn):
        return jnp.concatenate([fn(x[:, :GDN_HEAD_DIM]), fn(x[:, GDN_HEAD_DIM:])], axis=1)

    def l2n(xh):
        return xh * lax.rsqrt(jnp.sum(xh * xh, axis=-1, keepdims=True) + 1e-6)

    def rms(oh):
        return oh * lax.rsqrt(jnp.mean(oh * oh, axis=-1, keepdims=True) + NORM_EPS) * nw_row

    def body(c, carry):
        r0 = pl.multiple_of(c * c_len, c_len)

        def conv(x_ref, h_ref, cw_ref):
            cur, sh = _chunk_and_shifts(x_ref, h_ref, c, c_len, 3)
            cw = cw_ref[...]
            acc = cur * cw[3:4, :]
            for s in range(1, 4):
                acc = acc + sh[s - 1] * cw[3 - s:4 - s, :]
            return _silu(acc)

        q_all = conv(q_ref, hq_ref, cq_ref)
        k_all = conv(k_ref, hk_ref, ck_ref)
        v_all = conv(v_ref, hv_ref, cv_ref)
        z_all = z_ref[pl.ds(r0, c_len), :]
        ba = ba_ref[pl.ds(r0, c_len), :]
        g_all = -jnp.exp(alog_row) * _softplus(ba + dtb_row)

        def chain(t):
            sl = slice(t * width, (t + 1) * width)
            q = per_head(q_all[:, sl], l2n) * (GDN_HEAD_DIM ** -0.5)
            k = per_head(k_all[:, sl], l2n)
            v = v_all[:, sl]
            betas, gs = [], []
            for h in range(2):
                hid = 2 * (tile0 + t) + h
                betas.append(_sigmoid(jnp.sum(jnp.where(lane_b == hid, ba, 0.0), axis=-1, keepdims=True)))
                gs.append(jnp.sum(jnp.where(lane_b == GDN_HEADS + hid, g_all, 0.0), axis=-1, keepdims=True))
            beta = jnp.where(head0_lane, betas[0], betas[1])
            g_n = jnp.where(head0_lane, gs[0], gs[1])
            g_w = jnp.where(head0_wide, gs[0], gs[1])

            gcol = _mm_sel_lhs(tril_t, g_n)
            grow = _mm_sel_lhs(ones_t, g_w * triu_w)
            yield
            gcol_w = jnp.where(head0_wide, gcol[:, :rows], gcol[:, GDN_HEAD_DIM:GDN_HEAD_DIM + rows])
            diff = gcol_w - grow
            d_strict = jnp.where(strict_w, jnp.exp(jnp.where(strict_w, diff, 0.0)), 0.0)
            d_incl = jnp.where(incl_w, jnp.exp(jnp.where(incl_w, diff, 0.0)), 0.0)
            egc = jnp.exp(gcol)
            gl_lane = gcol[c_len - 1:c_len, :]

            kb = k * beta
            prod = lax.dot_general(jnp.concatenate([kb, q], axis=0).astype(BF16), stack(k.astype(BF16)),
                                   (((1,), (1,)), ((), ())), preferred_element_type=F32)
            yield
            lower = prod[:c_len, :] * d_strict
            attn = prod[c_len:, :] * d_incl
            inv = []
            yield from _unit_lower_inverse(-lower, eye_w, c_len, stack_w, inv)
            rhs = jnp.concatenate([stack((v * beta).astype(BF16)), stack((kb * egc).astype(BF16))], axis=1)
            uw = jnp.dot(inv[0].astype(BF16), rhs, preferred_element_type=F32)
            yield
            u, w = uw[:, :width], uw[:, width:]

            state = s_ref[t]
            x0 = _mm(jnp.concatenate([w, q * egc], axis=0), state)
            yield
            v_new = u - x0[:c_len, :]
            o = x0[c_len:, :] + jnp.dot(attn.astype(BF16), stack(v_new.astype(BF16)), preferred_element_type=F32)
            ds = _mm_tn(k * jnp.exp(gl_lane - gcol), v_new)
            s_ref[t] = state * jnp.exp(gl_lane) + jnp.where(same_head, ds, 0.0)
            yield

            o_ref[pl.ds(r0, c_len), sl] = (per_head(o, rms) * _silu(z_all[:, sl])).astype(o_ref.dtype)

        _run_lockstep([chain(t) for t in range(n_tile)])
        return carry

    lax.fori_loop(0, blk_rows // c_len, body, 0)
    _save_halos(((q_ref, hq_ref), (k_ref, hk_ref), (v_ref, hv_ref)), blk_rows)


def _gdn(proj, conv_w, alog_row, dtb_row, nw_row, batch, seq, n_tile=4):
    w = n_tile * MXU_TILE
    n_grp = GDN_WIDTH // w
    blk = min(SEQ_BLOCK, seq)
    nsb = seq // blk
    colspec = lambda base: pl.BlockSpec((blk, w), lambda b, g, s, base=base: (b * nsb + s, base * n_grp + g))
    cwspec = lambda base: pl.BlockSpec((4, w), lambda b, g, s, base=base: (0, base * n_grp + g))
    rowspec = pl.BlockSpec((1, LANE), lambda b, g, s: (0, 0))
    return pl.pallas_call(
        functools.partial(_gdn_kernel, blk_rows=blk, n_tile=n_tile),
        grid=(batch, n_grp, nsb),
        in_specs=[colspec(0), colspec(1), colspec(2), colspec(COL_GZ * LANE // GDN_WIDTH),
                  pl.BlockSpec((blk, LANE), lambda b, g, s: (b * nsb + s, COL_GBA)),
                  cwspec(0), cwspec(1), cwspec(2), rowspec, rowspec, rowspec],
        out_specs=pl.BlockSpec((blk, w), lambda b, g, s: (b * nsb + s, g)),
        out_shape=jax.ShapeDtypeStruct((batch * seq, GDN_WIDTH), BF16),
        scratch_shapes=[pltpu.VMEM((n_tile, MXU_TILE, MXU_TILE), F32)] + [pltpu.VMEM((HALO, w), F32)] * 3,
        compiler_params=pltpu.CompilerParams(
            dimension_semantics=("arbitrary", "arbitrary", "arbitrary"), vmem_limit_bytes=VMEM_LIMIT),
        name="gdn",
    )(proj, proj, proj, proj, proj, conv_w, conv_w, conv_w, alog_row, dtb_row, nw_row)


def _rwkv_kernel(r_ref, k_ref, v_ref, wl_ref, al_ref, z_ref,
                 mur_ref, muk_ref, muv_ref, muwl_ref, mual_ref,
                 w0_ref, wup_ref, a0_ref, aup_ref, kk_ref, ka_ref, rk_ref, lnw_ref, lnb_ref,
                 o_ref, n_ref, hr_ref, hk_ref, hv_ref, hwl_ref, hal_ref, *, blk_rows, n_tile):
    c_len = CHUNK
    nh = MXU_TILE // RWKV_HEAD_DIM
    width = MXU_TILE
    assert c_len == RWKV_HEAD_DIM

    same_head = (_iota2((width, width), 0) // RWKV_HEAD_DIM) == (_iota2((width, width), 1) // RWKV_HEAD_DIM)
    head_sum = jnp.where(same_head, 1.0, 0.0)
    t_w = _iota2((c_len, width), 0)
    s_w = _iota2((c_len, width), 1) % c_len
    strict_w = s_w < t_w
    incl_w = s_w <= t_w
    eye_w = jnp.where(s_w == t_w, 1.0, 0.0)
    ti = _iota2((c_len, c_len), 0)
    tj = _iota2((c_len, c_len), 1)
    tril_t = jnp.where(tj <= ti, 1.0, 0.0)
    lane_head = _iota2((1, width), 1) // RWKV_HEAD_DIM
    zero_b = jnp.zeros((), BF16)

    _recurrence_grid_setup(n_ref, (hr_ref, hk_ref, hv_ref, hwl_ref, hal_ref))

    def stack(xb):
        return jnp.concatenate([jnp.where(lane_head == h, xb, zero_b) for h in range(nh)], axis=0)

    def body(c, carry):
        r0 = pl.multiple_of(c * c_len, c_len)

        def shifted(x_ref, h_ref, mu_ref):
            cur, sh = _chunk_and_shifts(x_ref, h_ref, c, c_len, 1)
            return cur + (sh[0] - cur) * mu_ref[...]

        r_all = shifted(r_ref, hr_ref, mur_ref)
        k_all = shifted(k_ref, hk_ref, muk_ref)
        v_all = shifted(v_ref, hv_ref, muv_ref)
        wl = shifted(wl_ref, hwl_ref, muwl_ref)
        al = shifted(al_ref, hal_ref, mual_ref)
        z_all = z_ref[pl.ds(r0, c_len), :]

        w_pre = w0_ref[...] + _mm(jnp.tanh(wl), wup_ref[...])
        logw_all = -jnp.exp(-_softplus(-w_pre) - 0.5)
        a_all = _sigmoid(a0_ref[...] + _mm(al, aup_ref[...]))
        kkx_all = k_all * kk_ref[...]
        k2_all = k_all * (1.0 + (a_all - 1.0) * ka_ref[...])
        lc_all = _mm_sel_lhs(tril_t, logw_all)
        bonus_all = r_all * k2_all * rk_ref[...]

        def chain(t):
            sl = slice(t * width, (t + 1) * width)
            r, v, a, k2, logw, lc = r_all[:, sl], v_all[:, sl], a_all[:, sl], k2_all[:, sl], logw_all[:, sl], lc_all[:, sl]
            kkx = kkx_all[:, sl]
            sums = _mm_sel_rhs(jnp.concatenate([kkx * kkx, bonus_all[:, sl]], axis=0), head_sum)
            yield
            kk = kkx * lax.rsqrt(sums[:c_len, :] + 1e-6)
            bonus = sums[c_len:, :]

            lc_last = lc[c_len - 1:c_len, :]
            e_neg = jnp.exp(-lc)
            e_rem = jnp.exp(lc_last - lc)
            kka = kk * a
            a_n = -kk * jnp.exp(lc - logw)
            r_n = r * jnp.exp(lc)
            sv = stack(v.astype(BF16))

            prod = lax.dot_general(
                jnp.concatenate([a_n, r_n], axis=0).astype(BF16),
                jnp.concatenate([stack((kka * e_neg).astype(BF16)), stack((k2 * e_neg).astype(BF16))], axis=0),
                (((1,), (1,)), ((), ())), preferred_element_type=F32)
            yield
            a_ab = jnp.where(strict_w, prod[:c_len, :width], 0.0)
            a_ak = jnp.where(strict_w, prod[:c_len, width:], 0.0)
            a_rb = jnp.where(incl_w, prod[c_len:, :width], 0.0)
            a_rk = jnp.where(incl_w, prod[c_len:, width:], 0.0)

            inv = []
            yield from _unit_lower_inverse(a_ab, eye_w, c_len, stack, inv)
            t_mat = inv[0].astype(BF16)
            w_p = jnp.dot(t_mat, stack(a_n.astype(BF16)), preferred_element_type=F32)
            aks = jnp.dot(a_ak.astype(BF16), sv, preferred_element_type=F32)
            yield
            v_p = jnp.dot(t_mat, stack(aks.astype(BF16)), preferred_element_type=F32)
            yield

            state = n_ref[t]
            x0 = _mm(jnp.concatenate([w_p, r_n], axis=0), state)
            yield
            u = x0[:c_len, :] + v_p
            y = x0[c_len:, :] + jnp.dot(jnp.concatenate([a_rb, a_rk], axis=1).astype(BF16),
                                        jnp.concatenate([stack(u.astype(BF16)), sv], axis=0),
                                        preferred_element_type=F32)
            dn = _mm_tn(jnp.concatenate([kka * e_rem, k2 * e_rem], axis=0), jnp.concatenate([u, v], axis=0))
            gam_col = jnp.transpose(jnp.broadcast_to(jnp.exp(lc_last), (width, width)))
            n_ref[t] = gam_col * state + jnp.where(same_head, dn, 0.0)
            yield

            inv_n = 1.0 / RWKV_HEAD_DIM
            mean = _mm_sel_rhs(y, head_sum) * inv_n
            yield
            dlt = y - mean
            var = _mm_sel_rhs(dlt * dlt, head_sum) * inv_n
            yield
            y = dlt * lax.rsqrt(var + RWKV_LN_EPS) * lnw_ref[:, sl] + lnb_ref[:, sl]
            y = y + bonus * v
            o_ref[pl.ds(r0, c_len), sl] = (y * _silu(z_all[:, sl])).astype(o_ref.dtype)

        _run_lockstep([chain(t) for t in range(n_tile)])
        return carry

    lax.fori_loop(0, blk_rows // c_len, body, 0)
    _save_halos(((r_ref, hr_ref), (k_ref, hk_ref), (v_ref, hv_ref), (wl_ref, hwl_ref), (al_ref, hal_ref)),
                blk_rows)


def _rwkv(proj, p, batch, seq, n_tile=4):
    w = n_tile * MXU_TILE
    n_grp = RWKV_WIDTH // w
    blk = min(SEQ_BLOCK, seq)
    nsb = seq // blk
    colspec = lambda base: pl.BlockSpec((blk, w), lambda b, g, s, base=base: (b * nsb + s, base * n_grp + g))
    lspec = lambda col: pl.BlockSpec((blk, LANE), lambda b, g, s, col=col: (b * nsb + s, col))
    grow = pl.BlockSpec((1, w), lambda b, g, s: (0, g))
    lrow = pl.BlockSpec((1, LANE), lambda b, g, s: (0, 0))
    upspec = pl.BlockSpec((LANE, w), lambda b, g, s: (0, g))
    base = COL_RKV * LANE // RWKV_WIDTH
    return pl.pallas_call(
        functools.partial(_rwkv_kernel, blk_rows=blk, n_tile=n_tile),
        grid=(batch, n_grp, nsb),
        in_specs=[colspec(base), colspec(base + 1), colspec(base + 2), lspec(COL_WL), lspec(COL_AL),
                  colspec(COL_RZ * LANE // RWKV_WIDTH),
                  grow, grow, grow, lrow, lrow,
                  grow, upspec, grow, upspec, grow, grow, grow, grow, grow],
        out_specs=pl.BlockSpec((blk, w), lambda b, g, s: (b * nsb + s, g)),
        out_shape=jax.ShapeDtypeStruct((batch * seq, RWKV_WIDTH), BF16),
        scratch_shapes=([pltpu.VMEM((n_tile, MXU_TILE, MXU_TILE), F32)] + [pltpu.VMEM((HALO, w), F32)] * 3
                        + [pltpu.VMEM((HALO, LANE), F32)] * 2),
        compiler_params=pltpu.CompilerParams(
            dimension_semantics=("arbitrary", "arbitrary", "arbitrary"), vmem_limit_bytes=VMEM_LIMIT),
        name="rwkv",
    )(proj, proj, proj, proj, proj, proj,
      p["mu_r"], p["mu_k"], p["mu_v"], p["mu_wl"], p["mu_al"],
      p["w0"], p["w_up"], p["a0"], p["a_up"], p["k_k"], p["k_a"], p["r_k"], p["lnx_w"], p["lnx_b"])


def _s5_kernel(*refs, seq, n_lb):
    u_refs = refs[:n_lb]
    (lre_ref, lim_ref, ldt_ref, bre_ref, bim_ref, cre_ref, cim_ref, d_ref,
     o_ref, pre_ref, pim_ref, qre_ref, qim_ref, w_ref) = refs[n_lb:]
    tile = S5_TILE
    n_pair = LANE // (2 * S5_GROUP)
    n_chain = n_lb * n_pair
    sub = HALO
    n_sub = tile // sub

    def cmul(ar, ai, br, bi):
        return ar * br - ai * bi, ar * bi + ai * br

    def block_scan(sr, si, ar, ai, out):
        in_blk = _iota2(sr.shape, 0) % sub
        d = 1
        while d < sub:
            keep = in_blk >= d
            tr, ti = cmul(ar, ai, jnp.where(keep, pltpu.roll(sr, d, 0), 0.0),
                          jnp.where(keep, pltpu.roll(si, d, 0), 0.0))
            sr, si = sr + tr, si + ti
            ar, ai = cmul(ar, ai, ar, ai)
            d *= 2
            yield
        out.extend((sr, si))

    first = _iota2((sub, LANE), 0) == 0
    ab = []
    for q in range(n_chain):
        lb, p = divmod(q, n_pair)
        lre = lre_ref[lb, p:p + 1, :]
        lim = lim_ref[lb, p:p + 1, :]
        dt = jnp.exp(ldt_ref[lb, p:p + 1, :])
        mag = jnp.exp(lre * dt)
        ab_re = mag * jnp.cos(lim * dt)
        ab_im = mag * jnp.sin(lim * dt)
        den = lre * lre + lim * lim
        coef_re = ((ab_re - 1.0) * lre + ab_im * lim) / den
        coef_im = (ab_im * lre - (ab_re - 1.0) * lim) / den
        b_re = bre_ref[lb, p]
        b_im = bim_ref[lb, p]
        w_ref[q, 0] = (coef_re * b_re - coef_im * b_im).astype(BF16)
        w_ref[q, 1] = (coef_re * b_im + coef_im * b_re).astype(BF16)
        w_ref[q, 2] = cre_ref[lb, p].astype(BF16)
        w_ref[q, 3] = cim_ref[lb, p].astype(BF16)
        pr, pi = [ab_re], [ab_im]
        for _ in range(1, n_sub):
            nr, ni = cmul(pr[-1], pi[-1], ab_re, ab_im)
            pr.append(nr)
            pi.append(ni)
        pre_ref[q] = jnp.concatenate(pr, axis=0)
        pim_ref[q] = jnp.concatenate(pi, axis=0)
        imp_re = jnp.where(first, jnp.broadcast_to(pr[-1], (sub, LANE)), 0.0)
        imp_im = jnp.where(first, jnp.broadcast_to(pi[-1], (sub, LANE)), 0.0)
        pw = []
        for _ in block_scan(imp_re, imp_im, pr[-1], pi[-1], pw):
            pass
        qre_ref[q] = pw[0]
        qim_ref[q] = pw[1]
        ab.append((ab_re, ab_im, pr[-1], pi[-1]))

    def tile_body(i, st):
        t0 = pl.multiple_of(i * tile, tile)
        us = [jnp.concatenate([u_refs[lb][pl.ds(t0 + j, sub, stride=n_sub), :] for j in range(n_sub)], axis=0)
              for lb in range(n_lb)]
        ubs = [u.astype(BF16) for u in us]
        ys, new_st = [None] * n_chain, [None] * n_chain

        def chain(p):
            cr, ci = st[p]
            ar, ai, a_run_r, a_run_i = ab[p]
            ub = ubs[p // n_pair]
            sr = jnp.dot(ub, w_ref[p, 0], preferred_element_type=F32)
            si = jnp.dot(ub, w_ref[p, 1], preferred_element_type=F32)
            yield
            loc_r, loc_i = [sr[:sub, :]], [si[:sub, :]]
            for j in range(1, n_sub):
                tr, ti = cmul(ar, ai, loc_r[-1], loc_i[-1])
                loc_r.append(sr[j * sub:(j + 1) * sub, :] + tr)
                loc_i.append(si[j * sub:(j + 1) * sub, :] + ti)
                if j % 4 == 0:
                    yield
            res = []
            yield from block_scan(loc_r[-1], loc_i[-1], a_run_r, a_run_i, res)
            tr, ti = cmul(qre_ref[p], qim_ref[p], cr, ci)
            end_r, end_i = res[0] + tr, res[1] + ti
            new_st[p] = (end_r[sub - 1:sub, :], end_i[sub - 1:sub, :])
            in_r = jnp.where(first, cr, pltpu.roll(end_r, 1, 0))
            in_i = jnp.where(first, ci, pltpu.roll(end_i, 1, 0))
            out_r, out_i = [], []
            for j in range(n_sub):
                tr, ti = cmul(pre_ref[p, j:j + 1, :], pim_ref[p, j:j + 1, :], in_r, in_i)
                out_r.append(loc_r[j] + tr)
                out_i.append(loc_i[j] + ti)
            sr, si = jnp.concatenate(out_r, axis=0), jnp.concatenate(out_i, axis=0)
            ys[p] = jnp.dot(sr.astype(BF16), w_ref[p, 2], preferred_element_type=F32) - jnp.dot(
                si.astype(BF16), w_ref[p, 3], preferred_element_type=F32)

        _run_lockstep([chain(p) for p in range(n_chain)])
        for lb in range(n_lb):
            y = d_ref[lb] * us[lb]
            for p in range(n_pair):
                y = y + ys[lb * n_pair + p]
            inner = math.sqrt(2.0 / math.pi) * (y + 0.044715 * (y * y * y))
            y = 0.5 * y * (1.0 + jnp.tanh(inner))
            for j in range(n_sub):
                o_ref[lb, pl.ds(t0 + j, sub, stride=n_sub), :] = y[j * sub:(j + 1) * sub, :]
        return tuple(new_st)

    zero = jnp.zeros((1, LANE), F32)
    lax.fori_loop(0, seq // tile, tile_body, tuple((zero, zero) for _ in range(n_chain)))


def _s5(proj, p, batch, seq, n_lb=2):
    n_blk = S5_WIDTH // (n_lb * LANE)
    n_pair = LANE // (2 * S5_GROUP)
    n_chain = n_lb * n_pair
    rowspec = pl.BlockSpec((n_lb, n_pair, LANE), lambda b, j: (j, 0, 0))
    matspec = pl.BlockSpec((n_lb, n_pair, LANE, LANE), lambda b, j: (j, 0, 0, 0))
    return pl.pallas_call(
        functools.partial(_s5_kernel, seq=seq, n_lb=n_lb),
        grid=(batch, n_blk),
        in_specs=[pl.BlockSpec((seq, LANE), lambda b, j, lb=lb: (b, COL_SU + n_lb * j + lb)) for lb in range(n_lb)]
                 + [rowspec, rowspec, rowspec, matspec, matspec, matspec, matspec,
                    pl.BlockSpec((n_lb, 1, LANE), lambda b, j: (j, 0, 0))],
        out_specs=pl.BlockSpec((n_lb, seq, LANE), lambda b, j: (j, b, 0)),
        out_shape=jax.ShapeDtypeStruct((S5_WIDTH // LANE, batch * seq, LANE), F32),
        scratch_shapes=[pltpu.VMEM((n_chain, S5_TILE // HALO, LANE), F32)] * 2
                       + [pltpu.VMEM((n_chain, HALO, LANE), F32)] * 2
                       + [pltpu.VMEM((n_chain, 4, LANE, LANE), BF16)],
        compiler_params=pltpu.CompilerParams(
            dimension_semantics=("arbitrary", "arbitrary"), vmem_limit_bytes=VMEM_LIMIT),
        name="s5_scan",
    )(*([proj] * n_lb), p["lam_re"], p["lam_im"], p["log_dt"], p["b_re"], p["b_im"], p["c_re"], p["c_im"], p["d"])


def _s5_glu_kernel(y_ref, w_ref, b_ref, z_ref, o_ref):
    y = jnp.concatenate([y_ref[j] for j in range(S5_WIDTH // LANE)], axis=1)
    gate = _sigmoid(_mm(y, w_ref[...]) + b_ref[...])
    o_ref[...] = (y * gate * _silu(z_ref[...])).astype(o_ref.dtype)


def _s5_glu(yc, proj, glu_w, glu_b, tm=512):
    t = yc.shape[1]
    tm = min(tm, t)
    return pl.pallas_call(
        _s5_glu_kernel,
        grid=(t // tm,),
        in_specs=[pl.BlockSpec((S5_WIDTH // LANE, tm, LANE), lambda i: (0, i, 0)),
                  pl.BlockSpec((S5_WIDTH, S5_WIDTH), lambda i: (0, 0)),
                  pl.BlockSpec((1, S5_WIDTH), lambda i: (0, 0)),
                  pl.BlockSpec((tm, S5_WIDTH), lambda i: (i, COL_SZ * LANE // S5_WIDTH))],
        out_specs=pl.BlockSpec((tm, S5_WIDTH), lambda i: (i, 0)),
        out_shape=jax.ShapeDtypeStruct((t, S5_WIDTH), BF16),
        compiler_params=pltpu.CompilerParams(
            dimension_semantics=("arbitrary",), vmem_limit_bytes=VMEM_LIMIT),
        name="s5_glu",
    )(yc, glu_w, glu_b, proj)


def _merge_kernel(oa_ref, ob_ref, oc_ref, wb_ref, ga_ref, gb_ref, gc_ref, gbias_ref, o_ref):
    acc = None
    for i, (o_r, g_r) in enumerate(((oa_ref, ga_ref), (ob_ref, gb_ref), (oc_ref, gc_ref))):
        proj = jnp.dot(o_r[...], wb_ref[i], preferred_element_type=F32)
        term = _sigmoid(g_r[...] + gbias_ref[i]) * proj
        acc = term if acc is None else acc + term
    o_ref[...] = acc.astype(o_ref.dtype)


def _merge(oa, ob, oc, proj, w_branch, gate_b, tm=1024, tn=512):
    t = oa.shape[0]
    tm = min(tm, t)
    gbase = COL_GATE * LANE // tn
    per = D_MODEL // tn
    ospec = pl.BlockSpec((tm, 1024), lambda i, j: (i, 0))
    gspec = lambda br: pl.BlockSpec((tm, tn), lambda i, j, br=br: (i, gbase + br * per + j))
    return pl.pallas_call(
        _merge_kernel,
        grid=(t // tm, per),
        in_specs=[ospec, ospec, ospec,
                  pl.BlockSpec((N_BRANCH, 1024, tn), lambda i, j: (0, 0, j)),
                  gspec(0), gspec(1), gspec(2),
                  pl.BlockSpec((N_BRANCH, 1, tn), lambda i, j: (0, 0, j))],
        out_specs=pl.BlockSpec((tm, tn), lambda i, j: (i, j)),
        out_shape=jax.ShapeDtypeStruct((t, D_MODEL), BF16),
        compiler_params=pltpu.CompilerParams(
            dimension_semantics=("arbitrary", "arbitrary"), vmem_limit_bytes=VMEM_LIMIT),
        name="merge",
    )(oa, ob, oc, w_branch, proj, proj, proj, gate_b)


def _outproj_kernel(m_ref, w_ref, x_ref, nw_ref, *o_refs):
    x = x_ref[...] + jnp.dot(m_ref[...], w_ref[...], preferred_element_type=F32)
    normed = _rms(x, nw_ref[...])
    if len(o_refs) == 2:
        o_refs[0][...] = x
    o_refs[-1][...] = normed.astype(o_refs[-1].dtype)


def _outproj(merged, w_out, x2, next_norm_row, last, tm=512):
    t = x2.shape[0]
    tm = min(tm, t)
    row_blk = pl.BlockSpec((tm, D_MODEL), lambda i: (i, 0))
    if last:
        out_specs, out_shape = row_blk, jax.ShapeDtypeStruct((t, D_MODEL), F32)
    else:
        out_specs = (row_blk, row_blk)
        out_shape = (jax.ShapeDtypeStruct((t, D_MODEL), F32), jax.ShapeDtypeStruct((t, D_MODEL), BF16))
    return pl.pallas_call(
        _outproj_kernel,
        grid=(t // tm,),
        in_specs=[row_blk,
                  pl.BlockSpec((D_MODEL, D_MODEL), lambda i: (0, 0)),
                  row_blk,
                  pl.BlockSpec((1, D_MODEL), lambda i: (0, 0))],
        out_specs=out_specs,
        out_shape=out_shape,
        compiler_params=pltpu.CompilerParams(
            dimension_semantics=("arbitrary",), vmem_limit_bytes=VMEM_LIMIT),
        name="outproj",
    )(merged, w_out, x2, next_norm_row)


def _pad_cols(a, width):
    return jnp.pad(a, [(0, 0)] * (a.ndim - 1) + [(0, width - a.shape[-1])])


_W_SRC_GBA = 4096
_W_SRC_RKV = _W_SRC_GBA + 2 * GDN_HEADS
_W_SRC_WL = _W_SRC_RKV + 3 * RWKV_WIDTH
_W_SRC_RZ = _W_SRC_WL + 2 * RWKV_LORA
_W_BLK = 512


def _relayout_kernel(main_ref, next_ref, gba_ref, lora_ref, o_ref):
    j = pl.program_id(1)
    first_rkv, first_rz, tail = COL_RKV * LANE // _W_BLK, COL_RZ * LANE // _W_BLK, COL_GBA * LANE // _W_BLK

    def window(shift):
        cat = jnp.concatenate([main_ref[...], next_ref[...]], axis=1)
        return cat[:, shift:shift + _W_BLK]

    @pl.when(j < first_rkv)
    def _():
        o_ref[...] = main_ref[...].astype(BF16)

    @pl.when((j >= first_rkv) & (j < first_rz))
    def _():
        o_ref[...] = window(_W_SRC_RKV - COL_RKV * LANE).astype(BF16)

    @pl.when((j >= first_rz) & (j < tail))
    def _():
        o_ref[...] = window(_W_SRC_RZ - COL_RZ * LANE).astype(BF16)

    @pl.when(j == tail)
    def _():
        lane = _iota2((1, LANE), 1)
        wl0 = _W_SRC_WL - COL_RZ * LANE
        gba = jnp.where(lane < 2 * GDN_HEADS, gba_ref[:, :LANE], 0.0)
        wl = jnp.where(lane < RWKV_LORA, lora_ref[:, wl0:wl0 + LANE], 0.0)
        al = jnp.where(lane < RWKV_LORA, lora_ref[:, wl0 + RWKV_LORA:wl0 + RWKV_LORA + LANE], 0.0)
        o_ref[...] = jnp.concatenate([gba, wl, al, jnp.zeros_like(gba)], axis=1).astype(BF16)


def _relayout_w_in(w, rows=512):
    k = w.shape[0]
    tail = COL_GBA * LANE // _W_BLK
    src = lambda j: jnp.minimum(j, tail - 1)
    half = _W_BLK // 2
    return pl.pallas_call(
        _relayout_kernel,
        grid=(k // rows, NP // _W_BLK),
        in_specs=[pl.BlockSpec((rows, _W_BLK), lambda r, j: (r, src(j))),
                  pl.BlockSpec((rows, half), lambda r, j: (r, 2 * (src(j) + 1))),
                  pl.BlockSpec((rows, _W_BLK), lambda r, j: (r, _W_SRC_GBA // _W_BLK)),
                  pl.BlockSpec((rows, _W_BLK), lambda r, j: (r, COL_RZ * LANE // _W_BLK))],
        out_specs=pl.BlockSpec((rows, _W_BLK), lambda r, j: (r, j)),
        out_shape=jax.ShapeDtypeStruct((k, NP), BF16),
        compiler_params=pltpu.CompilerParams(
            dimension_semantics=("arbitrary", "arbitrary"), vmem_limit_bytes=VMEM_LIMIT),
        name="relayout_w",
    )(w, w, w, w)


def _place_s5(a, rows_are_channels):
    n_blk, n_pair = S5_WIDTH // LANE, LANE // (2 * S5_GROUP)
    grp_in_blk = LANE // S5_GROUP
    cs = a if not rows_are_channels else jnp.swapaxes(a, 1, 2)
    cs = cs.reshape(n_blk, n_pair, 2, S5_STATE, S5_GROUP)
    slot = jnp.arange(grp_in_blk)[None, :, None] == (2 * jnp.arange(n_pair)[:, None, None]
                                                      + jnp.arange(2)[None, None, :])
    placed = jnp.where(slot[None, :, :, None, :, None],
                       jnp.transpose(cs, (0, 1, 4, 2, 3))[:, :, None, :, :, :], 0.0)
    placed = placed.reshape(n_blk, n_pair, LANE, LANE)
    return placed if not rows_are_channels else jnp.swapaxes(placed, 2, 3)


def _s5_rows(a):
    return a.reshape(S5_WIDTH // LANE, LANE // (2 * S5_GROUP), 2 * S5_STATE)


def _gdn_head_row(a):
    return _pad_cols(jnp.pad(a, (GDN_HEADS, 0)).reshape(1, 2 * GDN_HEADS), LANE)


def _rwkv_params(mu, w0, w_up, a0, a_up, k_k, k_a, r_k, lnx_w, lnx_b):
    row = lambda a: a.reshape(1, -1)
    lora = lambda a: jnp.pad(a, ((0, LANE - RWKV_LORA), (0, 0))).astype(BF16)
    w3 = 3 * RWKV_WIDTH
    return dict(mu_r=row(mu[:RWKV_WIDTH]), mu_k=row(mu[RWKV_WIDTH:2 * RWKV_WIDTH]), mu_v=row(mu[2 * RWKV_WIDTH:w3]),
                mu_wl=_pad_cols(row(mu[w3:w3 + RWKV_LORA]), LANE), mu_al=_pad_cols(row(mu[w3 + RWKV_LORA:]), LANE),
                w0=row(w0), w_up=lora(w_up), a0=row(a0), a_up=lora(a_up), k_k=row(k_k), k_a=row(k_a),
                r_k=row(r_k), lnx_w=row(lnx_w), lnx_b=row(lnx_b))


def _s5_params(a_re, a_im, log_dt, b_re, b_im, c_re, c_im, d):
    return dict(lam_re=_s5_rows(a_re), lam_im=_s5_rows(a_im),
                log_dt=_s5_rows(jnp.broadcast_to(log_dt[:, None], (S5_WIDTH // S5_GROUP, S5_STATE))),
                b_re=_place_s5(b_re, False), b_im=_place_s5(b_im, False),
                c_re=_place_s5(c_re, True), c_im=_place_s5(c_im, True),
                d=d.reshape(S5_WIDTH // LANE, 1, LANE))


def kernel(x, norm_w, w_in, gdn_conv_w, gdn_a_log, gdn_dt_bias, gdn_norm_w, rwkv_mu, rwkv_w0, rwkv_w_up,
           rwkv_a0, rwkv_a_up, rwkv_k_k, rwkv_k_a, rwkv_r_k, rwkv_lnx_w, rwkv_lnx_b, s5_a_re, s5_a_im,
           s5_log_dt, s5_b_re, s5_b_im, s5_c_re, s5_c_im, s5_d, s5_glu_w, s5_glu_b, gate_b, w_branch,
           w_out, final_norm_w):
    batch, seq, _ = x.shape
    depth = w_in.shape[0]
    x2 = x.reshape(batch * seq, D_MODEL)
    h = _rmsnorm(x2, norm_w[0].reshape(1, D_MODEL))

    for i in range(depth):
        proj = _inproj(h, _relayout_w_in(w_in[i]))

        o_a = _gdn(proj, gdn_conv_w[i], _gdn_head_row(gdn_a_log[i]), _gdn_head_row(gdn_dt_bias[i]),
                   gdn_norm_w[i].reshape(1, GDN_HEAD_DIM), batch, seq)

        rp = _rwkv_params(rwkv_mu[i], rwkv_w0[i], rwkv_w_up[i], rwkv_a0[i], rwkv_a_up[i], rwkv_k_k[i],
                          rwkv_k_a[i], rwkv_r_k[i], rwkv_lnx_w[i], rwkv_lnx_b[i])
        o_b = _rwkv(proj, rp, batch, seq)

        sp = _s5_params(s5_a_re[i], s5_a_im[i], s5_log_dt[i], s5_b_re[i], s5_b_im[i], s5_c_re[i], s5_c_im[i],
                        s5_d[i])
        y_c = _s5(proj, sp, batch, seq)
        o_c = _s5_glu(y_c, proj, s5_glu_w[i].astype(BF16), s5_glu_b[i].reshape(1, S5_WIDTH))

        merged = _merge(o_a, o_b, o_c, proj, w_branch[i].astype(BF16),
                        gate_b[i].reshape(N_BRANCH, 1, D_MODEL))
        if i == depth - 1:
            return _outproj(merged, w_out[i].astype(BF16), x2, final_norm_w.reshape(1, D_MODEL), True
                            ).reshape(batch, seq, D_MODEL)
        x2, h = _outproj(merged, w_out[i].astype(BF16), x2, norm_w[i + 1].reshape(1, D_MODEL), False)
```

```python
import functools
import math

import jax
import jax.numpy as jnp
from jax import lax
from jax.experimental import pallas as pl
from jax.experimental.pallas import tpu as pltpu

F32 = jnp.float32
BF16 = jnp.bfloat16

D_MODEL = 2048
GDN_HEADS = 8
GDN_HEAD_DIM = 128
GDN_WIDTH = 1024
RWKV_HEAD_DIM = 64
RWKV_WIDTH = 1024
RWKV_LORA = 96
RWKV_LN_EPS = 64e-5
S5_GROUP = 16
S5_STATE = 64
S5_WIDTH = 1024
N_BRANCH = 3
NORM_EPS = 1e-6

LANE = 128
MXU_TILE = 256
HALO = 8
CHUNK = 64
SEQ_BLOCK = 512
S5_TILE = 256
VMEM_LIMIT = 52 * 1024 * 1024

COL_QKV = 0
COL_GZ = 24
COL_RKV = 32
COL_RZ = 56
COL_SU = 64
COL_SZ = 72
COL_GATE = 80
COL_GBA = 128
COL_WL = 129
COL_AL = 130
N_COLBLK = 132
NP = N_COLBLK * LANE


def _sigmoid(x):
    return 1.0 / (1.0 + jnp.exp(-x))


def _silu(x):
    return x * _sigmoid(x)


def _softplus(x):
    return jnp.maximum(x, 0.0) + jnp.log1p(jnp.exp(-jnp.abs(x)))


def _mm(a, b):
    return jnp.dot(a.astype(BF16), b.astype(BF16), preferred_element_type=F32)


def _mm_nt(a, b):
    return lax.dot_general(a.astype(BF16), b.astype(BF16), (((1,), (1,)), ((), ())),
                           preferred_element_type=F32)


def _mm_tn(a, b):
    return lax.dot_general(a.astype(BF16), b.astype(BF16), (((0,), (0,)), ((), ())),
                           preferred_element_type=F32)


def _split_bf16(x, parts):
    out = []
    for _ in range(parts - 1):
        hi = x.astype(BF16)
        out.append(hi)
        x = x - hi.astype(F32)
    out.append(x.astype(BF16))
    return out


def _mm_sel_lhs(sel, x, parts=3):
    cols = x.shape[1]
    res = jnp.dot(sel.astype(BF16), jnp.concatenate(_split_bf16(x, parts), axis=1), preferred_element_type=F32)
    out = res[:, :cols]
    for i in range(1, parts):
        out = out + res[:, i * cols:(i + 1) * cols]
    return out


def _mm_sel_rhs(x, sel, parts=2):
    rows = x.shape[0]
    res = jnp.dot(jnp.concatenate(_split_bf16(x, parts), axis=0), sel.astype(BF16), preferred_element_type=F32)
    out = res[:rows, :]
    for i in range(1, parts):
        out = out + res[i * rows:(i + 1) * rows, :]
    return out


def _unit_lower_inverse(x, eye, size, stack, out):
    rows = x.shape[0]
    y = x
    q = eye + x
    y = jnp.dot(x.astype(BF16), stack(x.astype(BF16)), preferred_element_type=F32)
    yield
    span = 4
    while span < size:
        res = jnp.dot(jnp.concatenate([y, q], axis=0).astype(BF16), stack(y.astype(BF16)),
                      preferred_element_type=F32)
        yield
        y, q = res[:rows, :], q + res[rows:, :]
        span *= 2
    q = q + jnp.dot(q.astype(BF16), stack(y.astype(BF16)), preferred_element_type=F32)
    yield
    out.append(q)


def _run_lockstep(chains):
    live = list(chains)
    while live:
        for g in list(live):
            try:
                next(g)
            except StopIteration:
                live.remove(g)


def _iota2(shape, dim):
    return lax.broadcasted_iota(jnp.int32, shape, dim)


def _chunk_and_shifts(ref, halo_ref, c, rows, shift):
    r0 = pl.multiple_of(c * rows, rows)
    h0 = pl.multiple_of(jnp.maximum(r0 - HALO, 0), HALO)
    halo = jnp.where(c == 0, halo_ref[...], ref[pl.ds(h0, HALO), :])
    cur = ref[pl.ds(r0, rows), :]
    xc = jnp.concatenate([halo, cur], axis=0)
    return cur, [pltpu.roll(xc, s, 0)[HALO:, :] for s in range(1, shift + 1)]


def _recurrence_grid_setup(state_ref, halo_refs):
    @pl.when(pl.program_id(2) == 0)
    def _():
        state_ref[...] = jnp.zeros_like(state_ref)
        for h in halo_refs:
            h[...] = jnp.zeros_like(h)


def _save_halos(pairs, rows):
    for src, dst in pairs:
        dst[...] = src[rows - HALO:rows, :]


def _rms(x, w_row):
    return x * lax.rsqrt(jnp.mean(x * x, axis=-1, keepdims=True) + NORM_EPS) * w_row


def _rmsnorm_kernel(x_ref, nw_ref, o_ref):
    o_ref[...] = _rms(x_ref[...], nw_ref[...]).astype(o_ref.dtype)


def _rmsnorm(x2, norm_w_row, tm=512):
    t = x2.shape[0]
    tm = min(tm, t)
    return pl.pallas_call(
        _rmsnorm_kernel,
        grid=(t // tm,),
        in_specs=[pl.BlockSpec((tm, D_MODEL), lambda i: (i, 0)),
                  pl.BlockSpec((1, D_MODEL), lambda i: (0, 0))],
        out_specs=pl.BlockSpec((tm, D_MODEL), lambda i: (i, 0)),
        out_shape=jax.ShapeDtypeStruct((t, D_MODEL), BF16),
        compiler_params=pltpu.CompilerParams(dimension_semantics=("arbitrary",), vmem_limit_bytes=VMEM_LIMIT),
        name="rmsnorm",
    )(x2, norm_w_row)


def _inproj_kernel(h_ref, w_ref, o_ref):
    o_ref[...] = jnp.dot(h_ref[...], w_ref[...], preferred_element_type=F32)


def _inproj(h, w_pad, tm=2048, tn=768):
    t = h.shape[0]
    tm = min(tm, t)
    return pl.pallas_call(
        _inproj_kernel,
        grid=(t // tm, NP // tn),
        in_specs=[pl.BlockSpec((tm, D_MODEL), lambda i, j: (i, 0)),
                  pl.BlockSpec((D_MODEL, tn), lambda i, j: (0, j))],
        out_specs=pl.BlockSpec((tm, tn), lambda i, j: (i, j)),
        out_shape=jax.ShapeDtypeStruct((t, NP), F32),
        compiler_params=pltpu.CompilerParams(
            dimension_semantics=("arbitrary", "arbitrary"), vmem_limit_bytes=VMEM_LIMIT),
        name="inproj",
    )(h, w_pad)


def _gdn_kernel(q_ref, k_ref, v_ref, z_ref, ba_ref, cq_ref, ck_ref, cv_ref,
                alog_ref, dtb_ref, nw_ref, o_ref, s_ref, hq_ref, hk_ref, hv_ref, *, blk_rows, n_tile):
    c_len = CHUNK
    rows = 2 * c_len
    width = MXU_TILE
    tile0 = pl.program_id(1) * n_tile

    lane_w = _iota2((1, width), 1)
    lane_b = _iota2((1, LANE), 1)
    t_w = _iota2((c_len, rows), 0)
    s_w = _iota2((c_len, rows), 1) % c_len
    strict_w = s_w < t_w
    incl_w = s_w <= t_w
    eye_w = jnp.where(s_w == t_w, 1.0, 0.0)
    triu_w = jnp.where(t_w <= s_w, 1.0, 0.0)
    ti = _iota2((c_len, c_len), 0)
    tj = _iota2((c_len, c_len), 1)
    tril_t = jnp.where(tj <= ti, 1.0, 0.0)
    ones_t = jnp.ones((c_len, c_len), F32)
    same_head = (_iota2((width, width), 0) // GDN_HEAD_DIM) == (_iota2((width, width), 1) // GDN_HEAD_DIM)
    head0_lane = lane_w < GDN_HEAD_DIM
    head0_wide = _iota2((1, rows), 1) < c_len
    zero_b = jnp.zeros((), BF16)

    _recurrence_grid_setup(s_ref, (hq_ref, hk_ref, hv_ref))
    alog_row = alog_ref[...]
    dtb_row = dtb_ref[...]
    nw_row = nw_ref[...]

    def stack(xb):
        return jnp.concatenate([jnp.where(head0_lane, xb, zero_b), jnp.where(head0_lane, zero_b, xb)], axis=0)

    def stack_w(xb):
        return jnp.concatenate([jnp.where(head0_wide, xb, zero_b), jnp.where(head0_wide, zero_b, xb)], axis=0)

    def per_head(x, fn):
        return jnp.concatenate([fn(x[:, :GDN_HEAD_DIM]), fn(x[:, GDN_HEAD_DIM:])], axis=1)

    def l2n(xh):
        return xh * lax.rsqrt(jnp.sum(xh * xh, axis=-1, keepdims=True) + 1e-6)

    def rms(oh):
        return oh * lax.rsqrt(jnp.mean(oh * oh, axis=-1, keepdims=True) + NORM_EPS) * nw_row

    def body(c, carry):
        r0 = pl.multiple_of(c * c_len, c_len)

        def conv(x_ref, h_ref, cw_ref):
            cur, sh = _chunk_and_shifts(x_ref, h_ref, c, c_len, 3)
            cw = cw_ref[...]
            acc = cur * cw[3:4, :]
            for s in range(1, 4):
                acc = acc + sh[s - 1] * cw[3 - s:4 - s, :]
            return _silu(acc)

        q_all = conv(q_ref, hq_ref, cq_ref)
        k_all = conv(k_ref, hk_ref, ck_ref)
        v_all = conv(v_ref, hv_ref, cv_ref)
        z_all = z_ref[pl.ds(r0, c_len), :]
        ba = ba_ref[pl.ds(r0, c_len), :]
        g_all = -jnp.exp(alog_row) * _softplus(ba + dtb_row)

        def chain(t):
            sl = slice(t * width, (t + 1) * width)
            q = per_head(q_all[:, sl], l2n) * (GDN_HEAD_DIM ** -0.5)
            k = per_head(k_all[:, sl], l2n)
            v = v_all[:, sl]
            betas, gs = [], []
            for h in range(2):
                hid = 2 * (tile0 + t) + h
                betas.append(_sigmoid(jnp.sum(jnp.where(lane_b == hid, ba, 0.0), axis=-1, keepdims=True)))
                gs.append(jnp.sum(jnp.where(lane_b == GDN_HEADS + hid, g_all, 0.0), axis=-1, keepdims=True))
            beta = jnp.where(head0_lane, betas[0], betas[1])
            g_n = jnp.where(head0_lane, gs[0], gs[1])
            g_w = jnp.where(head0_wide, gs[0], gs[1])

            gcol = _mm_sel_lhs(tril_t, g_n)
            grow = _mm_sel_lhs(ones_t, g_w * triu_w)
            yield
            gcol_w = jnp.where(head0_wide, gcol[:, :rows], gcol[:, GDN_HEAD_DIM:GDN_HEAD_DIM + rows])
            diff = gcol_w - grow
            d_strict = jnp.where(strict_w, jnp.exp(jnp.where(strict_w, diff, 0.0)), 0.0)
            d_incl = jnp.where(incl_w, jnp.exp(jnp.where(incl_w, diff, 0.0)), 0.0)
            egc = jnp.exp(gcol)
            gl_lane = gcol[c_len - 1:c_len, :]

            kb = k * beta
            prod = lax.dot_general(jnp.concatenate([kb, q], axis=0).astype(BF16), stack(k.astype(BF16)),
                                   (((1,), (1,)), ((), ())), preferred_element_type=F32)
            yield
            lower = prod[:c_len, :] * d_strict
            attn = prod[c_len:, :] * d_incl
            inv = []
            yield from _unit_lower_inverse(-lower, eye_w, c_len, stack_w, inv)
            rhs = jnp.concatenate([stack((v * beta).astype(BF16)), stack((kb * egc).astype(BF16))], axis=1)
            uw = jnp.dot(inv[0].astype(BF16), rhs, preferred_element_type=F32)
            yield
            u, w = uw[:, :width], uw[:, width:]

            state = s_ref[t]
            x0 = _mm(jnp.concatenate([w, q * egc], axis=0), state)
            yield
            v_new = u - x0[:c_len, :]
            o = x0[c_len:, :] + jnp.dot(attn.astype(BF16), stack(v_new.astype(BF16)), preferred_element_type=F32)
            ds = _mm_tn(k * jnp.exp(gl_lane - gcol), v_new)
            s_ref[t] = state * jnp.exp(gl_lane) + jnp.where(same_head, ds, 0.0)
            yield

            o_ref[pl.ds(r0, c_len), sl] = (per_head(o, rms) * _silu(z_all[:, sl])).astype(o_ref.dtype)

        _run_lockstep([chain(t) for t in range(n_tile)])
        return carry

    lax.fori_loop(0, blk_rows // c_len, body, 0)
    _save_halos(((q_ref, hq_ref), (k_ref, hk_ref), (v_ref, hv_ref)), blk_rows)


def _gdn(proj, conv_w, alog_row, dtb_row, nw_row, layer, batch, seq, n_tile=4):
    w = n_tile * MXU_TILE
    n_grp = GDN_WIDTH // w
    blk = min(SEQ_BLOCK, seq)
    nsb = seq // blk
    colspec = lambda base: pl.BlockSpec((blk, w), lambda b, g, s, base=base: (b * nsb + s, base * n_grp + g))
    cwspec = lambda base: pl.BlockSpec((None, 4, w), lambda b, g, s, base=base: (layer, 0, base * n_grp + g))
    rowspec = pl.BlockSpec((None, 1, LANE), lambda b, g, s: (layer, 0, 0))
    return pl.pallas_call(
        functools.partial(_gdn_kernel, blk_rows=blk, n_tile=n_tile),
        grid=(batch, n_grp, nsb),
        in_specs=[colspec(0), colspec(1), colspec(2), colspec(COL_GZ * LANE // GDN_WIDTH),
                  pl.BlockSpec((blk, LANE), lambda b, g, s: (b * nsb + s, COL_GBA)),
                  cwspec(0), cwspec(1), cwspec(2), rowspec, rowspec, rowspec],
        out_specs=pl.BlockSpec((blk, w), lambda b, g, s: (b * nsb + s, g)),
        out_shape=jax.ShapeDtypeStruct((batch * seq, GDN_WIDTH), BF16),
        scratch_shapes=[pltpu.VMEM((n_tile, MXU_TILE, MXU_TILE), F32)] + [pltpu.VMEM((HALO, w), F32)] * 3,
        compiler_params=pltpu.CompilerParams(
            dimension_semantics=("arbitrary", "arbitrary", "arbitrary"), vmem_limit_bytes=VMEM_LIMIT),
        name="gdn",
    )(proj, proj, proj, proj, proj, conv_w, conv_w, conv_w, alog_row, dtb_row, nw_row)


def _rwkv_kernel(r_ref, k_ref, v_ref, wl_ref, al_ref, z_ref,
                 mur_ref, muk_ref, muv_ref, muwl_ref, mual_ref,
                 w0_ref, wup_ref, a0_ref, aup_ref, kk_ref, ka_ref, rk_ref, lnw_ref, lnb_ref,
                 o_ref, n_ref, hr_ref, hk_ref, hv_ref, hwl_ref, hal_ref, *, blk_rows, n_tile):
    c_len = CHUNK
    nh = MXU_TILE // RWKV_HEAD_DIM
    width = MXU_TILE
    assert c_len == RWKV_HEAD_DIM

    same_head = (_iota2((width, width), 0) // RWKV_HEAD_DIM) == (_iota2((width, width), 1) // RWKV_HEAD_DIM)
    head_sum = jnp.where(same_head, 1.0, 0.0)
    t_w = _iota2((c_len, width), 0)
    s_w = _iota2((c_len, width), 1) % c_len
    strict_w = s_w < t_w
    incl_w = s_w <= t_w
    eye_w = jnp.where(s_w == t_w, 1.0, 0.0)
    ti = _iota2((c_len, c_len), 0)
    tj = _iota2((c_len, c_len), 1)
    tril_t = jnp.where(tj <= ti, 1.0, 0.0)
    lane_head = _iota2((1, width), 1) // RWKV_HEAD_DIM
    zero_b = jnp.zeros((), BF16)

    _recurrence_grid_setup(n_ref, (hr_ref, hk_ref, hv_ref, hwl_ref, hal_ref))

    def stack(xb):
        return jnp.concatenate([jnp.where(lane_head == h, xb, zero_b) for h in range(nh)], axis=0)

    def body(c, carry):
        r0 = pl.multiple_of(c * c_len, c_len)

        def shifted(x_ref, h_ref, mu_ref):
            cur, sh = _chunk_and_shifts(x_ref, h_ref, c, c_len, 1)
            return cur + (sh[0] - cur) * mu_ref[...]

        r_all = shifted(r_ref, hr_ref, mur_ref)
        k_all = shifted(k_ref, hk_ref, muk_ref)
        v_all = shifted(v_ref, hv_ref, muv_ref)
        wl = shifted(wl_ref, hwl_ref, muwl_ref)
        al = shifted(al_ref, hal_ref, mual_ref)
        z_all = z_ref[pl.ds(r0, c_len), :]

        w_pre = w0_ref[...] + _mm(jnp.tanh(wl), wup_ref[...])
        logw_all = -jnp.exp(-_softplus(-w_pre) - 0.5)
        a_all = _sigmoid(a0_ref[...] + _mm(al, aup_ref[...]))
        kkx_all = k_all * kk_ref[...]
        k2_all = k_all * (1.0 + (a_all - 1.0) * ka_ref[...])
        lc_all = _mm_sel_lhs(tril_t, logw_all)
        bonus_all = r_all * k2_all * rk_ref[...]

        def chain(t):
            sl = slice(t * width, (t + 1) * width)
            r, v, a, k2, logw, lc = r_all[:, sl], v_all[:, sl], a_all[:, sl], k2_all[:, sl], logw_all[:, sl], lc_all[:, sl]
            kkx = kkx_all[:, sl]
            sums = _mm_sel_rhs(jnp.concatenate([kkx * kkx, bonus_all[:, sl]], axis=0), head_sum)
            yield
            kk = kkx * lax.rsqrt(sums[:c_len, :] + 1e-6)
            bonus = sums[c_len:, :]

            lc_last = lc[c_len - 1:c_len, :]
            e_neg = jnp.exp(-lc)
            e_rem = jnp.exp(lc_last - lc)
            kka = kk * a
            a_n = -kk * jnp.exp(lc - logw)
            r_n = r * jnp.exp(lc)
            sv = stack(v.astype(BF16))

            prod = lax.dot_general(
                jnp.concatenate([a_n, r_n], axis=0).astype(BF16),
                jnp.concatenate([stack((kka * e_neg).astype(BF16)), stack((k2 * e_neg).astype(BF16))], axis=0),
                (((1,), (1,)), ((), ())), preferred_element_type=F32)
            yield
            a_ab = jnp.where(strict_w, prod[:c_len, :width], 0.0)
            a_ak = jnp.where(strict_w, prod[:c_len, width:], 0.0)
            a_rb = jnp.where(incl_w, prod[c_len:, :width], 0.0)
            a_rk = jnp.where(incl_w, prod[c_len:, width:], 0.0)

            inv = []
            yield from _unit_lower_inverse(a_ab, eye_w, c_len, stack, inv)
            t_mat = inv[0].astype(BF16)
            w_p = jnp.dot(t_mat, stack(a_n.astype(BF16)), preferred_element_type=F32)
            aks = jnp.dot(a_ak.astype(BF16), sv, preferred_element_type=F32)
            yield
            v_p = jnp.dot(t_mat, stack(aks.astype(BF16)), preferred_element_type=F32)
            yield

            state = n_ref[t]
            x0 = _mm(jnp.concatenate([w_p, r_n], axis=0), state)
            yield
            u = x0[:c_len, :] + v_p
            y = x0[c_len:, :] + jnp.dot(jnp.concatenate([a_rb, a_rk], axis=1).astype(BF16),
                                        jnp.concatenate([stack(u.astype(BF16)), sv], axis=0),
                                        preferred_element_type=F32)
            dn = _mm_tn(jnp.concatenate([kka * e_rem, k2 * e_rem], axis=0), jnp.concatenate([u, v], axis=0))
            gam_col = jnp.transpose(jnp.broadcast_to(jnp.exp(lc_last), (width, width)))
            n_ref[t] = gam_col * state + jnp.where(same_head, dn, 0.0)
            yield

            inv_n = 1.0 / RWKV_HEAD_DIM
            mean = _mm_sel_rhs(y, head_sum) * inv_n
            yield
            dlt = y - mean
            var = _mm_sel_rhs(dlt * dlt, head_sum) * inv_n
            yield
            y = dlt * lax.rsqrt(var + RWKV_LN_EPS) * lnw_ref[:, sl] + lnb_ref[:, sl]
            y = y + bonus * v
            o_ref[pl.ds(r0, c_len), sl] = (y * _silu(z_all[:, sl])).astype(o_ref.dtype)

        _run_lockstep([chain(t) for t in range(n_tile)])
        return carry

    lax.fori_loop(0, blk_rows // c_len, body, 0)
    _save_halos(((r_ref, hr_ref), (k_ref, hk_ref), (v_ref, hv_ref), (wl_ref, hwl_ref), (al_ref, hal_ref)),
                blk_rows)


def _rwkv(proj, p, layer, batch, seq, n_tile=4):
    w = n_tile * MXU_TILE
    n_grp = RWKV_WIDTH // w
    blk = min(SEQ_BLOCK, seq)
    nsb = seq // blk
    colspec = lambda base: pl.BlockSpec((blk, w), lambda b, g, s, base=base: (b * nsb + s, base * n_grp + g))
    lspec = lambda col: pl.BlockSpec((blk, LANE), lambda b, g, s, col=col: (b * nsb + s, col))
    grow = pl.BlockSpec((None, 1, w), lambda b, g, s: (layer, 0, g))
    lrow = pl.BlockSpec((None, 1, LANE), lambda b, g, s: (layer, 0, 0))
    upspec = pl.BlockSpec((None, LANE, w), lambda b, g, s: (layer, 0, g))
    base = COL_RKV * LANE // RWKV_WIDTH
    return pl.pallas_call(
        functools.partial(_rwkv_kernel, blk_rows=blk, n_tile=n_tile),
        grid=(batch, n_grp, nsb),
        in_specs=[colspec(base), colspec(base + 1), colspec(base + 2), lspec(COL_WL), lspec(COL_AL),
                  colspec(COL_RZ * LANE // RWKV_WIDTH),
                  grow, grow, grow, lrow, lrow,
                  grow, upspec, grow, upspec, grow, grow, grow, grow, grow],
        out_specs=pl.BlockSpec((blk, w), lambda b, g, s: (b * nsb + s, g)),
        out_shape=jax.ShapeDtypeStruct((batch * seq, RWKV_WIDTH), BF16),
        scratch_shapes=([pltpu.VMEM((n_tile, MXU_TILE, MXU_TILE), F32)] + [pltpu.VMEM((HALO, w), F32)] * 3
                        + [pltpu.VMEM((HALO, LANE), F32)] * 2),
        compiler_params=pltpu.CompilerParams(
            dimension_semantics=("arbitrary", "arbitrary", "arbitrary"), vmem_limit_bytes=VMEM_LIMIT),
        name="rwkv",
    )(proj, proj, proj, proj, proj, proj,
      p["mu_r"], p["mu_k"], p["mu_v"], p["mu_wl"], p["mu_al"],
      p["w0"], p["w_up"], p["a0"], p["a_up"], p["k_k"], p["k_a"], p["r_k"], p["lnx_w"], p["lnx_b"])


def _s5_kernel(*refs, seq, n_lb):
    u_refs = refs[:n_lb]
    (lre_ref, lim_ref, ldt_ref, bre_ref, bim_ref, cre_ref, cim_ref, d_ref,
     o_ref, pre_ref, pim_ref, qre_ref, qim_ref, w_ref) = refs[n_lb:]
    tile = S5_TILE
    n_pair = LANE // (2 * S5_GROUP)
    n_chain = n_lb * n_pair
    sub = HALO
    n_sub = tile // sub

    def cmul(ar, ai, br, bi):
        return ar * br - ai * bi, ar * bi + ai * br

    def block_scan(sr, si, ar, ai, out):
        in_blk = _iota2(sr.shape, 0) % sub
        d = 1
        while d < sub:
            keep = in_blk >= d
            tr, ti = cmul(ar, ai, jnp.where(keep, pltpu.roll(sr, d, 0), 0.0),
                          jnp.where(keep, pltpu.roll(si, d, 0), 0.0))
            sr, si = sr + tr, si + ti
            ar, ai = cmul(ar, ai, ar, ai)
            d *= 2
            yield
        out.extend((sr, si))

    first = _iota2((sub, LANE), 0) == 0
    ab = []
    for q in range(n_chain):
        lb, p = divmod(q, n_pair)
        lre = lre_ref[lb, p:p + 1, :]
        lim = lim_ref[lb, p:p + 1, :]
        dt = jnp.exp(ldt_ref[lb, p:p + 1, :])
        mag = jnp.exp(lre * dt)
        ab_re = mag * jnp.cos(lim * dt)
        ab_im = mag * jnp.sin(lim * dt)
        den = lre * lre + lim * lim
        coef_re = ((ab_re - 1.0) * lre + ab_im * lim) / den
        coef_im = (ab_im * lre - (ab_re - 1.0) * lim) / den
        b_re = bre_ref[lb, p]
        b_im = bim_ref[lb, p]
        w_ref[q, 0] = (coef_re * b_re - coef_im * b_im).astype(BF16)
        w_ref[q, 1] = (coef_re * b_im + coef_im * b_re).astype(BF16)
        w_ref[q, 2] = cre_ref[lb, p].astype(BF16)
        w_ref[q, 3] = cim_ref[lb, p].astype(BF16)
        pr, pi = [ab_re], [ab_im]
        for _ in range(1, n_sub):
            nr, ni = cmul(pr[-1], pi[-1], ab_re, ab_im)
            pr.append(nr)
            pi.append(ni)
        pre_ref[q] = jnp.concatenate(pr, axis=0)
        pim_ref[q] = jnp.concatenate(pi, axis=0)
        imp_re = jnp.where(first, jnp.broadcast_to(pr[-1], (sub, LANE)), 0.0)
        imp_im = jnp.where(first, jnp.broadcast_to(pi[-1], (sub, LANE)), 0.0)
        pw = []
        for _ in block_scan(imp_re, imp_im, pr[-1], pi[-1], pw):
            pass
        qre_ref[q] = pw[0]
        qim_ref[q] = pw[1]
        ab.append((ab_re, ab_im, pr[-1], pi[-1]))

    def tile_body(i, st):
        t0 = pl.multiple_of(i * tile, tile)
        us = [jnp.concatenate([u_refs[lb][pl.ds(t0 + j, sub, stride=n_sub), :] for j in range(n_sub)], axis=0)
              for lb in range(n_lb)]
        ubs = [u.astype(BF16) for u in us]
        ys, new_st = [None] * n_chain, [None] * n_chain

        def chain(p):
            cr, ci = st[p]
            ar, ai, a_run_r, a_run_i = ab[p]
            ub = ubs[p // n_pair]
            sr = jnp.dot(ub, w_ref[p, 0], preferred_element_type=F32)
            si = jnp.dot(ub, w_ref[p, 1], preferred_element_type=F32)
            yield
            loc_r, loc_i = [sr[:sub, :]], [si[:sub, :]]
            for j in range(1, n_sub):
                tr, ti = cmul(ar, ai, loc_r[-1], loc_i[-1])
                loc_r.append(sr[j * sub:(j + 1) * sub, :] + tr)
                loc_i.append(si[j * sub:(j + 1) * sub, :] + ti)
                if j % 4 == 0:
                    yield
            res = []
            yield from block_scan(loc_r[-1], loc_i[-1], a_run_r, a_run_i, res)
            tr, ti = cmul(qre_ref[p], qim_ref[p], cr, ci)
            end_r, end_i = res[0] + tr, res[1] + ti
            new_st[p] = (end_r[sub - 1:sub, :], end_i[sub - 1:sub, :])
            in_r = jnp.where(first, cr, pltpu.roll(end_r, 1, 0))
            in_i = jnp.where(first, ci, pltpu.roll(end_i, 1, 0))
            out_r, out_i = [], []
            for j in range(n_sub):
                tr, ti = cmul(pre_ref[p, j:j + 1, :], pim_ref[p, j:j + 1, :], in_r, in_i)
                out_r.append(loc_r[j] + tr)
                out_i.append(loc_i[j] + ti)
            sr, si = jnp.concatenate(out_r, axis=0), jnp.concatenate(out_i, axis=0)
            ys[p] = jnp.dot(sr.astype(BF16), w_ref[p, 2], preferred_element_type=F32) - jnp.dot(
                si.astype(BF16), w_ref[p, 3], preferred_element_type=F32)

        _run_lockstep([chain(p) for p in range(n_chain)])
        for lb in range(n_lb):
            y = d_ref[lb] * us[lb]
            for p in range(n_pair):
                y = y + ys[lb * n_pair + p]
            inner = math.sqrt(2.0 / math.pi) * (y + 0.044715 * (y * y * y))
            y = 0.5 * y * (1.0 + jnp.tanh(inner))
            for j in range(n_sub):
                o_ref[lb, pl.ds(t0 + j, sub, stride=n_sub), :] = y[j * sub:(j + 1) * sub, :]
        return tuple(new_st)

    zero = jnp.zeros((1, LANE), F32)
    lax.fori_loop(0, seq // tile, tile_body, tuple((zero, zero) for _ in range(n_chain)))


def _s5(proj, p, layer, batch, seq, n_lb=2):
    n_blk = S5_WIDTH // (n_lb * LANE)
    n_pair = LANE // (2 * S5_GROUP)
    n_chain = n_lb * n_pair
    rowspec = pl.BlockSpec((None, n_lb, n_pair, LANE), lambda b, j: (layer, j, 0, 0))
    matspec = pl.BlockSpec((None, n_lb, n_pair, LANE, LANE), lambda b, j: (layer, j, 0, 0, 0))
    return pl.pallas_call(
        functools.partial(_s5_kernel, seq=seq, n_lb=n_lb),
        grid=(batch, n_blk),
        in_specs=[pl.BlockSpec((seq, LANE), lambda b, j, lb=lb: (b, COL_SU + n_lb * j + lb)) for lb in range(n_lb)]
                 + [rowspec, rowspec, rowspec, matspec, matspec, matspec, matspec,
                    pl.BlockSpec((None, n_lb, 1, LANE), lambda b, j: (layer, j, 0, 0))],
        out_specs=pl.BlockSpec((n_lb, seq, LANE), lambda b, j: (j, b, 0)),
        out_shape=jax.ShapeDtypeStruct((S5_WIDTH // LANE, batch * seq, LANE), F32),
        scratch_shapes=[pltpu.VMEM((n_chain, S5_TILE // HALO, LANE), F32)] * 2
                       + [pltpu.VMEM((n_chain, HALO, LANE), F32)] * 2
                       + [pltpu.VMEM((n_chain, 4, LANE, LANE), BF16)],
        compiler_params=pltpu.CompilerParams(
            dimension_semantics=("arbitrary", "arbitrary"), vmem_limit_bytes=VMEM_LIMIT),
        name="s5_scan",
    )(*([proj] * n_lb), p["lam_re"], p["lam_im"], p["log_dt"], p["b_re"], p["b_im"], p["c_re"], p["c_im"], p["d"])


def _s5_glu_kernel(y_ref, w_ref, b_ref, z_ref, o_ref):
    y = jnp.concatenate([y_ref[j] for j in range(S5_WIDTH // LANE)], axis=1)
    gate = _sigmoid(_mm(y, w_ref[...]) + b_ref[...])
    o_ref[...] = (y * gate * _silu(z_ref[...])).astype(o_ref.dtype)


def _s5_glu(yc, proj, glu_w, glu_b, layer, tm=512):
    t = yc.shape[1]
    tm = min(tm, t)
    return pl.pallas_call(
        _s5_glu_kernel,
        grid=(t // tm,),
        in_specs=[pl.BlockSpec((S5_WIDTH // LANE, tm, LANE), lambda i: (0, i, 0)),
                  pl.BlockSpec((None, S5_WIDTH, S5_WIDTH), lambda i: (layer, 0, 0)),
                  pl.BlockSpec((None, 1, S5_WIDTH), lambda i: (layer, 0, 0)),
                  pl.BlockSpec((tm, S5_WIDTH), lambda i: (i, COL_SZ * LANE // S5_WIDTH))],
        out_specs=pl.BlockSpec((tm, S5_WIDTH), lambda i: (i, 0)),
        out_shape=jax.ShapeDtypeStruct((t, S5_WIDTH), BF16),
        compiler_params=pltpu.CompilerParams(
            dimension_semantics=("arbitrary",), vmem_limit_bytes=VMEM_LIMIT),
        name="s5_glu",
    )(yc, glu_w, glu_b, proj)


def _merge_kernel(oa_ref, ob_ref, oc_ref, wb_ref, ga_ref, gb_ref, gc_ref, gbias_ref, o_ref):
    acc = None
    for i, (o_r, g_r) in enumerate(((oa_ref, ga_ref), (ob_ref, gb_ref), (oc_ref, gc_ref))):
        proj = jnp.dot(o_r[...], wb_ref[i], preferred_element_type=F32)
        term = _sigmoid(g_r[...] + gbias_ref[i]) * proj
        acc = term if acc is None else acc + term
    o_ref[...] = acc.astype(o_ref.dtype)


def _merge(oa, ob, oc, proj, w_branch, gate_b, layer, tm=1024, tn=512):
    t = oa.shape[0]
    tm = min(tm, t)
    gbase = COL_GATE * LANE // tn
    per = D_MODEL // tn
    ospec = pl.BlockSpec((tm, 1024), lambda i, j: (i, 0))
    gspec = lambda br: pl.BlockSpec((tm, tn), lambda i, j, br=br: (i, gbase + br * per + j))
    return pl.pallas_call(
        _merge_kernel,
        grid=(t // tm, per),
        in_specs=[ospec, ospec, ospec,
                  pl.BlockSpec((None, N_BRANCH, 1024, tn), lambda i, j: (layer, 0, 0, j)),
                  gspec(0), gspec(1), gspec(2),
                  pl.BlockSpec((None, N_BRANCH, 1, tn), lambda i, j: (layer, 0, 0, j))],
        out_specs=pl.BlockSpec((tm, tn), lambda i, j: (i, j)),
        out_shape=jax.ShapeDtypeStruct((t, D_MODEL), BF16),
        compiler_params=pltpu.CompilerParams(
            dimension_semantics=("arbitrary", "arbitrary"), vmem_limit_bytes=VMEM_LIMIT),
        name="merge",
    )(oa, ob, oc, w_branch, proj, proj, proj, gate_b)


def _outproj_kernel(m_ref, w_ref, x_ref, nw_ref, *o_refs):
    x = x_ref[...] + jnp.dot(m_ref[...], w_ref[...], preferred_element_type=F32)
    normed = _rms(x, nw_ref[...])
    if len(o_refs) == 2:
        o_refs[0][...] = x
    o_refs[-1][...] = normed.astype(o_refs[-1].dtype)


def _outproj(merged, w_out, x2, next_norm_rows, layer, last, tm=512):
    t = x2.shape[0]
    tm = min(tm, t)
    row_blk = pl.BlockSpec((tm, D_MODEL), lambda i: (i, 0))
    if last:
        out_specs, out_shape = row_blk, jax.ShapeDtypeStruct((t, D_MODEL), F32)
    else:
        out_specs = (row_blk, row_blk)
        out_shape = (jax.ShapeDtypeStruct((t, D_MODEL), F32), jax.ShapeDtypeStruct((t, D_MODEL), BF16))
    return pl.pallas_call(
        _outproj_kernel,
        grid=(t // tm,),
        in_specs=[row_blk,
                  pl.BlockSpec((None, D_MODEL, D_MODEL), lambda i: (layer, 0, 0)),
                  row_blk,
                  pl.BlockSpec((None, 1, D_MODEL), lambda i: (layer, 0, 0))],
        out_specs=out_specs,
        out_shape=out_shape,
        compiler_params=pltpu.CompilerParams(
            dimension_semantics=("arbitrary",), vmem_limit_bytes=VMEM_LIMIT),
        name="outproj",
    )(merged, w_out, x2, next_norm_rows)


def _pad_cols(a, width):
    return jnp.pad(a, [(0, 0)] * (a.ndim - 1) + [(0, width - a.shape[-1])])


_W_SRC_GBA = 4096
_W_SRC_RKV = _W_SRC_GBA + 2 * GDN_HEADS
_W_SRC_WL = _W_SRC_RKV + 3 * RWKV_WIDTH
_W_SRC_RZ = _W_SRC_WL + 2 * RWKV_LORA
_W_BLK = 512


def _relayout_kernel(main_ref, next_ref, gba_ref, lora_ref, o_ref):
    j = pl.program_id(1)
    first_rkv, first_rz, tail = COL_RKV * LANE // _W_BLK, COL_RZ * LANE // _W_BLK, COL_GBA * LANE // _W_BLK

    def window(shift):
        cat = jnp.concatenate([main_ref[...], next_ref[...]], axis=1)
        return cat[:, shift:shift + _W_BLK]

    @pl.when(j < first_rkv)
    def _():
        o_ref[...] = main_ref[...].astype(BF16)

    @pl.when((j >= first_rkv) & (j < first_rz))
    def _():
        o_ref[...] = window(_W_SRC_RKV - COL_RKV * LANE).astype(BF16)

    @pl.when((j >= first_rz) & (j < tail))
    def _():
        o_ref[...] = window(_W_SRC_RZ - COL_RZ * LANE).astype(BF16)

    @pl.when(j == tail)
    def _():
        lane = _iota2((1, LANE), 1)
        wl0 = _W_SRC_WL - COL_RZ * LANE
        gba = jnp.where(lane < 2 * GDN_HEADS, gba_ref[:, :LANE], 0.0)
        wl = jnp.where(lane < RWKV_LORA, lora_ref[:, wl0:wl0 + LANE], 0.0)
        al = jnp.where(lane < RWKV_LORA, lora_ref[:, wl0 + RWKV_LORA:wl0 + RWKV_LORA + LANE], 0.0)
        o_ref[...] = jnp.concatenate([gba, wl, al, jnp.zeros_like(gba)], axis=1).astype(BF16)


def _relayout_w_in(w, layer, rows=1024):
    k = w.shape[1]
    tail = COL_GBA * LANE // _W_BLK
    src = lambda j: jnp.minimum(j, tail - 1)
    half = _W_BLK // 2
    return pl.pallas_call(
        _relayout_kernel,
        grid=(k // rows, NP // _W_BLK),
        in_specs=[pl.BlockSpec((None, rows, _W_BLK), lambda r, j: (layer, r, src(j))),
                  pl.BlockSpec((None, rows, half), lambda r, j: (layer, r, 2 * (src(j) + 1))),
                  pl.BlockSpec((None, rows, _W_BLK), lambda r, j: (layer, r, _W_SRC_GBA // _W_BLK)),
                  pl.BlockSpec((None, rows, _W_BLK), lambda r, j: (layer, r, COL_RZ * LANE // _W_BLK))],
        out_specs=pl.BlockSpec((rows, _W_BLK), lambda r, j: (r, j)),
        out_shape=jax.ShapeDtypeStruct((k, NP), BF16),
        compiler_params=pltpu.CompilerParams(
            dimension_semantics=("arbitrary", "arbitrary"), vmem_limit_bytes=VMEM_LIMIT),
        name="relayout_w",
    )(w, w, w, w)


def _place_s5(a, rows_are_channels):
    n_blk, n_pair = S5_WIDTH // LANE, LANE // (2 * S5_GROUP)
    grp_in_blk = LANE // S5_GROUP
    cs = a if not rows_are_channels else jnp.swapaxes(a, 1, 2)
    cs = cs.reshape(n_blk, n_pair, 2, S5_STATE, S5_GROUP)
    slot = jnp.arange(grp_in_blk)[None, :, None] == (2 * jnp.arange(n_pair)[:, None, None]
                                                      + jnp.arange(2)[None, None, :])
    placed = jnp.where(slot[None, :, :, None, :, None],
                       jnp.transpose(cs, (0, 1, 4, 2, 3))[:, :, None, :, :, :], 0.0)
    placed = placed.reshape(n_blk, n_pair, LANE, LANE)
    return placed if not rows_are_channels else jnp.swapaxes(placed, 2, 3)


def _s5_rows(a):
    return a.reshape(S5_WIDTH // LANE, LANE // (2 * S5_GROUP), 2 * S5_STATE)


def _rows(a):
    return a.reshape(a.shape[0], 1, -1)


def _gdn_head_rows(a):
    return _pad_cols(_rows(jnp.pad(a, ((0, 0), (GDN_HEADS, 0)))), LANE)


def _rwkv_params(mu, w0, w_up, a0, a_up, k_k, k_a, r_k, lnx_w, lnx_b):
    lora = lambda a: jnp.pad(a, ((0, 0), (0, LANE - RWKV_LORA), (0, 0))).astype(BF16)
    w3 = 3 * RWKV_WIDTH
    return dict(mu_r=_rows(mu[:, :RWKV_WIDTH]), mu_k=_rows(mu[:, RWKV_WIDTH:2 * RWKV_WIDTH]),
                mu_v=_rows(mu[:, 2 * RWKV_WIDTH:w3]),
                mu_wl=_pad_cols(_rows(mu[:, w3:w3 + RWKV_LORA]), LANE),
                mu_al=_pad_cols(_rows(mu[:, w3 + RWKV_LORA:]), LANE),
                w0=_rows(w0), w_up=lora(w_up), a0=_rows(a0), a_up=lora(a_up), k_k=_rows(k_k), k_a=_rows(k_a),
                r_k=_rows(r_k), lnx_w=_rows(lnx_w), lnx_b=_rows(lnx_b))


def _s5_params(a_re, a_im, log_dt, b_re, b_im, c_re, c_im, d):
    depth = a_re.shape[0]
    rows = jax.vmap(_s5_rows)
    place = lambda a, t: jax.vmap(lambda m: _place_s5(m, t))(a)
    return dict(lam_re=rows(a_re), lam_im=rows(a_im),
                log_dt=rows(jnp.broadcast_to(log_dt[:, :, None], (depth, S5_WIDTH // S5_GROUP, S5_STATE))),
                b_re=place(b_re, False), b_im=place(b_im, False),
                c_re=place(c_re, True), c_im=place(c_im, True),
                d=d.reshape(depth, S5_WIDTH // LANE, 1, LANE))


def kernel(x, norm_w, w_in, gdn_conv_w, gdn_a_log, gdn_dt_bias, gdn_norm_w, rwkv_mu, rwkv_w0, rwkv_w_up,
           rwkv_a0, rwkv_a_up, rwkv_k_k, rwkv_k_a, rwkv_r_k, rwkv_lnx_w, rwkv_lnx_b, s5_a_re, s5_a_im,
           s5_log_dt, s5_b_re, s5_b_im, s5_c_re, s5_c_im, s5_d, s5_glu_w, s5_glu_b, gate_b, w_branch,
           w_out, final_norm_w):
    batch, seq, _ = x.shape
    depth = w_in.shape[0]
    x2 = x.reshape(batch * seq, D_MODEL)
    h = _rmsnorm(x2, norm_w[0].reshape(1, D_MODEL))

    gdn_alog, gdn_dtb, gdn_nw = _gdn_head_rows(gdn_a_log), _gdn_head_rows(gdn_dt_bias), _rows(gdn_norm_w)
    rp = _rwkv_params(rwkv_mu, rwkv_w0, rwkv_w_up, rwkv_a0, rwkv_a_up, rwkv_k_k, rwkv_k_a, rwkv_r_k,
                      rwkv_lnx_w, rwkv_lnx_b)
    sp = _s5_params(s5_a_re, s5_a_im, s5_log_dt, s5_b_re, s5_b_im, s5_c_re, s5_c_im, s5_d)
    glu_w, glu_b = s5_glu_w.astype(BF16), _rows(s5_glu_b)
    wb, gb = w_branch.astype(BF16), gate_b.reshape(depth, N_BRANCH, 1, D_MODEL)
    wo = w_out.astype(BF16)
    next_norm = _rows(jnp.concatenate([norm_w[1:], final_norm_w[None, :]], axis=0))

    for i in range(depth):
        proj = _inproj(h, _relayout_w_in(w_in, i))
        o_a = _gdn(proj, gdn_conv_w, gdn_alog, gdn_dtb, gdn_nw, i, batch, seq)
        o_b = _rwkv(proj, rp, i, batch, seq)
        o_c = _s5_glu(_s5(proj, sp, i, batch, seq), proj, glu_w, glu_b, i)
        merged = _merge(o_a, o_b, o_c, proj, wb, gb, i)
        if i == depth - 1:
            return _outproj(merged, wo, x2, next_norm, i, True).reshape(batch, seq, D_MODEL)
        x2, h = _outproj(merged, wo, x2, next_norm, i, False)
```

```python
import functools
import math

import jax
import jax.numpy as jnp
from jax import lax
from jax.experimental import pallas as pl
from jax.experimental.pallas import tpu as pltpu

F32 = jnp.float32
BF16 = jnp.bfloat16

D_MODEL = 2048
GDN_HEADS = 8
GDN_HEAD_DIM = 128
GDN_WIDTH = 1024
RWKV_HEAD_DIM = 64
RWKV_WIDTH = 1024
RWKV_LORA = 96
RWKV_LN_EPS = 64e-5
S5_GROUP = 16
S5_STATE = 64
S5_WIDTH = 1024
N_BRANCH = 3
NORM_EPS = 1e-6

LANE = 128
MXU_TILE = 256
HALO = 8
CHUNK = 64
SEQ_BLOCK = 512
S5_TILE = 256
VMEM_LIMIT = 52 * 1024 * 1024

COL_QKV = 0
COL_GZ = 24
COL_RKV = 32
COL_RZ = 56
COL_SU = 64
COL_SZ = 72
COL_GATE = 80
COL_GBA = 128
COL_WL = 129
COL_AL = 130
N_COLBLK = 132
NP = N_COLBLK * LANE


def _sigmoid(x):
    return 1.0 / (1.0 + jnp.exp(-x))


def _silu(x):
    return x * _sigmoid(x)


def _softplus(x):
    return jnp.maximum(x, 0.0) + jnp.log1p(jnp.exp(-jnp.abs(x)))


def _mm(a, b):
    return jnp.dot(a.astype(BF16), b.astype(BF16), preferred_element_type=F32)


def _mm_nt(a, b):
    return lax.dot_general(a.astype(BF16), b.astype(BF16), (((1,), (1,)), ((), ())),
                           preferred_element_type=F32)


def _mm_tn(a, b):
    return lax.dot_general(a.astype(BF16), b.astype(BF16), (((0,), (0,)), ((), ())),
                           preferred_element_type=F32)


def _split_bf16(x, parts):
    out = []
    for _ in range(parts - 1):
        hi = x.astype(BF16)
        out.append(hi)
        x = x - hi.astype(F32)
    out.append(x.astype(BF16))
    return out


def _mm_sel_lhs(sel, x, parts=3):
    cols = x.shape[1]
    res = jnp.dot(sel.astype(BF16), jnp.concatenate(_split_bf16(x, parts), axis=1), preferred_element_type=F32)
    out = res[:, :cols]
    for i in range(1, parts):
        out = out + res[:, i * cols:(i + 1) * cols]
    return out


def _mm_sel_rhs(x, sel, parts=2):
    rows = x.shape[0]
    res = jnp.dot(jnp.concatenate(_split_bf16(x, parts), axis=0), sel.astype(BF16), preferred_element_type=F32)
    out = res[:rows, :]
    for i in range(1, parts):
        out = out + res[i * rows:(i + 1) * rows, :]
    return out


def _unit_lower_inverse(x, eye, size, stack, out):
    rows = x.shape[0]
    y = x
    q = eye + x
    y = jnp.dot(x.astype(BF16), stack(x.astype(BF16)), preferred_element_type=F32)
    yield
    span = 4
    while span < size:
        res = jnp.dot(jnp.concatenate([y, q], axis=0).astype(BF16), stack(y.astype(BF16)),
                      preferred_element_type=F32)
        yield
        y, q = res[:rows, :], q + res[rows:, :]
        span *= 2
    q = q + jnp.dot(q.astype(BF16), stack(y.astype(BF16)), preferred_element_type=F32)
    yield
    out.append(q)


def _run_lockstep(chains):
    live = list(chains)
    while live:
        for g in list(live):
            try:
                next(g)
            except StopIteration:
                live.remove(g)


def _iota2(shape, dim):
    return lax.broadcasted_iota(jnp.int32, shape, dim)


def _chunk_and_shifts(ref, halo_ref, c, rows, shift):
    r0 = pl.multiple_of(c * rows, rows)
    h0 = pl.multiple_of(jnp.maximum(r0 - HALO, 0), HALO)
    halo = jnp.where(c == 0, halo_ref[...], ref[pl.ds(h0, HALO), :])
    cur = ref[pl.ds(r0, rows), :]
    xc = jnp.concatenate([halo, cur], axis=0)
    return cur, [pltpu.roll(xc, s, 0)[HALO:, :] for s in range(1, shift + 1)]


def _recurrence_grid_setup(state_ref, halo_refs):
    @pl.when(pl.program_id(2) == 0)
    def _():
        state_ref[...] = jnp.zeros_like(state_ref)
        for h in halo_refs:
            h[...] = jnp.zeros_like(h)


def _save_halos(pairs, rows):
    for src, dst in pairs:
        dst[...] = src[rows - HALO:rows, :]


def _rms(x, w_row):
    return x * lax.rsqrt(jnp.mean(x * x, axis=-1, keepdims=True) + NORM_EPS) * w_row


def _rmsnorm_kernel(x_ref, nw_ref, o_ref):
    o_ref[...] = _rms(x_ref[...], nw_ref[...]).astype(o_ref.dtype)


def _rmsnorm(x2, norm_w_row, tm=512):
    t = x2.shape[0]
    tm = min(tm, t)
    return pl.pallas_call(
        _rmsnorm_kernel,
        grid=(t // tm,),
        in_specs=[pl.BlockSpec((tm, D_MODEL), lambda i: (i, 0)),
                  pl.BlockSpec((1, D_MODEL), lambda i: (0, 0))],
        out_specs=pl.BlockSpec((tm, D_MODEL), lambda i: (i, 0)),
        out_shape=jax.ShapeDtypeStruct((t, D_MODEL), BF16),
        compiler_params=pltpu.CompilerParams(dimension_semantics=("arbitrary",), vmem_limit_bytes=VMEM_LIMIT),
        name="rmsnorm",
    )(x2, norm_w_row)


def _inproj_kernel(h_ref, w_ref, o_ref):
    o_ref[...] = jnp.dot(h_ref[...], w_ref[...], preferred_element_type=F32)


def _inproj(h, w_pad, tm=2048, tn=768):
    t = h.shape[0]
    tm = min(tm, t)
    return pl.pallas_call(
        _inproj_kernel,
        grid=(t // tm, NP // tn),
        in_specs=[pl.BlockSpec((tm, D_MODEL), lambda i, j: (i, 0)),
                  pl.BlockSpec((D_MODEL, tn), lambda i, j: (0, j))],
        out_specs=pl.BlockSpec((tm, tn), lambda i, j: (i, j)),
        out_shape=jax.ShapeDtypeStruct((t, NP), F32),
        compiler_params=pltpu.CompilerParams(
            dimension_semantics=("arbitrary", "arbitrary"), vmem_limit_bytes=VMEM_LIMIT),
        name="inproj",
    )(h, w_pad)


def _gdn_parts(q_ref, k_ref, v_ref, z_ref, ba_ref, cq_ref, ck_ref, cv_ref,
               alog_ref, dtb_ref, nw_ref, o_ref, s_ref, hq_ref, hk_ref, hv_ref, *, blk_rows, n_tile):
    c_len = CHUNK
    rows = 2 * c_len
    width = MXU_TILE
    tile0 = pl.program_id(1) * n_tile

    lane_w = _iota2((1, width), 1)
    lane_b = _iota2((1, LANE), 1)
    t_w = _iota2((c_len, rows), 0)
    s_w = _iota2((c_len, rows), 1) % c_len
    strict_w = s_w < t_w
    incl_w = s_w <= t_w
    eye_w = jnp.where(s_w == t_w, 1.0, 0.0)
    triu_w = jnp.where(t_w <= s_w, 1.0, 0.0)
    ti = _iota2((c_len, c_len), 0)
    tj = _iota2((c_len, c_len), 1)
    tril_t = jnp.where(tj <= ti, 1.0, 0.0)
    ones_t = jnp.ones((c_len, c_len), F32)
    same_head = (_iota2((width, width), 0) // GDN_HEAD_DIM) == (_iota2((width, width), 1) // GDN_HEAD_DIM)
    head0_lane = lane_w < GDN_HEAD_DIM
    head0_wide = _iota2((1, rows), 1) < c_len
    zero_b = jnp.zeros((), BF16)

    _recurrence_grid_setup(s_ref, (hq_ref, hk_ref, hv_ref))
    alog_row = alog_ref[...]
    dtb_row = dtb_ref[...]
    nw_row = nw_ref[...]

    def stack(xb):
        return jnp.concatenate([jnp.where(head0_lane, xb, zero_b), jnp.where(head0_lane, zero_b, xb)], axis=0)

    def stack_w(xb):
        return jnp.concatenate([jnp.where(head0_wide, xb, zero_b), jnp.where(head0_wide, zero_b, xb)], axis=0)

    def per_head(x, fn):
        return jnp.concatenate([fn(x[:, :GDN_HEAD_DIM]), fn(x[:, GDN_HEAD_DIM:])], axis=1)

    def l2n(xh):
        return xh * lax.rsqrt(jnp.sum(xh * xh, axis=-1, keepdims=True) + 1e-6)

    def rms(oh):
        return oh * lax.rsqrt(jnp.mean(oh * oh, axis=-1, keepdims=True) + NORM_EPS) * nw_row

    def chains(c):
        r0 = pl.multiple_of(c * c_len, c_len)

        def conv(x_ref, h_ref, cw_ref):
            cur, sh = _chunk_and_shifts(x_ref, h_ref, c, c_len, 3)
            cw = cw_ref[...]
            acc = cur * cw[3:4, :]
            for s in range(1, 4):
                acc = acc + sh[s - 1] * cw[3 - s:4 - s, :]
            return _silu(acc)

        q_all = conv(q_ref, hq_ref, cq_ref)
        k_all = conv(k_ref, hk_ref, ck_ref)
        v_all = conv(v_ref, hv_ref, cv_ref)
        z_all = z_ref[pl.ds(r0, c_len), :]
        ba = ba_ref[pl.ds(r0, c_len), :]
        g_all = -jnp.exp(alog_row) * _softplus(ba + dtb_row)

        def chain(t):
            sl = slice(t * width, (t + 1) * width)
            q = per_head(q_all[:, sl], l2n) * (GDN_HEAD_DIM ** -0.5)
            k = per_head(k_all[:, sl], l2n)
            v = v_all[:, sl]
            betas, gs = [], []
            for h in range(2):
                hid = 2 * (tile0 + t) + h
                betas.append(_sigmoid(jnp.sum(jnp.where(lane_b == hid, ba, 0.0), axis=-1, keepdims=True)))
                gs.append(jnp.sum(jnp.where(lane_b == GDN_HEADS + hid, g_all, 0.0), axis=-1, keepdims=True))
            beta = jnp.where(head0_lane, betas[0], betas[1])
            g_n = jnp.where(head0_lane, gs[0], gs[1])
            g_w = jnp.where(head0_wide, gs[0], gs[1])

            gcol = _mm_sel_lhs(tril_t, g_n)
            grow = _mm_sel_lhs(ones_t, g_w * triu_w)
            yield
            gcol_w = jnp.where(head0_wide, gcol[:, :rows], gcol[:, GDN_HEAD_DIM:GDN_HEAD_DIM + rows])
            diff = gcol_w - grow
            d_strict = jnp.where(strict_w, jnp.exp(jnp.where(strict_w, diff, 0.0)), 0.0)
            d_incl = jnp.where(incl_w, jnp.exp(jnp.where(incl_w, diff, 0.0)), 0.0)
            egc = jnp.exp(gcol)
            gl_lane = gcol[c_len - 1:c_len, :]

            kb = k * beta
            prod = lax.dot_general(jnp.concatenate([kb, q], axis=0).astype(BF16), stack(k.astype(BF16)),
                                   (((1,), (1,)), ((), ())), preferred_element_type=F32)
            yield
            lower = prod[:c_len, :] * d_strict
            attn = prod[c_len:, :] * d_incl
            inv = []
            yield from _unit_lower_inverse(-lower, eye_w, c_len, stack_w, inv)
            rhs = jnp.concatenate([stack((v * beta).astype(BF16)), stack((kb * egc).astype(BF16))], axis=1)
            uw = jnp.dot(inv[0].astype(BF16), rhs, preferred_element_type=F32)
            yield
            u, w = uw[:, :width], uw[:, width:]

            state = s_ref[t]
            x0 = _mm(jnp.concatenate([w, q * egc], axis=0), state)
            yield
            v_new = u - x0[:c_len, :]
            o = x0[c_len:, :] + jnp.dot(attn.astype(BF16), stack(v_new.astype(BF16)), preferred_element_type=F32)
            ds = _mm_tn(k * jnp.exp(gl_lane - gcol), v_new)
            s_ref[t] = state * jnp.exp(gl_lane) + jnp.where(same_head, ds, 0.0)
            yield

            o_ref[pl.ds(r0, c_len), sl] = (per_head(o, rms) * _silu(z_all[:, sl])).astype(o_ref.dtype)

        return [chain(t) for t in range(n_tile)]

    def finish():
        _save_halos(((q_ref, hq_ref), (k_ref, hk_ref), (v_ref, hv_ref)), blk_rows)

    return chains, finish


def _gdn_specs(proj, conv_w, alog_row, dtb_row, nw_row, layer, blk, nsb):
    w = GDN_WIDTH
    colspec = lambda base: pl.BlockSpec((blk, w), lambda b, g, s, base=base: (b * nsb + s, base))
    cwspec = lambda base: pl.BlockSpec((None, 4, w), lambda b, g, s, base=base: (layer, 0, base))
    rowspec = pl.BlockSpec((None, 1, LANE), lambda b, g, s: (layer, 0, 0))
    in_specs = [colspec(0), colspec(1), colspec(2), colspec(COL_GZ * LANE // GDN_WIDTH),
                pl.BlockSpec((blk, LANE), lambda b, g, s: (b * nsb + s, COL_GBA)),
                cwspec(0), cwspec(1), cwspec(2), rowspec, rowspec, rowspec]
    scratch = [pltpu.VMEM((w // MXU_TILE, MXU_TILE, MXU_TILE), F32)] + [pltpu.VMEM((HALO, w), F32)] * 3
    return (proj, proj, proj, proj, proj, conv_w, conv_w, conv_w, alog_row, dtb_row, nw_row), in_specs, scratch


def _rwkv_parts(r_ref, k_ref, v_ref, wl_ref, al_ref, z_ref,
                mur_ref, muk_ref, muv_ref, muwl_ref, mual_ref,
                w0_ref, wup_ref, a0_ref, aup_ref, kk_ref, ka_ref, rk_ref, lnw_ref, lnb_ref,
                o_ref, n_ref, hr_ref, hk_ref, hv_ref, hwl_ref, hal_ref, *, blk_rows, n_tile):
    c_len = CHUNK
    nh = MXU_TILE // RWKV_HEAD_DIM
    width = MXU_TILE
    assert c_len == RWKV_HEAD_DIM

    same_head = (_iota2((width, width), 0) // RWKV_HEAD_DIM) == (_iota2((width, width), 1) // RWKV_HEAD_DIM)
    head_sum = jnp.where(same_head, 1.0, 0.0)
    t_w = _iota2((c_len, width), 0)
    s_w = _iota2((c_len, width), 1) % c_len
    strict_w = s_w < t_w
    incl_w = s_w <= t_w
    eye_w = jnp.where(s_w == t_w, 1.0, 0.0)
    ti = _iota2((c_len, c_len), 0)
    tj = _iota2((c_len, c_len), 1)
    tril_t = jnp.where(tj <= ti, 1.0, 0.0)
    lane_head = _iota2((1, width), 1) // RWKV_HEAD_DIM
    zero_b = jnp.zeros((), BF16)

    _recurrence_grid_setup(n_ref, (hr_ref, hk_ref, hv_ref, hwl_ref, hal_ref))

    def stack(xb):
        return jnp.concatenate([jnp.where(lane_head == h, xb, zero_b) for h in range(nh)], axis=0)

    def chains(c):
        r0 = pl.multiple_of(c * c_len, c_len)

        def shifted(x_ref, h_ref, mu_ref):
            cur, sh = _chunk_and_shifts(x_ref, h_ref, c, c_len, 1)
            return cur + (sh[0] - cur) * mu_ref[...]

        r_all = shifted(r_ref, hr_ref, mur_ref)
        k_all = shifted(k_ref, hk_ref, muk_ref)
        v_all = shifted(v_ref, hv_ref, muv_ref)
        wl = shifted(wl_ref, hwl_ref, muwl_ref)
        al = shifted(al_ref, hal_ref, mual_ref)
        z_all = z_ref[pl.ds(r0, c_len), :]

        w_pre = w0_ref[...] + _mm(jnp.tanh(wl), wup_ref[...])
        logw_all = -jnp.exp(-_softplus(-w_pre) - 0.5)
        a_all = _sigmoid(a0_ref[...] + _mm(al, aup_ref[...]))
        kkx_all = k_all * kk_ref[...]
        k2_all = k_all * (1.0 + (a_all - 1.0) * ka_ref[...])
        lc_all = _mm_sel_lhs(tril_t, logw_all)
        bonus_all = r_all * k2_all * rk_ref[...]

        def chain(t):
            sl = slice(t * width, (t + 1) * width)
            r, v, a, k2, logw, lc = r_all[:, sl], v_all[:, sl], a_all[:, sl], k2_all[:, sl], logw_all[:, sl], lc_all[:, sl]
            kkx = kkx_all[:, sl]
            sums = _mm_sel_rhs(jnp.concatenate([kkx * kkx, bonus_all[:, sl]], axis=0), head_sum)
            yield
            kk = kkx * lax.rsqrt(sums[:c_len, :] + 1e-6)
            bonus = sums[c_len:, :]

            lc_last = lc[c_len - 1:c_len, :]
            e_neg = jnp.exp(-lc)
            e_rem = jnp.exp(lc_last - lc)
            kka = kk * a
            a_n = -kk * jnp.exp(lc - logw)
            r_n = r * jnp.exp(lc)
            sv = stack(v.astype(BF16))

            prod = lax.dot_general(
                jnp.concatenate([a_n, r_n], axis=0).astype(BF16),
                jnp.concatenate([stack((kka * e_neg).astype(BF16)), stack((k2 * e_neg).astype(BF16))], axis=0),
                (((1,), (1,)), ((), ())), preferred_element_type=F32)
            yield
            a_ab = jnp.where(strict_w, prod[:c_len, :width], 0.0)
            a_ak = jnp.where(strict_w, prod[:c_len, width:], 0.0)
            a_rb = jnp.where(incl_w, prod[c_len:, :width], 0.0)
            a_rk = jnp.where(incl_w, prod[c_len:, width:], 0.0)

            inv = []
            yield from _unit_lower_inverse(a_ab, eye_w, c_len, stack, inv)
            t_mat = inv[0].astype(BF16)
            w_p = jnp.dot(t_mat, stack(a_n.astype(BF16)), preferred_element_type=F32)
            aks = jnp.dot(a_ak.astype(BF16), sv, preferred_element_type=F32)
            yield
            v_p = jnp.dot(t_mat, stack(aks.astype(BF16)), preferred_element_type=F32)
            yield

            state = n_ref[t]
            x0 = _mm(jnp.concatenate([w_p, r_n], axis=0), state)
            yield
            u = x0[:c_len, :] + v_p
            y = x0[c_len:, :] + jnp.dot(jnp.concatenate([a_rb, a_rk], axis=1).astype(BF16),
                                        jnp.concatenate([stack(u.astype(BF16)), sv], axis=0),
                                        preferred_element_type=F32)
            dn = _mm_tn(jnp.concatenate([kka * e_rem, k2 * e_rem], axis=0), jnp.concatenate([u, v], axis=0))
            gam_col = jnp.transpose(jnp.broadcast_to(jnp.exp(lc_last), (width, width)))
            n_ref[t] = gam_col * state + jnp.where(same_head, dn, 0.0)
            yield

            inv_n = 1.0 / RWKV_HEAD_DIM
            mean = _mm_sel_rhs(y, head_sum) * inv_n
            yield
            dlt = y - mean
            var = _mm_sel_rhs(dlt * dlt, head_sum) * inv_n
            yield
            y = dlt * lax.rsqrt(var + RWKV_LN_EPS) * lnw_ref[:, sl] + lnb_ref[:, sl]
            y = y + bonus * v
            o_ref[pl.ds(r0, c_len), sl] = (y * _silu(z_all[:, sl])).astype(o_ref.dtype)

        return [chain(t) for t in range(n_tile)]

    def finish():
        _save_halos(((r_ref, hr_ref), (k_ref, hk_ref), (v_ref, hv_ref), (wl_ref, hwl_ref), (al_ref, hal_ref)),
                    blk_rows)

    return chains, finish


def _rwkv_specs(proj, p, layer, blk, nsb):
    w = RWKV_WIDTH
    colspec = lambda base: pl.BlockSpec((blk, w), lambda b, g, s, base=base: (b * nsb + s, base))
    lspec = lambda col: pl.BlockSpec((blk, LANE), lambda b, g, s, col=col: (b * nsb + s, col))
    grow = pl.BlockSpec((None, 1, w), lambda b, g, s: (layer, 0, 0))
    lrow = pl.BlockSpec((None, 1, LANE), lambda b, g, s: (layer, 0, 0))
    upspec = pl.BlockSpec((None, LANE, w), lambda b, g, s: (layer, 0, 0))
    base = COL_RKV * LANE // RWKV_WIDTH
    in_specs = [colspec(base), colspec(base + 1), colspec(base + 2), lspec(COL_WL), lspec(COL_AL),
                colspec(COL_RZ * LANE // RWKV_WIDTH),
                grow, grow, grow, lrow, lrow,
                grow, upspec, grow, upspec, grow, grow, grow, grow, grow]
    scratch = ([pltpu.VMEM((w // MXU_TILE, MXU_TILE, MXU_TILE), F32)] + [pltpu.VMEM((HALO, w), F32)] * 3
               + [pltpu.VMEM((HALO, LANE), F32)] * 2)
    operands = (proj, proj, proj, proj, proj, proj,
                p["mu_r"], p["mu_k"], p["mu_v"], p["mu_wl"], p["mu_al"],
                p["w0"], p["w_up"], p["a0"], p["a_up"], p["k_k"], p["k_a"], p["r_k"], p["lnx_w"], p["lnx_b"])
    return operands, in_specs, scratch


def _mixers_kernel(*refs, n_gdn, n_rwkv, n_gdn_scr, blk_rows):
    gdn_in, rwkv_in = refs[:n_gdn], refs[n_gdn:n_gdn + n_rwkv]
    o_gdn, o_rwkv = refs[n_gdn + n_rwkv:n_gdn + n_rwkv + 2]
    scratch = refs[n_gdn + n_rwkv + 2:]
    g_chains, g_finish = _gdn_parts(*gdn_in, o_gdn, *scratch[:n_gdn_scr], blk_rows=blk_rows,
                                    n_tile=GDN_WIDTH // MXU_TILE)
    r_chains, r_finish = _rwkv_parts(*rwkv_in, o_rwkv, *scratch[n_gdn_scr:], blk_rows=blk_rows,
                                     n_tile=RWKV_WIDTH // MXU_TILE)

    def body(c, carry):
        _run_lockstep([ch for pair in zip(g_chains(c), r_chains(c)) for ch in pair])
        return carry

    lax.fori_loop(0, blk_rows // CHUNK, body, 0)
    g_finish()
    r_finish()


def _gdn_rwkv(proj, gdn_args, rwkv_p, layer, batch, seq):
    blk = min(SEQ_BLOCK, seq)
    nsb = seq // blk
    g_ops, g_specs, g_scr = _gdn_specs(proj, *gdn_args, layer, blk, nsb)
    r_ops, r_specs, r_scr = _rwkv_specs(proj, rwkv_p, layer, blk, nsb)
    out_spec = pl.BlockSpec((blk, GDN_WIDTH), lambda b, g, s: (b * nsb + s, 0))
    out_shape = jax.ShapeDtypeStruct((batch * seq, GDN_WIDTH), BF16)
    return pl.pallas_call(
        functools.partial(_mixers_kernel, n_gdn=len(g_ops), n_rwkv=len(r_ops), n_gdn_scr=len(g_scr),
                          blk_rows=blk),
        grid=(batch, 1, nsb),
        in_specs=g_specs + r_specs,
        out_specs=(out_spec, out_spec),
        out_shape=(out_shape, out_shape),
        scratch_shapes=g_scr + r_scr,
        compiler_params=pltpu.CompilerParams(
            dimension_semantics=("arbitrary", "arbitrary", "arbitrary"), vmem_limit_bytes=VMEM_LIMIT),
        name="gdn_rwkv",
    )(*g_ops, *r_ops)


def _s5_kernel(*refs, seq, n_lb):
    u_refs = refs[:n_lb]
    (lre_ref, lim_ref, ldt_ref, bre_ref, bim_ref, cre_ref, cim_ref, d_ref,
     o_ref, pre_ref, pim_ref, qre_ref, qim_ref, w_ref) = refs[n_lb:]
    tile = S5_TILE
    n_pair = LANE // (2 * S5_GROUP)
    n_chain = n_lb * n_pair
    sub = HALO
    n_sub = tile // sub

    def cmul(ar, ai, br, bi):
        return ar * br - ai * bi, ar * bi + ai * br

    def block_scan(sr, si, ar, ai, out):
        in_blk = _iota2(sr.shape, 0) % sub
        d = 1
        while d < sub:
            keep = in_blk >= d
            tr, ti = cmul(ar, ai, jnp.where(keep, pltpu.roll(sr, d, 0), 0.0),
                          jnp.where(keep, pltpu.roll(si, d, 0), 0.0))
            sr, si = sr + tr, si + ti
            ar, ai = cmul(ar, ai, ar, ai)
            d *= 2
            yield
        out.extend((sr, si))

    first = _iota2((sub, LANE), 0) == 0
    ab = []
    for q in range(n_chain):
        lb, p = divmod(q, n_pair)
        lre = lre_ref[lb, p:p + 1, :]
        lim = lim_ref[lb, p:p + 1, :]
        dt = jnp.exp(ldt_ref[lb, p:p + 1, :])
        mag = jnp.exp(lre * dt)
        ab_re = mag * jnp.cos(lim * dt)
        ab_im = mag * jnp.sin(lim * dt)
        den = lre * lre + lim * lim
        coef_re = ((ab_re - 1.0) * lre + ab_im * lim) / den
        coef_im = (ab_im * lre - (ab_re - 1.0) * lim) / den
        b_re = bre_ref[lb, p]
        b_im = bim_ref[lb, p]
        w_ref[q, 0] = (coef_re * b_re - coef_im * b_im).astype(BF16)
        w_ref[q, 1] = (coef_re * b_im + coef_im * b_re).astype(BF16)
        w_ref[q, 2] = cre_ref[lb, p].astype(BF16)
        w_ref[q, 3] = cim_ref[lb, p].astype(BF16)
        pr, pi = [ab_re], [ab_im]
        for _ in range(1, n_sub):
            nr, ni = cmul(pr[-1], pi[-1], ab_re, ab_im)
            pr.append(nr)
            pi.append(ni)
        pre_ref[q] = jnp.concatenate(pr, axis=0)
        pim_ref[q] = jnp.concatenate(pi, axis=0)
        imp_re = jnp.where(first, jnp.broadcast_to(pr[-1], (sub, LANE)), 0.0)
        imp_im = jnp.where(first, jnp.broadcast_to(pi[-1], (sub, LANE)), 0.0)
        pw = []
        for _ in block_scan(imp_re, imp_im, pr[-1], pi[-1], pw):
            pass
        qre_ref[q] = pw[0]
        qim_ref[q] = pw[1]
        ab.append((ab_re, ab_im, pr[-1], pi[-1]))

    def tile_body(i, st):
        t0 = pl.multiple_of(i * tile, tile)
        us = [jnp.concatenate([u_refs[lb][pl.ds(t0 + j, sub, stride=n_sub), :] for j in range(n_sub)], axis=0)
              for lb in range(n_lb)]
        ubs = [u.astype(BF16) for u in us]
        ys, new_st = [None] * n_chain, [None] * n_chain

        def chain(p):
            cr, ci = st[p]
            ar, ai, a_run_r, a_run_i = ab[p]
            ub = ubs[p // n_pair]
            sr = jnp.dot(ub, w_ref[p, 0], preferred_element_type=F32)
            si = jnp.dot(ub, w_ref[p, 1], preferred_element_type=F32)
            yield
            loc_r, loc_i = [sr[:sub, :]], [si[:sub, :]]
            for j in range(1, n_sub):
                tr, ti = cmul(ar, ai, loc_r[-1], loc_i[-1])
                loc_r.append(sr[j * sub:(j + 1) * sub, :] + tr)
                loc_i.append(si[j * sub:(j + 1) * sub, :] + ti)
                if j % 4 == 0:
                    yield
            res = []
            yield from block_scan(loc_r[-1], loc_i[-1], a_run_r, a_run_i, res)
            tr, ti = cmul(qre_ref[p], qim_ref[p], cr, ci)
            end_r, end_i = res[0] + tr, res[1] + ti
            new_st[p] = (end_r[sub - 1:sub, :], end_i[sub - 1:sub, :])
            in_r = jnp.where(first, cr, pltpu.roll(end_r, 1, 0))
            in_i = jnp.where(first, ci, pltpu.roll(end_i, 1, 0))
            out_r, out_i = [], []
            for j in range(n_sub):
                tr, ti = cmul(pre_ref[p, j:j + 1, :], pim_ref[p, j:j + 1, :], in_r, in_i)
                out_r.append(loc_r[j] + tr)
                out_i.append(loc_i[j] + ti)
            sr, si = jnp.concatenate(out_r, axis=0), jnp.concatenate(out_i, axis=0)
            ys[p] = jnp.dot(sr.astype(BF16), w_ref[p, 2], preferred_element_type=F32) - jnp.dot(
                si.astype(BF16), w_ref[p, 3], preferred_element_type=F32)

        _run_lockstep([chain(p) for p in range(n_chain)])
        for lb in range(n_lb):
            y = d_ref[lb] * us[lb]
            for p in range(n_pair):
                y = y + ys[lb * n_pair + p]
            inner = math.sqrt(2.0 / math.pi) * (y + 0.044715 * (y * y * y))
            y = 0.5 * y * (1.0 + jnp.tanh(inner))
            for j in range(n_sub):
                o_ref[lb, pl.ds(t0 + j, sub, stride=n_sub), :] = y[j * sub:(j + 1) * sub, :]
        return tuple(new_st)

    zero = jnp.zeros((1, LANE), F32)
    lax.fori_loop(0, seq // tile, tile_body, tuple((zero, zero) for _ in range(n_chain)))


def _s5(proj, p, layer, batch, seq, n_lb=2):
    n_blk = S5_WIDTH // (n_lb * LANE)
    n_pair = LANE // (2 * S5_GROUP)
    n_chain = n_lb * n_pair
    rowspec = pl.BlockSpec((None, n_lb, n_pair, LANE), lambda b, j: (layer, j, 0, 0))
    matspec = pl.BlockSpec((None, n_lb, n_pair, LANE, LANE), lambda b, j: (layer, j, 0, 0, 0))
    return pl.pallas_call(
        functools.partial(_s5_kernel, seq=seq, n_lb=n_lb),
        grid=(batch, n_blk),
        in_specs=[pl.BlockSpec((seq, LANE), lambda b, j, lb=lb: (b, COL_SU + n_lb * j + lb)) for lb in range(n_lb)]
                 + [rowspec, rowspec, rowspec, matspec, matspec, matspec, matspec,
                    pl.BlockSpec((None, n_lb, 1, LANE), lambda b, j: (layer, j, 0, 0))],
        out_specs=pl.BlockSpec((n_lb, seq, LANE), lambda b, j: (j, b, 0)),
        out_shape=jax.ShapeDtypeStruct((S5_WIDTH // LANE, batch * seq, LANE), F32),
        scratch_shapes=[pltpu.VMEM((n_chain, S5_TILE // HALO, LANE), F32)] * 2
                       + [pltpu.VMEM((n_chain, HALO, LANE), F32)] * 2
                       + [pltpu.VMEM((n_chain, 4, LANE, LANE), BF16)],
        compiler_params=pltpu.CompilerParams(
            dimension_semantics=("arbitrary", "arbitrary"), vmem_limit_bytes=VMEM_LIMIT),
        name="s5_scan",
    )(*([proj] * n_lb), p["lam_re"], p["lam_im"], p["log_dt"], p["b_re"], p["b_im"], p["c_re"], p["c_im"], p["d"])


def _s5_glu_kernel(y_ref, w_ref, b_ref, z_ref, o_ref):
    y = jnp.concatenate([y_ref[j] for j in range(S5_WIDTH // LANE)], axis=1)
    gate = _sigmoid(_mm(y, w_ref[...]) + b_ref[...])
    o_ref[...] = (y * gate * _silu(z_ref[...])).astype(o_ref.dtype)


def _s5_glu(yc, proj, glu_w, glu_b, layer, tm=512):
    t = yc.shape[1]
    tm = min(tm, t)
    return pl.pallas_call(
        _s5_glu_kernel,
        grid=(t // tm,),
        in_specs=[pl.BlockSpec((S5_WIDTH // LANE, tm, LANE), lambda i: (0, i, 0)),
                  pl.BlockSpec((None, S5_WIDTH, S5_WIDTH), lambda i: (layer, 0, 0)),
                  pl.BlockSpec((None, 1, S5_WIDTH), lambda i: (layer, 0, 0)),
                  pl.BlockSpec((tm, S5_WIDTH), lambda i: (i, COL_SZ * LANE // S5_WIDTH))],
        out_specs=pl.BlockSpec((tm, S5_WIDTH), lambda i: (i, 0)),
        out_shape=jax.ShapeDtypeStruct((t, S5_WIDTH), BF16),
        compiler_params=pltpu.CompilerParams(
            dimension_semantics=("arbitrary",), vmem_limit_bytes=VMEM_LIMIT),
        name="s5_glu",
    )(yc, glu_w, glu_b, proj)


def _merge_kernel(oa_ref, ob_ref, oc_ref, wb_ref, ga_ref, gb_ref, gc_ref, gbias_ref, o_ref):
    acc = None
    for i, (o_r, g_r) in enumerate(((oa_ref, ga_ref), (ob_ref, gb_ref), (oc_ref, gc_ref))):
        proj = jnp.dot(o_r[...], wb_ref[i], preferred_element_type=F32)
        term = _sigmoid(g_r[...] + gbias_ref[i]) * proj
        acc = term if acc is None else acc + term
    o_ref[...] = acc.astype(o_ref.dtype)


def _merge(oa, ob, oc, proj, w_branch, gate_b, layer, tm=1024, tn=512):
    t = oa.shape[0]
    tm = min(tm, t)
    gbase = COL_GATE * LANE // tn
    per = D_MODEL // tn
    ospec = pl.BlockSpec((tm, 1024), lambda i, j: (i, 0))
    gspec = lambda br: pl.BlockSpec((tm, tn), lambda i, j, br=br: (i, gbase + br * per + j))
    return pl.pallas_call(
        _merge_kernel,
        grid=(t // tm, per),
        in_specs=[ospec, ospec, ospec,
                  pl.BlockSpec((None, N_BRANCH, 1024, tn), lambda i, j: (layer, 0, 0, j)),
                  gspec(0), gspec(1), gspec(2),
                  pl.BlockSpec((None, N_BRANCH, 1, tn), lambda i, j: (layer, 0, 0, j))],
        out_specs=pl.BlockSpec((tm, tn), lambda i, j: (i, j)),
        out_shape=jax.ShapeDtypeStruct((t, D_MODEL), BF16),
        compiler_params=pltpu.CompilerParams(
            dimension_semantics=("arbitrary", "arbitrary"), vmem_limit_bytes=VMEM_LIMIT),
        name="merge",
    )(oa, ob, oc, w_branch, proj, proj, proj, gate_b)


def _outproj_kernel(m_ref, w_ref, x_ref, nw_ref, *o_refs):
    x = x_ref[...] + jnp.dot(m_ref[...], w_ref[...], preferred_element_type=F32)
    normed = _rms(x, nw_ref[...])
    if len(o_refs) == 2:
        o_refs[0][...] = x
    o_refs[-1][...] = normed.astype(o_refs[-1].dtype)


def _outproj(merged, w_out, x2, next_norm_rows, layer, last, tm=512):
    t = x2.shape[0]
    tm = min(tm, t)
    row_blk = pl.BlockSpec((tm, D_MODEL), lambda i: (i, 0))
    if last:
        out_specs, out_shape = row_blk, jax.ShapeDtypeStruct((t, D_MODEL), F32)
    else:
        out_specs = (row_blk, row_blk)
        out_shape = (jax.ShapeDtypeStruct((t, D_MODEL), F32), jax.ShapeDtypeStruct((t, D_MODEL), BF16))
    return pl.pallas_call(
        _outproj_kernel,
        grid=(t // tm,),
        in_specs=[row_blk,
                  pl.BlockSpec((None, D_MODEL, D_MODEL), lambda i: (layer, 0, 0)),
                  row_blk,
                  pl.BlockSpec((None, 1, D_MODEL), lambda i: (layer, 0, 0))],
        out_specs=out_specs,
        out_shape=out_shape,
        compiler_params=pltpu.CompilerParams(
            dimension_semantics=("arbitrary",), vmem_limit_bytes=VMEM_LIMIT),
        name="outproj",
    )(merged, w_out, x2, next_norm_rows)


def _pad_cols(a, width):
    return jnp.pad(a, [(0, 0)] * (a.ndim - 1) + [(0, width - a.shape[-1])])


_W_SRC_GBA = 4096
_W_SRC_RKV = _W_SRC_GBA + 2 * GDN_HEADS
_W_SRC_WL = _W_SRC_RKV + 3 * RWKV_WIDTH
_W_SRC_RZ = _W_SRC_WL + 2 * RWKV_LORA
_W_BLK = 512


def _relayout_kernel(main_ref, gba_ref, wl_ref, al_ref, o_ref):
    j = pl.program_id(0)
    tail = COL_GBA * LANE // _W_BLK

    @pl.when(j < tail)
    def _():
        o_ref[...] = jnp.transpose(main_ref[0]).astype(BF16)

    @pl.when(j == tail)
    def _():
        k = main_ref.shape[2]
        zeros = lambda n: jnp.zeros((n, k), F32)
        rows = jnp.concatenate([gba_ref[0], zeros(LANE - 2 * GDN_HEADS),
                                wl_ref[0], zeros(LANE - RWKV_LORA),
                                al_ref[0], zeros(LANE - RWKV_LORA), zeros(LANE)], axis=0)
        o_ref[...] = jnp.transpose(rows).astype(BF16)


def _relayout_w_in(w_t, layer):
    k = w_t.shape[2]
    first_rkv, first_rz, tail = COL_RKV * LANE // _W_BLK, COL_RZ * LANE // _W_BLK, COL_GBA * LANE // _W_BLK

    unit = 2 * GDN_HEADS

    def start(j):
        shift = jnp.where(j < first_rkv, 0, jnp.where(j < first_rz, (_W_SRC_RKV - COL_RKV * LANE) // unit,
                                                       (_W_SRC_RZ - COL_RZ * LANE) // unit))
        return jnp.where(j < tail, j * (_W_BLK // unit) + shift, 0) * unit

    rows_at = lambda n, row: pl.BlockSpec((pl.Element(1), pl.Element(n), pl.Element(k)),
                                          lambda j: (layer, row(j), 0))
    fixed = lambda n, row: rows_at(n, lambda j: row)
    return pl.pallas_call(
        _relayout_kernel,
        grid=(NP // _W_BLK,),
        in_specs=[rows_at(_W_BLK, start),
                  fixed(2 * GDN_HEADS, _W_SRC_GBA), fixed(RWKV_LORA, _W_SRC_WL),
                  fixed(RWKV_LORA, _W_SRC_WL + RWKV_LORA)],
        out_specs=pl.BlockSpec((k, _W_BLK), lambda j: (0, j)),
        out_shape=jax.ShapeDtypeStruct((k, NP), BF16),
        compiler_params=pltpu.CompilerParams(dimension_semantics=("arbitrary",), vmem_limit_bytes=VMEM_LIMIT),
        name="relayout_w",
    )(w_t, w_t, w_t, w_t)


def _place_s5(a, rows_are_channels):
    n_blk, n_pair = S5_WIDTH // LANE, LANE // (2 * S5_GROUP)
    grp_in_blk = LANE // S5_GROUP
    cs = a if not rows_are_channels else jnp.swapaxes(a, 1, 2)
    cs = cs.reshape(n_blk, n_pair, 2, S5_STATE, S5_GROUP)
    slot = jnp.arange(grp_in_blk)[None, :, None] == (2 * jnp.arange(n_pair)[:, None, None]
                                                      + jnp.arange(2)[None, None, :])
    placed = jnp.where(slot[None, :, :, None, :, None],
                       jnp.transpose(cs, (0, 1, 4, 2, 3))[:, :, None, :, :, :], 0.0)
    placed = placed.reshape(n_blk, n_pair, LANE, LANE)
    return placed if not rows_are_channels else jnp.swapaxes(placed, 2, 3)


def _s5_rows(a):
    return a.reshape(S5_WIDTH // LANE, LANE // (2 * S5_GROUP), 2 * S5_STATE)


def _rows(a):
    return a.reshape(a.shape[0], 1, -1)


def _gdn_head_rows(a):
    return _pad_cols(_rows(jnp.pad(a, ((0, 0), (GDN_HEADS, 0)))), LANE)


def _rwkv_params(mu, w0, w_up, a0, a_up, k_k, k_a, r_k, lnx_w, lnx_b):
    lora = lambda a: jnp.pad(a, ((0, 0), (0, LANE - RWKV_LORA), (0, 0))).astype(BF16)
    w3 = 3 * RWKV_WIDTH
    return dict(mu_r=_rows(mu[:, :RWKV_WIDTH]), mu_k=_rows(mu[:, RWKV_WIDTH:2 * RWKV_WIDTH]),
                mu_v=_rows(mu[:, 2 * RWKV_WIDTH:w3]),
                mu_wl=_pad_cols(_rows(mu[:, w3:w3 + RWKV_LORA]), LANE),
                mu_al=_pad_cols(_rows(mu[:, w3 + RWKV_LORA:]), LANE),
                w0=_rows(w0), w_up=lora(w_up), a0=_rows(a0), a_up=lora(a_up), k_k=_rows(k_k), k_a=_rows(k_a),
                r_k=_rows(r_k), lnx_w=_rows(lnx_w), lnx_b=_rows(lnx_b))


def _s5_params(a_re, a_im, log_dt, b_re, b_im, c_re, c_im, d):
    depth = a_re.shape[0]
    rows = jax.vmap(_s5_rows)
    place = lambda a, t: jax.vmap(lambda m: _place_s5(m, t))(a)
    return dict(lam_re=rows(a_re), lam_im=rows(a_im),
                log_dt=rows(jnp.broadcast_to(log_dt[:, :, None], (depth, S5_WIDTH // S5_GROUP, S5_STATE))),
                b_re=place(b_re, False), b_im=place(b_im, False),
                c_re=place(c_re, True), c_im=place(c_im, True),
                d=d.reshape(depth, S5_WIDTH // LANE, 1, LANE))


def kernel(x, norm_w, w_in, gdn_conv_w, gdn_a_log, gdn_dt_bias, gdn_norm_w, rwkv_mu, rwkv_w0, rwkv_w_up,
           rwkv_a0, rwkv_a_up, rwkv_k_k, rwkv_k_a, rwkv_r_k, rwkv_lnx_w, rwkv_lnx_b, s5_a_re, s5_a_im,
           s5_log_dt, s5_b_re, s5_b_im, s5_c_re, s5_c_im, s5_d, s5_glu_w, s5_glu_b, gate_b, w_branch,
           w_out, final_norm_w):
    batch, seq, _ = x.shape
    depth = w_in.shape[0]
    x2 = x.reshape(batch * seq, D_MODEL)
    h = _rmsnorm(x2, norm_w[0].reshape(1, D_MODEL))

    gdn_alog, gdn_dtb, gdn_nw = _gdn_head_rows(gdn_a_log), _gdn_head_rows(gdn_dt_bias), _rows(gdn_norm_w)
    rp = _rwkv_params(rwkv_mu, rwkv_w0, rwkv_w_up, rwkv_a0, rwkv_a_up, rwkv_k_k, rwkv_k_a, rwkv_r_k,
                      rwkv_lnx_w, rwkv_lnx_b)
    sp = _s5_params(s5_a_re, s5_a_im, s5_log_dt, s5_b_re, s5_b_im, s5_c_re, s5_c_im, s5_d)
    glu_w, glu_b = s5_glu_w.astype(BF16), _rows(s5_glu_b)
    wb, gb = w_branch.astype(BF16), gate_b.reshape(depth, N_BRANCH, 1, D_MODEL)
    wo = w_out.astype(BF16)
    next_norm = _rows(jnp.concatenate([norm_w[1:], final_norm_w[None, :]], axis=0))
    w_in_t = jnp.swapaxes(w_in, 1, 2)

    for i in range(depth):
        proj = _inproj(h, _relayout_w_in(w_in_t, i))
        o_a, o_b = _gdn_rwkv(proj, (gdn_conv_w, gdn_alog, gdn_dtb, gdn_nw), rp, i, batch, seq)
        o_c = _s5_glu(_s5(proj, sp, i, batch, seq), proj, glu_w, glu_b, i)
        merged = _merge(o_a, o_b, o_c, proj, wb, gb, i)
        if i == depth - 1:
            return _outproj(merged, wo, x2, next_norm, i, True).reshape(batch, seq, D_MODEL)
        x2, h = _outproj(merged, wo, x2, next_norm, i, False)
```

```python
import functools
import math

import jax
import jax.numpy as jnp
from jax import lax
from jax.experimental import pallas as pl
from jax.experimental.pallas import tpu as pltpu

F32 = jnp.float32
BF16 = jnp.bfloat16

D_MODEL = 2048
GDN_HEADS = 8
GDN_HEAD_DIM = 128
GDN_WIDTH = 1024
RWKV_HEAD_DIM = 64
RWKV_WIDTH = 1024
RWKV_LORA = 96
RWKV_LN_EPS = 64e-5
S5_GROUP = 16
S5_STATE = 64
S5_WIDTH = 1024
N_BRANCH = 3
NORM_EPS = 1e-6

LANE = 128
MXU_TILE = 256
HALO = 8
CHUNK = 64
SEQ_BLOCK = 512
S5_TILE = 256
VMEM_LIMIT = 52 * 1024 * 1024

COL_QKV = 0
COL_GZ = 24
COL_RKV = 32
COL_RZ = 56
COL_SU = 64
COL_SZ = 72
COL_GATE = 80
COL_GBA = 128
COL_WL = 129
COL_AL = 130
N_COLBLK = 132
NP = N_COLBLK * LANE


def _sigmoid(x):
    return 1.0 / (1.0 + jnp.exp(-x))


def _silu(x):
    return x * _sigmoid(x)


def _softplus(x):
    return jnp.maximum(x, 0.0) + jnp.log1p(jnp.exp(-jnp.abs(x)))


def _mm(a, b):
    return jnp.dot(a.astype(BF16), b.astype(BF16), preferred_element_type=F32)


def _mm_nt(a, b):
    return lax.dot_general(a.astype(BF16), b.astype(BF16), (((1,), (1,)), ((), ())),
                           preferred_element_type=F32)


def _mm_tn(a, b):
    return lax.dot_general(a.astype(BF16), b.astype(BF16), (((0,), (0,)), ((), ())),
                           preferred_element_type=F32)


def _split_bf16(x, parts):
    out = []
    for _ in range(parts - 1):
        hi = x.astype(BF16)
        out.append(hi)
        x = x - hi.astype(F32)
    out.append(x.astype(BF16))
    return out


def _mm_sel_lhs(sel, x, parts=2):
    cols = x.shape[1]
    res = jnp.dot(sel.astype(BF16), jnp.concatenate(_split_bf16(x, parts), axis=1), preferred_element_type=F32)
    out = res[:, :cols]
    for i in range(1, parts):
        out = out + res[:, i * cols:(i + 1) * cols]
    return out


def _mm_sel_rhs(x, sel, parts=2):
    rows = x.shape[0]
    res = jnp.dot(jnp.concatenate(_split_bf16(x, parts), axis=0), sel.astype(BF16), preferred_element_type=F32)
    out = res[:rows, :]
    for i in range(1, parts):
        out = out + res[i * rows:(i + 1) * rows, :]
    return out


def _unit_lower_inverse(x, eye, size, stack, out):
    rows = x.shape[0]
    y = x
    q = eye + x
    y = jnp.dot(x.astype(BF16), stack(x.astype(BF16)), preferred_element_type=F32)
    yield
    span = 4
    while span < size:
        res = jnp.dot(jnp.concatenate([y, q], axis=0).astype(BF16), stack(y.astype(BF16)),
                      preferred_element_type=F32)
        yield
        y, q = res[:rows, :], q + res[rows:, :]
        span *= 2
    q = q + jnp.dot(q.astype(BF16), stack(y.astype(BF16)), preferred_element_type=F32)
    yield
    out.append(q)


def _run_lockstep(chains):
    live = list(chains)
    while live:
        for g in list(live):
            try:
                next(g)
            except StopIteration:
                live.remove(g)


def _iota2(shape, dim):
    return lax.broadcasted_iota(jnp.int32, shape, dim)


def _chunk_and_shifts(ref, halo_ref, c, rows, shift):
    r0 = pl.multiple_of(c * rows, rows)
    h0 = pl.multiple_of(jnp.maximum(r0 - HALO, 0), HALO)
    halo = jnp.where(c == 0, halo_ref[...], ref[pl.ds(h0, HALO), :])
    cur = ref[pl.ds(r0, rows), :]
    xc = jnp.concatenate([halo, cur], axis=0)
    return cur, [pltpu.roll(xc, s, 0)[HALO:, :] for s in range(1, shift + 1)]


def _recurrence_grid_setup(state_ref, halo_refs):
    @pl.when(pl.program_id(2) == 0)
    def _():
        state_ref[...] = jnp.zeros_like(state_ref)
        for h in halo_refs:
            h[...] = jnp.zeros_like(h)


def _save_halos(pairs, rows):
    for src, dst in pairs:
        dst[...] = src[rows - HALO:rows, :]


def _rms(x, w_row):
    return x * lax.rsqrt(jnp.mean(x * x, axis=-1, keepdims=True) + NORM_EPS) * w_row


def _rmsnorm_kernel(x_ref, nw_ref, o_ref):
    o_ref[...] = _rms(x_ref[...], nw_ref[...]).astype(o_ref.dtype)


def _rmsnorm(x2, norm_w_row, tm=512):
    t = x2.shape[0]
    tm = min(tm, t)
    return pl.pallas_call(
        _rmsnorm_kernel,
        grid=(t // tm,),
        in_specs=[pl.BlockSpec((tm, D_MODEL), lambda i: (i, 0)),
                  pl.BlockSpec((1, D_MODEL), lambda i: (0, 0))],
        out_specs=pl.BlockSpec((tm, D_MODEL), lambda i: (i, 0)),
        out_shape=jax.ShapeDtypeStruct((t, D_MODEL), BF16),
        compiler_params=pltpu.CompilerParams(dimension_semantics=("arbitrary",), vmem_limit_bytes=VMEM_LIMIT),
        name="rmsnorm",
    )(x2, norm_w_row)


def _inproj_kernel(h_ref, w_ref, o_ref):
    o_ref[...] = jnp.dot(h_ref[...], w_ref[...], preferred_element_type=F32)


def _inproj(h, w_pad, tm=2048, tn=768):
    t = h.shape[0]
    tm = min(tm, t)
    return pl.pallas_call(
        _inproj_kernel,
        grid=(t // tm, NP // tn),
        in_specs=[pl.BlockSpec((tm, D_MODEL), lambda i, j: (i, 0)),
                  pl.BlockSpec((D_MODEL, tn), lambda i, j: (0, j))],
        out_specs=pl.BlockSpec((tm, tn), lambda i, j: (i, j)),
        out_shape=jax.ShapeDtypeStruct((t, NP), F32),
        compiler_params=pltpu.CompilerParams(
            dimension_semantics=("arbitrary", "arbitrary"), vmem_limit_bytes=VMEM_LIMIT),
        name="inproj",
    )(h, w_pad)


def _gdn_parts(q_ref, k_ref, v_ref, z_ref, ba_ref, cq_ref, ck_ref, cv_ref,
               alog_ref, dtb_ref, nw_ref, o_ref, s_ref, hq_ref, hk_ref, hv_ref, *, blk_rows, n_tile):
    c_len = CHUNK
    rows = 2 * c_len
    width = MXU_TILE
    tile0 = pl.program_id(1) * n_tile

    lane_w = _iota2((1, width), 1)
    lane_b = _iota2((1, LANE), 1)
    t_w = _iota2((c_len, rows), 0)
    s_w = _iota2((c_len, rows), 1) % c_len
    strict_w = s_w < t_w
    incl_w = s_w <= t_w
    eye_w = jnp.where(s_w == t_w, 1.0, 0.0)
    triu_w = jnp.where(t_w <= s_w, 1.0, 0.0)
    ti = _iota2((c_len, c_len), 0)
    tj = _iota2((c_len, c_len), 1)
    tril_t = jnp.where(tj <= ti, 1.0, 0.0)
    ones_t = jnp.ones((c_len, c_len), F32)
    same_head = (_iota2((width, width), 0) // GDN_HEAD_DIM) == (_iota2((width, width), 1) // GDN_HEAD_DIM)
    head0_lane = lane_w < GDN_HEAD_DIM
    head0_wide = _iota2((1, rows), 1) < c_len
    zero_b = jnp.zeros((), BF16)

    _recurrence_grid_setup(s_ref, (hq_ref, hk_ref, hv_ref))
    alog_row = alog_ref[...]
    dtb_row = dtb_ref[...]
    nw_row = nw_ref[...]

    def stack(xb):
        return jnp.concatenate([jnp.where(head0_lane, xb, zero_b), jnp.where(head0_lane, zero_b, xb)], axis=0)

    def stack_w(xb):
        return jnp.concatenate([jnp.where(head0_wide, xb, zero_b), jnp.where(head0_wide, zero_b, xb)], axis=0)

    def per_head(x, fn):
        return jnp.concatenate([fn(x[:, :GDN_HEAD_DIM]), fn(x[:, GDN_HEAD_DIM:])], axis=1)

    def l2n(xh):
        return xh * lax.rsqrt(jnp.sum(xh * xh, axis=-1, keepdims=True) + 1e-6)

    def rms(oh):
        return oh * lax.rsqrt(jnp.mean(oh * oh, axis=-1, keepdims=True) + NORM_EPS) * nw_row

    def chains(c):
        r0 = pl.multiple_of(c * c_len, c_len)

        def conv(x_ref, h_ref, cw_ref):
            cur, sh = _chunk_and_shifts(x_ref, h_ref, c, c_len, 3)
            cw = cw_ref[...]
            acc = cur * cw[3:4, :]
            for s in range(1, 4):
                acc = acc + sh[s - 1] * cw[3 - s:4 - s, :]
            return _silu(acc)

        q_all = conv(q_ref, hq_ref, cq_ref)
        k_all = conv(k_ref, hk_ref, ck_ref)
        v_all = conv(v_ref, hv_ref, cv_ref)
        z_all = z_ref[pl.ds(r0, c_len), :]
        ba = ba_ref[pl.ds(r0, c_len), :]
        g_all = -jnp.exp(alog_row) * _softplus(ba + dtb_row)

        def chain(t):
            sl = slice(t * width, (t + 1) * width)
            q = per_head(q_all[:, sl], l2n) * (GDN_HEAD_DIM ** -0.5)
            k = per_head(k_all[:, sl], l2n)
            v = v_all[:, sl]
            betas, gs = [], []
            for h in range(2):
                hid = 2 * (tile0 + t) + h
                betas.append(_sigmoid(jnp.sum(jnp.where(lane_b == hid, ba, 0.0), axis=-1, keepdims=True)))
                gs.append(jnp.sum(jnp.where(lane_b == GDN_HEADS + hid, g_all, 0.0), axis=-1, keepdims=True))
            beta = jnp.where(head0_lane, betas[0], betas[1])
            g_n = jnp.where(head0_lane, gs[0], gs[1])
            g_w = jnp.where(head0_wide, gs[0], gs[1])

            gcol = _mm_sel_lhs(tril_t, g_n)
            grow = _mm_sel_lhs(ones_t, g_w * triu_w)
            yield
            gcol_w = jnp.where(head0_wide, gcol[:, :rows], gcol[:, GDN_HEAD_DIM:GDN_HEAD_DIM + rows])
            diff = gcol_w - grow
            d_strict = jnp.where(strict_w, jnp.exp(jnp.where(strict_w, diff, 0.0)), 0.0)
            d_incl = jnp.where(incl_w, jnp.exp(jnp.where(incl_w, diff, 0.0)), 0.0)
            egc = jnp.exp(gcol)
            gl_lane = gcol[c_len - 1:c_len, :]

            kb = k * beta
            prod = lax.dot_general(jnp.concatenate([kb, q], axis=0).astype(BF16), stack(k.astype(BF16)),
                                   (((1,), (1,)), ((), ())), preferred_element_type=F32)
            yield
            lower = prod[:c_len, :] * d_strict
            attn = prod[c_len:, :] * d_incl
            inv = []
            yield from _unit_lower_inverse(-lower, eye_w, c_len, stack_w, inv)
            rhs = jnp.concatenate([stack((v * beta).astype(BF16)), stack((kb * egc).astype(BF16))], axis=1)
            uw = jnp.dot(inv[0].astype(BF16), rhs, preferred_element_type=F32)
            yield
            u, w = uw[:, :width], uw[:, width:]

            state = s_ref[t]
            x0 = _mm(jnp.concatenate([w, q * egc], axis=0), state)
            yield
            v_new = u - x0[:c_len, :]
            o = x0[c_len:, :] + jnp.dot(attn.astype(BF16), stack(v_new.astype(BF16)), preferred_element_type=F32)
            ds = _mm_tn(k * jnp.exp(gl_lane - gcol), v_new)
            s_ref[t] = state * jnp.exp(gl_lane) + jnp.where(same_head, ds, 0.0)
            yield

            o_ref[pl.ds(r0, c_len), sl] = (per_head(o, rms) * _silu(z_all[:, sl])).astype(o_ref.dtype)

        return [chain(t) for t in range(n_tile)]

    def finish():
        _save_halos(((q_ref, hq_ref), (k_ref, hk_ref), (v_ref, hv_ref)), blk_rows)

    return chains, finish


def _gdn_specs(proj, conv_w, alog_row, dtb_row, nw_row, layer, blk, nsb):
    w = GDN_WIDTH
    colspec = lambda base: pl.BlockSpec((blk, w), lambda b, g, s, base=base: (b * nsb + s, base))
    cwspec = lambda base: pl.BlockSpec((None, 4, w), lambda b, g, s, base=base: (layer, 0, base))
    rowspec = pl.BlockSpec((None, 1, LANE), lambda b, g, s: (layer, 0, 0))
    in_specs = [colspec(0), colspec(1), colspec(2), colspec(COL_GZ * LANE // GDN_WIDTH),
                pl.BlockSpec((blk, LANE), lambda b, g, s: (b * nsb + s, COL_GBA)),
                cwspec(0), cwspec(1), cwspec(2), rowspec, rowspec, rowspec]
    scratch = [pltpu.VMEM((w // MXU_TILE, MXU_TILE, MXU_TILE), F32)] + [pltpu.VMEM((HALO, w), F32)] * 3
    return (proj, proj, proj, proj, proj, conv_w, conv_w, conv_w, alog_row, dtb_row, nw_row), in_specs, scratch


def _rwkv_parts(r_ref, k_ref, v_ref, wl_ref, al_ref, z_ref,
                mur_ref, muk_ref, muv_ref, muwl_ref, mual_ref,
                w0_ref, wup_ref, a0_ref, aup_ref, kk_ref, ka_ref, rk_ref, lnw_ref, lnb_ref,
                o_ref, n_ref, hr_ref, hk_ref, hv_ref, hwl_ref, hal_ref, *, blk_rows, n_tile):
    c_len = CHUNK
    nh = MXU_TILE // RWKV_HEAD_DIM
    width = MXU_TILE
    assert c_len == RWKV_HEAD_DIM

    same_head = (_iota2((width, width), 0) // RWKV_HEAD_DIM) == (_iota2((width, width), 1) // RWKV_HEAD_DIM)
    head_sum = jnp.where(same_head, 1.0, 0.0)
    t_w = _iota2((c_len, width), 0)
    s_w = _iota2((c_len, width), 1) % c_len
    strict_w = s_w < t_w
    incl_w = s_w <= t_w
    eye_w = jnp.where(s_w == t_w, 1.0, 0.0)
    ti = _iota2((c_len, c_len), 0)
    tj = _iota2((c_len, c_len), 1)
    tril_t = jnp.where(tj <= ti, 1.0, 0.0)
    lane_head = _iota2((1, width), 1) // RWKV_HEAD_DIM
    zero_b = jnp.zeros((), BF16)

    _recurrence_grid_setup(n_ref, (hr_ref, hk_ref, hv_ref, hwl_ref, hal_ref))

    def stack(xb):
        return jnp.concatenate([jnp.where(lane_head == h, xb, zero_b) for h in range(nh)], axis=0)

    def chains(c):
        r0 = pl.multiple_of(c * c_len, c_len)

        def shifted(x_ref, h_ref, mu_ref):
            cur, sh = _chunk_and_shifts(x_ref, h_ref, c, c_len, 1)
            return cur + (sh[0] - cur) * mu_ref[...]

        r_all = shifted(r_ref, hr_ref, mur_ref)
        k_all = shifted(k_ref, hk_ref, muk_ref)
        v_all = shifted(v_ref, hv_ref, muv_ref)
        wl = shifted(wl_ref, hwl_ref, muwl_ref)
        al = shifted(al_ref, hal_ref, mual_ref)
        z_all = z_ref[pl.ds(r0, c_len), :]

        w_pre = w0_ref[...] + _mm(jnp.tanh(wl), wup_ref[...])
        logw_all = -jnp.exp(-_softplus(-w_pre) - 0.5)
        a_all = _sigmoid(a0_ref[...] + _mm(al, aup_ref[...]))
        kkx_all = k_all * kk_ref[...]
        k2_all = k_all * (1.0 + (a_all - 1.0) * ka_ref[...])
        lc_all = _mm_sel_lhs(tril_t, logw_all)
        bonus_all = r_all * k2_all * rk_ref[...]

        def chain(t):
            sl = slice(t * width, (t + 1) * width)
            r, v, a, k2, logw, lc = r_all[:, sl], v_all[:, sl], a_all[:, sl], k2_all[:, sl], logw_all[:, sl], lc_all[:, sl]
            kkx = kkx_all[:, sl]
            sums = _mm_sel_rhs(jnp.concatenate([kkx * kkx, bonus_all[:, sl]], axis=0), head_sum)
            yield
            kk = kkx * lax.rsqrt(sums[:c_len, :] + 1e-6)
            bonus = sums[c_len:, :]

            lc_last = lc[c_len - 1:c_len, :]
            e_neg = jnp.exp(-lc)
            e_rem = jnp.exp(lc_last - lc)
            kka = kk * a
            a_n = -kk * jnp.exp(lc - logw)
            r_n = r * jnp.exp(lc)
            sv = stack(v.astype(BF16))

            prod = lax.dot_general(
                jnp.concatenate([a_n, r_n], axis=0).astype(BF16),
                jnp.concatenate([stack((kka * e_neg).astype(BF16)), stack((k2 * e_neg).astype(BF16))], axis=0),
                (((1,), (1,)), ((), ())), preferred_element_type=F32)
            yield
            a_ab = jnp.where(strict_w, prod[:c_len, :width], 0.0)
            a_ak = jnp.where(strict_w, prod[:c_len, width:], 0.0)
            a_rb = jnp.where(incl_w, prod[c_len:, :width], 0.0)
            a_rk = jnp.where(incl_w, prod[c_len:, width:], 0.0)

            inv = []
            yield from _unit_lower_inverse(a_ab, eye_w, c_len, stack, inv)
            t_mat = inv[0].astype(BF16)
            w_p = jnp.dot(t_mat, stack(a_n.astype(BF16)), preferred_element_type=F32)
            aks = jnp.dot(a_ak.astype(BF16), sv, preferred_element_type=F32)
            yield
            v_p = jnp.dot(t_mat, stack(aks.astype(BF16)), preferred_element_type=F32)
            yield

            state = n_ref[t]
            x0 = _mm(jnp.concatenate([w_p, r_n], axis=0), state)
            yield
            u = x0[:c_len, :] + v_p
            y = x0[c_len:, :] + jnp.dot(jnp.concatenate([a_rb, a_rk], axis=1).astype(BF16),
                                        jnp.concatenate([stack(u.astype(BF16)), sv], axis=0),
                                        preferred_element_type=F32)
            dn = _mm_tn(jnp.concatenate([kka * e_rem, k2 * e_rem], axis=0), jnp.concatenate([u, v], axis=0))
            gam_col = jnp.transpose(jnp.broadcast_to(jnp.exp(lc_last), (width, width)))
            n_ref[t] = gam_col * state + jnp.where(same_head, dn, 0.0)
            yield

            inv_n = 1.0 / RWKV_HEAD_DIM
            mean = _mm_sel_rhs(y, head_sum) * inv_n
            yield
            dlt = y - mean
            var = _mm_sel_rhs(dlt * dlt, head_sum) * inv_n
            yield
            y = dlt * lax.rsqrt(var + RWKV_LN_EPS) * lnw_ref[:, sl] + lnb_ref[:, sl]
            y = y + bonus * v
            o_ref[pl.ds(r0, c_len), sl] = (y * _silu(z_all[:, sl])).astype(o_ref.dtype)

        return [chain(t) for t in range(n_tile)]

    def finish():
        _save_halos(((r_ref, hr_ref), (k_ref, hk_ref), (v_ref, hv_ref), (wl_ref, hwl_ref), (al_ref, hal_ref)),
                    blk_rows)

    return chains, finish


def _rwkv_specs(proj, p, layer, blk, nsb):
    w = RWKV_WIDTH
    colspec = lambda base: pl.BlockSpec((blk, w), lambda b, g, s, base=base: (b * nsb + s, base))
    lspec = lambda col: pl.BlockSpec((blk, LANE), lambda b, g, s, col=col: (b * nsb + s, col))
    grow = pl.BlockSpec((None, 1, w), lambda b, g, s: (layer, 0, 0))
    lrow = pl.BlockSpec((None, 1, LANE), lambda b, g, s: (layer, 0, 0))
    upspec = pl.BlockSpec((None, LANE, w), lambda b, g, s: (layer, 0, 0))
    base = COL_RKV * LANE // RWKV_WIDTH
    in_specs = [colspec(base), colspec(base + 1), colspec(base + 2), lspec(COL_WL), lspec(COL_AL),
                colspec(COL_RZ * LANE // RWKV_WIDTH),
                grow, grow, grow, lrow, lrow,
                grow, upspec, grow, upspec, grow, grow, grow, grow, grow]
    scratch = ([pltpu.VMEM((w // MXU_TILE, MXU_TILE, MXU_TILE), F32)] + [pltpu.VMEM((HALO, w), F32)] * 3
               + [pltpu.VMEM((HALO, LANE), F32)] * 2)
    operands = (proj, proj, proj, proj, proj, proj,
                p["mu_r"], p["mu_k"], p["mu_v"], p["mu_wl"], p["mu_al"],
                p["w0"], p["w_up"], p["a0"], p["a_up"], p["k_k"], p["k_a"], p["r_k"], p["lnx_w"], p["lnx_b"])
    return operands, in_specs, scratch


def _mixers_kernel(*refs, n_gdn, n_rwkv, n_gdn_scr, blk_rows):
    gdn_in, rwkv_in = refs[:n_gdn], refs[n_gdn:n_gdn + n_rwkv]
    o_gdn, o_rwkv = refs[n_gdn + n_rwkv:n_gdn + n_rwkv + 2]
    scratch = refs[n_gdn + n_rwkv + 2:]
    g_chains, g_finish = _gdn_parts(*gdn_in, o_gdn, *scratch[:n_gdn_scr], blk_rows=blk_rows,
                                    n_tile=GDN_WIDTH // MXU_TILE)
    r_chains, r_finish = _rwkv_parts(*rwkv_in, o_rwkv, *scratch[n_gdn_scr:], blk_rows=blk_rows,
                                     n_tile=RWKV_WIDTH // MXU_TILE)

    def body(c, carry):
        _run_lockstep([ch for pair in zip(g_chains(c), r_chains(c)) for ch in pair])
        return carry

    lax.fori_loop(0, blk_rows // CHUNK, body, 0)
    g_finish()
    r_finish()


def _gdn_rwkv(proj, gdn_args, rwkv_p, layer, batch, seq):
    blk = min(SEQ_BLOCK, seq)
    nsb = seq // blk
    g_ops, g_specs, g_scr = _gdn_specs(proj, *gdn_args, layer, blk, nsb)
    r_ops, r_specs, r_scr = _rwkv_specs(proj, rwkv_p, layer, blk, nsb)
    out_spec = pl.BlockSpec((blk, GDN_WIDTH), lambda b, g, s: (b * nsb + s, 0))
    out_shape = jax.ShapeDtypeStruct((batch * seq, GDN_WIDTH), BF16)
    return pl.pallas_call(
        functools.partial(_mixers_kernel, n_gdn=len(g_ops), n_rwkv=len(r_ops), n_gdn_scr=len(g_scr),
                          blk_rows=blk),
        grid=(batch, 1, nsb),
        in_specs=g_specs + r_specs,
        out_specs=(out_spec, out_spec),
        out_shape=(out_shape, out_shape),
        scratch_shapes=g_scr + r_scr,
        compiler_params=pltpu.CompilerParams(
            dimension_semantics=("arbitrary", "arbitrary", "arbitrary"), vmem_limit_bytes=VMEM_LIMIT),
        name="gdn_rwkv",
    )(*g_ops, *r_ops)


def _s5_kernel(*refs, seq, n_lb):
    u_refs = refs[:n_lb]
    (lre_ref, lim_ref, ldt_ref, bre_ref, bim_ref, cre_ref, cim_ref, d_ref,
     o_ref, pre_ref, pim_ref, qre_ref, qim_ref, w_ref) = refs[n_lb:]
    tile = S5_TILE
    n_pair = LANE // (2 * S5_GROUP)
    n_chain = n_lb * n_pair
    sub = HALO
    n_sub = tile // sub

    def cmul(ar, ai, br, bi):
        return ar * br - ai * bi, ar * bi + ai * br

    def block_scan(sr, si, ar, ai, out):
        in_blk = _iota2(sr.shape, 0) % sub
        d = 1
        while d < sub:
            keep = in_blk >= d
            tr, ti = cmul(ar, ai, jnp.where(keep, pltpu.roll(sr, d, 0), 0.0),
                          jnp.where(keep, pltpu.roll(si, d, 0), 0.0))
            sr, si = sr + tr, si + ti
            ar, ai = cmul(ar, ai, ar, ai)
            d *= 2
            yield
        out.extend((sr, si))

    first = _iota2((sub, LANE), 0) == 0
    ab = []
    for q in range(n_chain):
        lb, p = divmod(q, n_pair)
        lre = lre_ref[lb, p:p + 1, :]
        lim = lim_ref[lb, p:p + 1, :]
        dt = jnp.exp(ldt_ref[lb, p:p + 1, :])
        mag = jnp.exp(lre * dt)
        ab_re = mag * jnp.cos(lim * dt)
        ab_im = mag * jnp.sin(lim * dt)
        den = lre * lre + lim * lim
        coef_re = ((ab_re - 1.0) * lre + ab_im * lim) / den
        coef_im = (ab_im * lre - (ab_re - 1.0) * lim) / den
        b_re = bre_ref[lb, p]
        b_im = bim_ref[lb, p]

        def place(m, p=p):
            top, n = p * 2 * S5_GROUP, m.shape[0]
            parts = [jnp.zeros((top, LANE), BF16)] if top else []
            parts.append(m.astype(BF16))
            if LANE - top - n:
                parts.append(jnp.zeros((LANE - top - n, LANE), BF16))
            return jnp.concatenate(parts, axis=0)

        w_ref[q, 0] = place(coef_re * b_re - coef_im * b_im)
        w_ref[q, 1] = place(coef_re * b_im + coef_im * b_re)
        w_ref[q, 2] = place(cre_ref[lb, p])
        w_ref[q, 3] = place(cim_ref[lb, p])
        pr, pi = [ab_re], [ab_im]
        for _ in range(1, n_sub):
            nr, ni = cmul(pr[-1], pi[-1], ab_re, ab_im)
            pr.append(nr)
            pi.append(ni)
        pre_ref[q] = jnp.concatenate(pr, axis=0)
        pim_ref[q] = jnp.concatenate(pi, axis=0)
        imp_re = jnp.where(first, jnp.broadcast_to(pr[-1], (sub, LANE)), 0.0)
        imp_im = jnp.where(first, jnp.broadcast_to(pi[-1], (sub, LANE)), 0.0)
        pw = []
        for _ in block_scan(imp_re, imp_im, pr[-1], pi[-1], pw):
            pass
        qre_ref[q] = pw[0]
        qim_ref[q] = pw[1]
        ab.append((ab_re, ab_im, pr[-1], pi[-1]))

    def tile_body(i, st):
        t0 = pl.multiple_of(i * tile, tile)
        us = [jnp.concatenate([u_refs[lb][pl.ds(t0 + j, sub, stride=n_sub), :] for j in range(n_sub)], axis=0)
              for lb in range(n_lb)]
        ubs = [u.astype(BF16) for u in us]
        ys, new_st = [None] * n_chain, [None] * n_chain

        def chain(p):
            cr, ci = st[p]
            ar, ai, a_run_r, a_run_i = ab[p]
            ub = ubs[p // n_pair]
            sr = jnp.dot(ub, w_ref[p, 0], preferred_element_type=F32)
            si = jnp.dot(ub, w_ref[p, 1], preferred_element_type=F32)
            yield
            loc_r, loc_i = [sr[:sub, :]], [si[:sub, :]]
            for j in range(1, n_sub):
                tr, ti = cmul(ar, ai, loc_r[-1], loc_i[-1])
                loc_r.append(sr[j * sub:(j + 1) * sub, :] + tr)
                loc_i.append(si[j * sub:(j + 1) * sub, :] + ti)
                if j % 4 == 0:
                    yield
            res = []
            yield from block_scan(loc_r[-1], loc_i[-1], a_run_r, a_run_i, res)
            tr, ti = cmul(qre_ref[p], qim_ref[p], cr, ci)
            end_r, end_i = res[0] + tr, res[1] + ti
            new_st[p] = (end_r[sub - 1:sub, :], end_i[sub - 1:sub, :])
            in_r = jnp.where(first, cr, pltpu.roll(end_r, 1, 0))
            in_i = jnp.where(first, ci, pltpu.roll(end_i, 1, 0))
            out_r, out_i = [], []
            for j in range(n_sub):
                tr, ti = cmul(pre_ref[p, j:j + 1, :], pim_ref[p, j:j + 1, :], in_r, in_i)
                out_r.append(loc_r[j] + tr)
                out_i.append(loc_i[j] + ti)
            sr, si = jnp.concatenate(out_r, axis=0), jnp.concatenate(out_i, axis=0)
            nt = (((1,), (1,)), ((), ()))
            ys[p] = (lax.dot_general(sr.astype(BF16), w_ref[p, 2], nt, preferred_element_type=F32)
                     - lax.dot_general(si.astype(BF16), w_ref[p, 3], nt, preferred_element_type=F32))

        _run_lockstep([chain(p) for p in range(n_chain)])
        for lb in range(n_lb):
            y = d_ref[lb] * us[lb]
            for p in range(n_pair):
                y = y + ys[lb * n_pair + p]
            inner = math.sqrt(2.0 / math.pi) * (y + 0.044715 * (y * y * y))
            y = 0.5 * y * (1.0 + jnp.tanh(inner))
            for j in range(n_sub):
                o_ref[lb, pl.ds(t0 + j, sub, stride=n_sub), :] = y[j * sub:(j + 1) * sub, :]
        return tuple(new_st)

    zero = jnp.zeros((1, LANE), F32)
    lax.fori_loop(0, seq // tile, tile_body, tuple((zero, zero) for _ in range(n_chain)))


def _s5(proj, p, layer, batch, seq, n_lb=4):
    n_blk = S5_WIDTH // (n_lb * LANE)
    n_pair = LANE // (2 * S5_GROUP)
    n_chain = n_lb * n_pair
    rowspec = pl.BlockSpec((None, n_lb, n_pair, LANE), lambda b, j: (layer, j, 0, 0))
    matspec = pl.BlockSpec((None, n_lb, n_pair, 2 * S5_GROUP, LANE), lambda b, j: (layer, j, 0, 0, 0))
    return pl.pallas_call(
        functools.partial(_s5_kernel, seq=seq, n_lb=n_lb),
        grid=(batch, n_blk),
        in_specs=[pl.BlockSpec((seq, LANE), lambda b, j, lb=lb: (b, COL_SU + n_lb * j + lb)) for lb in range(n_lb)]
                 + [rowspec, rowspec, rowspec, matspec, matspec, matspec, matspec,
                    pl.BlockSpec((None, n_lb, 1, LANE), lambda b, j: (layer, j, 0, 0))],
        out_specs=pl.BlockSpec((n_lb, seq, LANE), lambda b, j: (j, b, 0)),
        out_shape=jax.ShapeDtypeStruct((S5_WIDTH // LANE, batch * seq, LANE), F32),
        scratch_shapes=[pltpu.VMEM((n_chain, S5_TILE // HALO, LANE), F32)] * 2
                       + [pltpu.VMEM((n_chain, HALO, LANE), F32)] * 2
                       + [pltpu.VMEM((n_chain, 4, LANE, LANE), BF16)],
        compiler_params=pltpu.CompilerParams(
            dimension_semantics=("arbitrary", "arbitrary"), vmem_limit_bytes=VMEM_LIMIT),
        name="s5_scan",
    )(*([proj] * n_lb), p["lam_re"], p["lam_im"], p["log_dt"], p["b_re"], p["b_im"], p["c_re"], p["c_im"], p["d"])


def _s5_glu_kernel(y_ref, w_ref, b_ref, z_ref, o_ref):
    y = jnp.concatenate([y_ref[j] for j in range(S5_WIDTH // LANE)], axis=1)
    gate = _sigmoid(_mm(y, w_ref[...]) + b_ref[...])
    o_ref[...] = (y * gate * _silu(z_ref[...])).astype(o_ref.dtype)


def _s5_glu(yc, proj, glu_w, glu_b, layer, tm=512):
    t = yc.shape[1]
    tm = min(tm, t)
    return pl.pallas_call(
        _s5_glu_kernel,
        grid=(t // tm,),
        in_specs=[pl.BlockSpec((S5_WIDTH // LANE, tm, LANE), lambda i: (0, i, 0)),
                  pl.BlockSpec((None, S5_WIDTH, S5_WIDTH), lambda i: (layer, 0, 0)),
                  pl.BlockSpec((None, 1, S5_WIDTH), lambda i: (layer, 0, 0)),
                  pl.BlockSpec((tm, S5_WIDTH), lambda i: (i, COL_SZ * LANE // S5_WIDTH))],
        out_specs=pl.BlockSpec((tm, S5_WIDTH), lambda i: (i, 0)),
        out_shape=jax.ShapeDtypeStruct((t, S5_WIDTH), BF16),
        compiler_params=pltpu.CompilerParams(
            dimension_semantics=("arbitrary",), vmem_limit_bytes=VMEM_LIMIT),
        name="s5_glu",
    )(yc, glu_w, glu_b, proj)


def _merge_kernel(oa_ref, ob_ref, oc_ref, wb_ref, ga_ref, gb_ref, gc_ref, gbias_ref, o_ref):
    acc = None
    for i, (o_r, g_r) in enumerate(((oa_ref, ga_ref), (ob_ref, gb_ref), (oc_ref, gc_ref))):
        proj = jnp.dot(o_r[...], wb_ref[i], preferred_element_type=F32)
        term = _sigmoid(g_r[...] + gbias_ref[i]) * proj
        acc = term if acc is None else acc + term
    o_ref[...] = acc.astype(o_ref.dtype)


def _merge(oa, ob, oc, proj, w_branch, gate_b, layer, tm=1024, tn=512):
    t = oa.shape[0]
    tm = min(tm, t)
    gbase = COL_GATE * LANE // tn
    per = D_MODEL // tn
    ospec = pl.BlockSpec((tm, 1024), lambda i, j: (i, 0))
    gspec = lambda br: pl.BlockSpec((tm, tn), lambda i, j, br=br: (i, gbase + br * per + j))
    return pl.pallas_call(
        _merge_kernel,
        grid=(t // tm, per),
        in_specs=[ospec, ospec, ospec,
                  pl.BlockSpec((None, N_BRANCH, 1024, tn), lambda i, j: (layer, 0, 0, j)),
                  gspec(0), gspec(1), gspec(2),
                  pl.BlockSpec((None, N_BRANCH, 1, tn), lambda i, j: (layer, 0, 0, j))],
        out_specs=pl.BlockSpec((tm, tn), lambda i, j: (i, j)),
        out_shape=jax.ShapeDtypeStruct((t, D_MODEL), BF16),
        compiler_params=pltpu.CompilerParams(
            dimension_semantics=("arbitrary", "arbitrary"), vmem_limit_bytes=VMEM_LIMIT),
        name="merge",
    )(oa, ob, oc, w_branch, proj, proj, proj, gate_b)


def _outproj_kernel(m_ref, w_ref, x_ref, nw_ref, *o_refs):
    x = x_ref[...] + jnp.dot(m_ref[...], w_ref[...], preferred_element_type=F32)
    normed = _rms(x, nw_ref[...])
    if len(o_refs) == 2:
        o_refs[0][...] = x
    o_refs[-1][...] = normed.astype(o_refs[-1].dtype)


def _outproj(merged, w_out, x2, next_norm_rows, layer, last, tm=512):
    t = x2.shape[0]
    tm = min(tm, t)
    row_blk = pl.BlockSpec((tm, D_MODEL), lambda i: (i, 0))
    if last:
        out_specs, out_shape = row_blk, jax.ShapeDtypeStruct((t, D_MODEL), F32)
    else:
        out_specs = (row_blk, row_blk)
        out_shape = (jax.ShapeDtypeStruct((t, D_MODEL), F32), jax.ShapeDtypeStruct((t, D_MODEL), BF16))
    return pl.pallas_call(
        _outproj_kernel,
        grid=(t // tm,),
        in_specs=[row_blk,
                  pl.BlockSpec((None, D_MODEL, D_MODEL), lambda i: (layer, 0, 0)),
                  row_blk,
                  pl.BlockSpec((None, 1, D_MODEL), lambda i: (layer, 0, 0))],
        out_specs=out_specs,
        out_shape=out_shape,
        compiler_params=pltpu.CompilerParams(
            dimension_semantics=("arbitrary",), vmem_limit_bytes=VMEM_LIMIT),
        name="outproj",
    )(merged, w_out, x2, next_norm_rows)


def _pad_cols(a, width):
    return jnp.pad(a, [(0, 0)] * (a.ndim - 1) + [(0, width - a.shape[-1])])


_W_SRC_GBA = 4096
_W_SRC_RKV = _W_SRC_GBA + 2 * GDN_HEADS
_W_SRC_WL = _W_SRC_RKV + 3 * RWKV_WIDTH
_W_SRC_RZ = _W_SRC_WL + 2 * RWKV_LORA
_W_BLK = 512


def _relayout_kernel(main_ref, gba_ref, wl_ref, al_ref, o_ref):
    j = pl.program_id(0)
    tail = COL_GBA * LANE // _W_BLK

    @pl.when(j < tail)
    def _():
        o_ref[...] = jnp.transpose(main_ref[0]).astype(BF16)

    @pl.when(j == tail)
    def _():
        k = main_ref.shape[2]
        zeros = lambda n: jnp.zeros((n, k), F32)
        rows = jnp.concatenate([gba_ref[0], zeros(LANE - 2 * GDN_HEADS),
                                wl_ref[0], zeros(LANE - RWKV_LORA),
                                al_ref[0], zeros(LANE - RWKV_LORA), zeros(LANE)], axis=0)
        o_ref[...] = jnp.transpose(rows).astype(BF16)


def _relayout_w_in(w_t, layer):
    k = w_t.shape[2]
    first_rkv, first_rz, tail = COL_RKV * LANE // _W_BLK, COL_RZ * LANE // _W_BLK, COL_GBA * LANE // _W_BLK

    unit = 2 * GDN_HEADS

    def start(j):
        shift = jnp.where(j < first_rkv, 0, jnp.where(j < first_rz, (_W_SRC_RKV - COL_RKV * LANE) // unit,
                                                       (_W_SRC_RZ - COL_RZ * LANE) // unit))
        return jnp.where(j < tail, j * (_W_BLK // unit) + shift, 0) * unit

    rows_at = lambda n, row: pl.BlockSpec((pl.Element(1), pl.Element(n), pl.Element(k)),
                                          lambda j: (layer, row(j), 0))
    fixed = lambda n, row: rows_at(n, lambda j: row)
    return pl.pallas_call(
        _relayout_kernel,
        grid=(NP // _W_BLK,),
        in_specs=[rows_at(_W_BLK, start),
                  fixed(2 * GDN_HEADS, _W_SRC_GBA), fixed(RWKV_LORA, _W_SRC_WL),
                  fixed(RWKV_LORA, _W_SRC_WL + RWKV_LORA)],
        out_specs=pl.BlockSpec((k, _W_BLK), lambda j: (0, j)),
        out_shape=jax.ShapeDtypeStruct((k, NP), BF16),
        compiler_params=pltpu.CompilerParams(dimension_semantics=("arbitrary",), vmem_limit_bytes=VMEM_LIMIT),
        name="relayout_w",
    )(w_t, w_t, w_t, w_t)


def _pair_s5(a, rows_are_channels):
    n_blk, n_pair = S5_WIDTH // LANE, LANE // (2 * S5_GROUP)
    cs = a if rows_are_channels else jnp.swapaxes(a, 1, 2)
    cs = cs.reshape(n_blk, n_pair, 2, S5_GROUP, S5_STATE)
    same = jnp.eye(2, dtype=bool)[None, None, :, None, :, None]
    paired = jnp.where(same, cs[:, :, :, :, None, :], 0.0)
    return paired.reshape(n_blk, n_pair, 2 * S5_GROUP, 2 * S5_STATE)


def _s5_rows(a):
    return a.reshape(S5_WIDTH // LANE, LANE // (2 * S5_GROUP), 2 * S5_STATE)


def _rows(a):
    return a.reshape(a.shape[0], 1, -1)


def _gdn_head_rows(a):
    return _pad_cols(_rows(jnp.pad(a, ((0, 0), (GDN_HEADS, 0)))), LANE)


def _rwkv_params(mu, w0, w_up, a0, a_up, k_k, k_a, r_k, lnx_w, lnx_b):
    lora = lambda a: jnp.pad(a, ((0, 0), (0, LANE - RWKV_LORA), (0, 0))).astype(BF16)
    w3 = 3 * RWKV_WIDTH
    return dict(mu_r=_rows(mu[:, :RWKV_WIDTH]), mu_k=_rows(mu[:, RWKV_WIDTH:2 * RWKV_WIDTH]),
                mu_v=_rows(mu[:, 2 * RWKV_WIDTH:w3]),
                mu_wl=_pad_cols(_rows(mu[:, w3:w3 + RWKV_LORA]), LANE),
                mu_al=_pad_cols(_rows(mu[:, w3 + RWKV_LORA:]), LANE),
                w0=_rows(w0), w_up=lora(w_up), a0=_rows(a0), a_up=lora(a_up), k_k=_rows(k_k), k_a=_rows(k_a),
                r_k=_rows(r_k), lnx_w=_rows(lnx_w), lnx_b=_rows(lnx_b))


def _s5_params(a_re, a_im, log_dt, b_re, b_im, c_re, c_im, d):
    depth = a_re.shape[0]
    rows = jax.vmap(_s5_rows)
    place = lambda a, t: jax.vmap(lambda m: _pair_s5(m, t))(a)
    return dict(lam_re=rows(a_re), lam_im=rows(a_im),
                log_dt=rows(jnp.broadcast_to(log_dt[:, :, None], (depth, S5_WIDTH // S5_GROUP, S5_STATE))),
                b_re=place(b_re, False), b_im=place(b_im, False),
                c_re=place(c_re, True), c_im=place(c_im, True),
                d=d.reshape(depth, S5_WIDTH // LANE, 1, LANE))


def kernel(x, norm_w, w_in, gdn_conv_w, gdn_a_log, gdn_dt_bias, gdn_norm_w, rwkv_mu, rwkv_w0, rwkv_w_up,
           rwkv_a0, rwkv_a_up, rwkv_k_k, rwkv_k_a, rwkv_r_k, rwkv_lnx_w, rwkv_lnx_b, s5_a_re, s5_a_im,
           s5_log_dt, s5_b_re, s5_b_im, s5_c_re, s5_c_im, s5_d, s5_glu_w, s5_glu_b, gate_b, w_branch,
           w_out, final_norm_w):
    batch, seq, _ = x.shape
    depth = w_in.shape[0]
    x2 = x.reshape(batch * seq, D_MODEL)
    h = _rmsnorm(x2, norm_w[0].reshape(1, D_MODEL))

    gdn_alog, gdn_dtb, gdn_nw = _gdn_head_rows(gdn_a_log), _gdn_head_rows(gdn_dt_bias), _rows(gdn_norm_w)
    rp = _rwkv_params(rwkv_mu, rwkv_w0, rwkv_w_up, rwkv_a0, rwkv_a_up, rwkv_k_k, rwkv_k_a, rwkv_r_k,
                      rwkv_lnx_w, rwkv_lnx_b)
    sp = _s5_params(s5_a_re, s5_a_im, s5_log_dt, s5_b_re, s5_b_im, s5_c_re, s5_c_im, s5_d)
    glu_w, glu_b = s5_glu_w.astype(BF16), _rows(s5_glu_b)
    wb, gb = w_branch.astype(BF16), gate_b.reshape(depth, N_BRANCH, 1, D_MODEL)
    wo = w_out.astype(BF16)
    next_norm = _rows(jnp.concatenate([norm_w[1:], final_norm_w[None, :]], axis=0))
    w_in_t = jnp.swapaxes(w_in, 1, 2)

    for i in range(depth):
        proj = _inproj(h, _relayout_w_in(w_in_t, i))
        o_a, o_b = _gdn_rwkv(proj, (gdn_conv_w, gdn_alog, gdn_dtb, gdn_nw), rp, i, batch, seq)
        o_c = _s5_glu(_s5(proj, sp, i, batch, seq), proj, glu_w, glu_b, i)
        merged = _merge(o_a, o_b, o_c, proj, wb, gb, i)
        if i == depth - 1:
            return _outproj(merged, wo, x2, next_norm, i, True).reshape(batch, seq, D_MODEL)
        x2, h = _outproj(merged, wo, x2, next_norm, i, False)
```

```python
import functools
import math

import jax
import jax.numpy as jnp
from jax import lax
from jax.experimental import pallas as pl
from jax.experimental.pallas import tpu as pltpu

F32 = jnp.float32
BF16 = jnp.bfloat16

D_MODEL = 2048
GDN_HEADS = 8
GDN_HEAD_DIM = 128
GDN_WIDTH = 1024
RWKV_HEAD_DIM = 64
RWKV_WIDTH = 1024
RWKV_LORA = 96
RWKV_LN_EPS = 64e-5
S5_GROUP = 16
S5_STATE = 64
S5_WIDTH = 1024
N_BRANCH = 3
NORM_EPS = 1e-6

LANE = 128
MXU_TILE = 256
HALO = 8
CHUNK = 64
SEQ_BLOCK = 512
S5_TILE = 256
VMEM_LIMIT = 52 * 1024 * 1024

COL_QKV = 0
COL_GZ = 24
COL_RKV = 32
COL_RZ = 56
COL_SU = 64
COL_SZ = 72
COL_GATE = 80
COL_GBA = 128
COL_WL = 129
COL_AL = 130
N_COLBLK = 132
NP = N_COLBLK * LANE


def _sigmoid(x):
    return 1.0 / (1.0 + jnp.exp(-x))


def _silu(x):
    return x * _sigmoid(x)


def _softplus(x):
    return jnp.maximum(x, 0.0) + jnp.log1p(jnp.exp(-jnp.abs(x)))


def _mm(a, b):
    return jnp.dot(a.astype(BF16), b.astype(BF16), preferred_element_type=F32)


def _mm_nt(a, b):
    return lax.dot_general(a.astype(BF16), b.astype(BF16), (((1,), (1,)), ((), ())),
                           preferred_element_type=F32)


def _mm_tn(a, b):
    return lax.dot_general(a.astype(BF16), b.astype(BF16), (((0,), (0,)), ((), ())),
                           preferred_element_type=F32)


def _split_bf16(x, parts):
    out = []
    for _ in range(parts - 1):
        hi = x.astype(BF16)
        out.append(hi)
        x = x - hi.astype(F32)
    out.append(x.astype(BF16))
    return out


def _mm_sel_lhs(sel, x, parts=2):
    cols = x.shape[1]
    res = jnp.dot(sel.astype(BF16), jnp.concatenate(_split_bf16(x, parts), axis=1), preferred_element_type=F32)
    out = res[:, :cols]
    for i in range(1, parts):
        out = out + res[:, i * cols:(i + 1) * cols]
    return out


def _mm_sel_rhs(x, sel, parts=2):
    rows = x.shape[0]
    res = jnp.dot(jnp.concatenate(_split_bf16(x, parts), axis=0), sel.astype(BF16), preferred_element_type=F32)
    out = res[:rows, :]
    for i in range(1, parts):
        out = out + res[i * rows:(i + 1) * rows, :]
    return out


def _unit_lower_inverse(x, eye, size, stack, out):
    rows = x.shape[0]
    y = x
    q = eye + x
    y = jnp.dot(x.astype(BF16), stack(x.astype(BF16)), preferred_element_type=F32)
    yield
    span = 4
    while span < size:
        res = jnp.dot(jnp.concatenate([y, q], axis=0).astype(BF16), stack(y.astype(BF16)),
                      preferred_element_type=F32)
        yield
        y, q = res[:rows, :], q + res[rows:, :]
        span *= 2
    q = q + jnp.dot(q.astype(BF16), stack(y.astype(BF16)), preferred_element_type=F32)
    yield
    out.append(q)


def _run_lockstep(chains):
    live = list(chains)
    while live:
        for g in list(live):
            try:
                next(g)
            except StopIteration:
                live.remove(g)


def _iota2(shape, dim):
    return lax.broadcasted_iota(jnp.int32, shape, dim)


def _chunk_and_shifts(ref, halo_ref, c, rows, shift):
    r0 = pl.multiple_of(c * rows, rows)
    h0 = pl.multiple_of(jnp.maximum(r0 - HALO, 0), HALO)
    halo = jnp.where(c == 0, halo_ref[...], ref[pl.ds(h0, HALO), :])
    cur = ref[pl.ds(r0, rows), :]
    xc = jnp.concatenate([halo, cur], axis=0)
    return cur, [pltpu.roll(xc, s, 0)[HALO:, :] for s in range(1, shift + 1)]


def _recurrence_grid_setup(state_ref, halo_refs):
    @pl.when(pl.program_id(2) == 0)
    def _():
        state_ref[...] = jnp.zeros_like(state_ref)
        for h in halo_refs:
            h[...] = jnp.zeros_like(h)


def _save_halos(pairs, rows):
    for src, dst in pairs:
        dst[...] = src[rows - HALO:rows, :]


def _rms(x, w_row):
    return x * lax.rsqrt(jnp.mean(x * x, axis=-1, keepdims=True) + NORM_EPS) * w_row


def _rmsnorm_kernel(x_ref, nw_ref, o_ref):
    o_ref[...] = _rms(x_ref[...], nw_ref[...]).astype(o_ref.dtype)


def _rmsnorm(x2, norm_w_row, tm=512):
    t = x2.shape[0]
    tm = min(tm, t)
    return pl.pallas_call(
        _rmsnorm_kernel,
        grid=(t // tm,),
        in_specs=[pl.BlockSpec((tm, D_MODEL), lambda i: (i, 0)),
                  pl.BlockSpec((1, D_MODEL), lambda i: (0, 0))],
        out_specs=pl.BlockSpec((tm, D_MODEL), lambda i: (i, 0)),
        out_shape=jax.ShapeDtypeStruct((t, D_MODEL), BF16),
        compiler_params=pltpu.CompilerParams(dimension_semantics=("arbitrary",), vmem_limit_bytes=VMEM_LIMIT),
        name="rmsnorm",
    )(x2, norm_w_row)


_W_SRC_GBA = 4096
_W_SRC_RKV = _W_SRC_GBA + 2 * GDN_HEADS
_W_SRC_WL = _W_SRC_RKV + 3 * RWKV_WIDTH
_W_SRC_RZ = _W_SRC_WL + 2 * RWKV_LORA
_W_BLK = 512


def _inproj_t_kernel(h_ref, main_ref, gba_ref, wl_ref, al_ref, o_ref):
    tail = COL_GBA * LANE // _W_BLK
    nt = (((1,), (1,)), ((), ()))

    @pl.when(pl.program_id(1) < tail)
    def _():
        o_ref[...] = lax.dot_general(h_ref[...], main_ref[0].astype(BF16), nt, preferred_element_type=F32)

    @pl.when(pl.program_id(1) == tail)
    def _():
        k = main_ref.shape[2]
        zeros = lambda n: jnp.zeros((n, k), BF16)
        rows = jnp.concatenate([gba_ref[0].astype(BF16), zeros(LANE - 2 * GDN_HEADS),
                                wl_ref[0].astype(BF16), zeros(LANE - RWKV_LORA),
                                al_ref[0].astype(BF16), zeros(LANE - RWKV_LORA), zeros(LANE)], axis=0)
        o_ref[...] = lax.dot_general(h_ref[...], rows, nt, preferred_element_type=F32)


def _inproj_t(h, w_t, layer, tm=2048):
    t, k = h.shape
    tm = min(tm, t)
    first_rkv, first_rz, tail = COL_RKV * LANE // _W_BLK, COL_RZ * LANE // _W_BLK, COL_GBA * LANE // _W_BLK
    unit = 2 * GDN_HEADS

    def start(j):
        shift = jnp.where(j < first_rkv, 0, jnp.where(j < first_rz, (_W_SRC_RKV - COL_RKV * LANE) // unit,
                                                       (_W_SRC_RZ - COL_RZ * LANE) // unit))
        return jnp.where(j < tail, j * (_W_BLK // unit) + shift, 0) * unit

    rows_at = lambda n, row: pl.BlockSpec((pl.Element(1), pl.Element(n), pl.Element(k)),
                                          lambda i, j: (layer, row(j), 0))
    fixed = lambda n, row: rows_at(n, lambda j: row)
    return pl.pallas_call(
        _inproj_t_kernel,
        grid=(t // tm, NP // _W_BLK),
        in_specs=[pl.BlockSpec((tm, k), lambda i, j: (i, 0)),
                  rows_at(_W_BLK, start),
                  fixed(2 * GDN_HEADS, _W_SRC_GBA), fixed(RWKV_LORA, _W_SRC_WL),
                  fixed(RWKV_LORA, _W_SRC_WL + RWKV_LORA)],
        out_specs=pl.BlockSpec((tm, _W_BLK), lambda i, j: (i, j)),
        out_shape=jax.ShapeDtypeStruct((t, NP), F32),
        compiler_params=pltpu.CompilerParams(
            dimension_semantics=("arbitrary", "arbitrary"), vmem_limit_bytes=VMEM_LIMIT),
        name="inproj",
    )(h, w_t, w_t, w_t, w_t)


def _gdn_parts(q_ref, k_ref, v_ref, z_ref, ba_ref, cq_ref, ck_ref, cv_ref,
               alog_ref, dtb_ref, nw_ref, o_ref, s_ref, hq_ref, hk_ref, hv_ref, *, blk_rows, n_tile):
    c_len = CHUNK
    rows = 2 * c_len
    width = MXU_TILE
    tile0 = pl.program_id(1) * n_tile

    lane_w = _iota2((1, width), 1)
    lane_b = _iota2((1, LANE), 1)
    t_w = _iota2((c_len, rows), 0)
    s_w = _iota2((c_len, rows), 1) % c_len
    strict_w = s_w < t_w
    incl_w = s_w <= t_w
    eye_w = jnp.where(s_w == t_w, 1.0, 0.0)
    triu_w = jnp.where(t_w <= s_w, 1.0, 0.0)
    ti = _iota2((c_len, c_len), 0)
    tj = _iota2((c_len, c_len), 1)
    tril_t = jnp.where(tj <= ti, 1.0, 0.0)
    ones_t = jnp.ones((c_len, c_len), F32)
    same_head = (_iota2((width, width), 0) // GDN_HEAD_DIM) == (_iota2((width, width), 1) // GDN_HEAD_DIM)
    head0_lane = lane_w < GDN_HEAD_DIM
    head0_wide = _iota2((1, rows), 1) < c_len
    zero_b = jnp.zeros((), BF16)

    _recurrence_grid_setup(s_ref, (hq_ref, hk_ref, hv_ref))
    alog_row = alog_ref[...]
    dtb_row = dtb_ref[...]
    nw_row = nw_ref[...]

    def stack(xb):
        return jnp.concatenate([jnp.where(head0_lane, xb, zero_b), jnp.where(head0_lane, zero_b, xb)], axis=0)

    def stack_w(xb):
        return jnp.concatenate([jnp.where(head0_wide, xb, zero_b), jnp.where(head0_wide, zero_b, xb)], axis=0)

    def per_head(x, fn):
        return jnp.concatenate([fn(x[:, :GDN_HEAD_DIM]), fn(x[:, GDN_HEAD_DIM:])], axis=1)

    def l2n(xh):
        return xh * lax.rsqrt(jnp.sum(xh * xh, axis=-1, keepdims=True) + 1e-6)

    def rms(oh):
        return oh * lax.rsqrt(jnp.mean(oh * oh, axis=-1, keepdims=True) + NORM_EPS) * nw_row

    def chains(c):
        r0 = pl.multiple_of(c * c_len, c_len)

        def conv(x_ref, h_ref, cw_ref):
            cur, sh = _chunk_and_shifts(x_ref, h_ref, c, c_len, 3)
            cw = cw_ref[...]
            acc = cur * cw[3:4, :]
            for s in range(1, 4):
                acc = acc + sh[s - 1] * cw[3 - s:4 - s, :]
            return _silu(acc)

        q_all = conv(q_ref, hq_ref, cq_ref)
        k_all = conv(k_ref, hk_ref, ck_ref)
        v_all = conv(v_ref, hv_ref, cv_ref)
        z_all = z_ref[pl.ds(r0, c_len), :]
        ba = ba_ref[pl.ds(r0, c_len), :]
        g_all = -jnp.exp(alog_row) * _softplus(ba + dtb_row)

        def chain(t):
            sl = slice(t * width, (t + 1) * width)
            q = per_head(q_all[:, sl], l2n) * (GDN_HEAD_DIM ** -0.5)
            k = per_head(k_all[:, sl], l2n)
            v = v_all[:, sl]
            betas, gs = [], []
            for h in range(2):
                hid = 2 * (tile0 + t) + h
                betas.append(_sigmoid(jnp.sum(jnp.where(lane_b == hid, ba, 0.0), axis=-1, keepdims=True)))
                gs.append(jnp.sum(jnp.where(lane_b == GDN_HEADS + hid, g_all, 0.0), axis=-1, keepdims=True))
            beta = jnp.where(head0_lane, betas[0], betas[1])
            g_n = jnp.where(head0_lane, gs[0], gs[1])
            g_w = jnp.where(head0_wide, gs[0], gs[1])

            gcol = _mm_sel_lhs(tril_t, g_n)
            grow = _mm_sel_lhs(ones_t, g_w * triu_w)
            yield
            gcol_w = jnp.where(head0_wide, gcol[:, :rows], gcol[:, GDN_HEAD_DIM:GDN_HEAD_DIM + rows])
            diff = gcol_w - grow
            d_strict = jnp.where(strict_w, jnp.exp(jnp.where(strict_w, diff, 0.0)), 0.0)
            d_incl = jnp.where(incl_w, jnp.exp(jnp.where(incl_w, diff, 0.0)), 0.0)
            egc = jnp.exp(gcol)
            gl_lane = gcol[c_len - 1:c_len, :]

            kb = k * beta
            prod = lax.dot_general(jnp.concatenate([kb, q], axis=0).astype(BF16), stack(k.astype(BF16)),
                                   (((1,), (1,)), ((), ())), preferred_element_type=F32)
            yield
            lower = prod[:c_len, :] * d_strict
            attn = prod[c_len:, :] * d_incl
            inv = []
            yield from _unit_lower_inverse(-lower, eye_w, c_len, stack_w, inv)
            rhs = jnp.concatenate([stack((v * beta).astype(BF16)), stack((kb * egc).astype(BF16))], axis=1)
            uw = jnp.dot(inv[0].astype(BF16), rhs, preferred_element_type=F32)
            yield
            u, w = uw[:, :width], uw[:, width:]

            state = s_ref[t]
            x0 = _mm(jnp.concatenate([w, q * egc], axis=0), state)
            yield
            v_new = u - x0[:c_len, :]
            o = x0[c_len:, :] + jnp.dot(attn.astype(BF16), stack(v_new.astype(BF16)), preferred_element_type=F32)
            ds = _mm_tn(k * jnp.exp(gl_lane - gcol), v_new)
            s_ref[t] = state * jnp.exp(gl_lane) + jnp.where(same_head, ds, 0.0)
            yield

            o_ref[pl.ds(r0, c_len), sl] = (per_head(o, rms) * _silu(z_all[:, sl])).astype(o_ref.dtype)

        return [chain(t) for t in range(n_tile)]

    def finish():
        _save_halos(((q_ref, hq_ref), (k_ref, hk_ref), (v_ref, hv_ref)), blk_rows)

    return chains, finish


def _gdn_specs(proj, conv_w, alog_row, dtb_row, nw_row, layer, blk, nsb):
    w = GDN_WIDTH
    colspec = lambda base: pl.BlockSpec((blk, w), lambda b, g, s, base=base: (b * nsb + s, base))
    cwspec = lambda base: pl.BlockSpec((None, 4, w), lambda b, g, s, base=base: (layer, 0, base))
    rowspec = pl.BlockSpec((None, 1, LANE), lambda b, g, s: (layer, 0, 0))
    in_specs = [colspec(0), colspec(1), colspec(2), colspec(COL_GZ * LANE // GDN_WIDTH),
                pl.BlockSpec((blk, LANE), lambda b, g, s: (b * nsb + s, COL_GBA)),
                cwspec(0), cwspec(1), cwspec(2), rowspec, rowspec, rowspec]
    scratch = [pltpu.VMEM((w // MXU_TILE, MXU_TILE, MXU_TILE), F32)] + [pltpu.VMEM((HALO, w), F32)] * 3
    return (proj, proj, proj, proj, proj, conv_w, conv_w, conv_w, alog_row, dtb_row, nw_row), in_specs, scratch


def _rwkv_parts(r_ref, k_ref, v_ref, wl_ref, al_ref, z_ref,
                mur_ref, muk_ref, muv_ref, muwl_ref, mual_ref,
                w0_ref, wup_ref, a0_ref, aup_ref, kk_ref, ka_ref, rk_ref, lnw_ref, lnb_ref,
                o_ref, n_ref, hr_ref, hk_ref, hv_ref, hwl_ref, hal_ref, *, blk_rows, n_tile):
    c_len = CHUNK
    nh = MXU_TILE // RWKV_HEAD_DIM
    width = MXU_TILE
    assert c_len == RWKV_HEAD_DIM

    same_head = (_iota2((width, width), 0) // RWKV_HEAD_DIM) == (_iota2((width, width), 1) // RWKV_HEAD_DIM)
    head_sum = jnp.where(same_head, 1.0, 0.0)
    t_w = _iota2((c_len, width), 0)
    s_w = _iota2((c_len, width), 1) % c_len
    strict_w = s_w < t_w
    incl_w = s_w <= t_w
    eye_w = jnp.where(s_w == t_w, 1.0, 0.0)
    ti = _iota2((c_len, c_len), 0)
    tj = _iota2((c_len, c_len), 1)
    tril_t = jnp.where(tj <= ti, 1.0, 0.0)
    lane_head = _iota2((1, width), 1) // RWKV_HEAD_DIM
    zero_b = jnp.zeros((), BF16)

    _recurrence_grid_setup(n_ref, (hr_ref, hk_ref, hv_ref, hwl_ref, hal_ref))

    def stack(xb):
        return jnp.concatenate([jnp.where(lane_head == h, xb, zero_b) for h in range(nh)], axis=0)

    def chains(c):
        r0 = pl.multiple_of(c * c_len, c_len)

        def shifted(x_ref, h_ref, mu_ref):
            cur, sh = _chunk_and_shifts(x_ref, h_ref, c, c_len, 1)
            return cur + (sh[0] - cur) * mu_ref[...]

        r_all = shifted(r_ref, hr_ref, mur_ref)
        k_all = shifted(k_ref, hk_ref, muk_ref)
        v_all = shifted(v_ref, hv_ref, muv_ref)
        wl = shifted(wl_ref, hwl_ref, muwl_ref)
        al = shifted(al_ref, hal_ref, mual_ref)
        z_all = z_ref[pl.ds(r0, c_len), :]

        w_pre = w0_ref[...] + _mm(jnp.tanh(wl), wup_ref[...])
        logw_all = -jnp.exp(-_softplus(-w_pre) - 0.5)
        a_all = _sigmoid(a0_ref[...] + _mm(al, aup_ref[...]))
        kkx_all = k_all * kk_ref[...]
        k2_all = k_all * (1.0 + (a_all - 1.0) * ka_ref[...])
        lc_all = _mm_sel_lhs(tril_t, logw_all)
        bonus_all = r_all * k2_all * rk_ref[...]

        def chain(t):
            sl = slice(t * width, (t + 1) * width)
            r, v, a, k2, logw, lc = r_all[:, sl], v_all[:, sl], a_all[:, sl], k2_all[:, sl], logw_all[:, sl], lc_all[:, sl]
            kkx = kkx_all[:, sl]
            sums = _mm_sel_rhs(jnp.concatenate([kkx * kkx, bonus_all[:, sl]], axis=0), head_sum)
            yield
            kk = kkx * lax.rsqrt(sums[:c_len, :] + 1e-6)
            bonus = sums[c_len:, :]

            lc_last = lc[c_len - 1:c_len, :]
            e_neg = jnp.exp(-lc)
            e_rem = jnp.exp(lc_last - lc)
            kka = kk * a
            a_n = -kk * jnp.exp(lc - logw)
            r_n = r * jnp.exp(lc)
            sv = stack(v.astype(BF16))

            prod = lax.dot_general(
                jnp.concatenate([a_n, r_n], axis=0).astype(BF16),
                jnp.concatenate([stack((kka * e_neg).astype(BF16)), stack((k2 * e_neg).astype(BF16))], axis=0),
                (((1,), (1,)), ((), ())), preferred_element_type=F32)
            yield
            a_ab = jnp.where(strict_w, prod[:c_len, :width], 0.0)
            a_ak = jnp.where(strict_w, prod[:c_len, width:], 0.0)
            a_rb = jnp.where(incl_w, prod[c_len:, :width], 0.0)
            a_rk = jnp.where(incl_w, prod[c_len:, width:], 0.0)

            inv = []
            yield from _unit_lower_inverse(a_ab, eye_w, c_len, stack, inv)
            t_mat = inv[0].astype(BF16)
            w_p = jnp.dot(t_mat, stack(a_n.astype(BF16)), preferred_element_type=F32)
            aks = jnp.dot(a_ak.astype(BF16), sv, preferred_element_type=F32)
            yield
            v_p = jnp.dot(t_mat, stack(aks.astype(BF16)), preferred_element_type=F32)
            yield

            state = n_ref[t]
            x0 = _mm(jnp.concatenate([w_p, r_n], axis=0), state)
            yield
            u = x0[:c_len, :] + v_p
            y = x0[c_len:, :] + jnp.dot(jnp.concatenate([a_rb, a_rk], axis=1).astype(BF16),
                                        jnp.concatenate([stack(u.astype(BF16)), sv], axis=0),
                                        preferred_element_type=F32)
            dn = _mm_tn(jnp.concatenate([kka * e_rem, k2 * e_rem], axis=0), jnp.concatenate([u, v], axis=0))
            gam_col = jnp.transpose(jnp.broadcast_to(jnp.exp(lc_last), (width, width)))
            n_ref[t] = gam_col * state + jnp.where(same_head, dn, 0.0)
            yield

            inv_n = 1.0 / RWKV_HEAD_DIM
            mean = _mm_sel_rhs(y, head_sum) * inv_n
            yield
            dlt = y - mean
            var = _mm_sel_rhs(dlt * dlt, head_sum) * inv_n
            yield
            y = dlt * lax.rsqrt(var + RWKV_LN_EPS) * lnw_ref[:, sl] + lnb_ref[:, sl]
            y = y + bonus * v
            o_ref[pl.ds(r0, c_len), sl] = (y * _silu(z_all[:, sl])).astype(o_ref.dtype)

        return [chain(t) for t in range(n_tile)]

    def finish():
        _save_halos(((r_ref, hr_ref), (k_ref, hk_ref), (v_ref, hv_ref), (wl_ref, hwl_ref), (al_ref, hal_ref)),
                    blk_rows)

    return chains, finish


def _rwkv_specs(proj, p, layer, blk, nsb):
    w = RWKV_WIDTH
    colspec = lambda base: pl.BlockSpec((blk, w), lambda b, g, s, base=base: (b * nsb + s, base))
    lspec = lambda col: pl.BlockSpec((blk, LANE), lambda b, g, s, col=col: (b * nsb + s, col))
    grow = pl.BlockSpec((None, 1, w), lambda b, g, s: (layer, 0, 0))
    lrow = pl.BlockSpec((None, 1, LANE), lambda b, g, s: (layer, 0, 0))
    upspec = pl.BlockSpec((None, LANE, w), lambda b, g, s: (layer, 0, 0))
    base = COL_RKV * LANE // RWKV_WIDTH
    in_specs = [colspec(base), colspec(base + 1), colspec(base + 2), lspec(COL_WL), lspec(COL_AL),
                colspec(COL_RZ * LANE // RWKV_WIDTH),
                grow, grow, grow, lrow, lrow,
                grow, upspec, grow, upspec, grow, grow, grow, grow, grow]
    scratch = ([pltpu.VMEM((w // MXU_TILE, MXU_TILE, MXU_TILE), F32)] + [pltpu.VMEM((HALO, w), F32)] * 3
               + [pltpu.VMEM((HALO, LANE), F32)] * 2)
    operands = (proj, proj, proj, proj, proj, proj,
                p["mu_r"], p["mu_k"], p["mu_v"], p["mu_wl"], p["mu_al"],
                p["w0"], p["w_up"], p["a0"], p["a_up"], p["k_k"], p["k_a"], p["r_k"], p["lnx_w"], p["lnx_b"])
    return operands, in_specs, scratch


def _mixers_kernel(*refs, n_gdn, n_rwkv, n_gdn_scr, blk_rows):
    gdn_in, rwkv_in = refs[:n_gdn], refs[n_gdn:n_gdn + n_rwkv]
    o_gdn, o_rwkv = refs[n_gdn + n_rwkv:n_gdn + n_rwkv + 2]
    scratch = refs[n_gdn + n_rwkv + 2:]
    g_chains, g_finish = _gdn_parts(*gdn_in, o_gdn, *scratch[:n_gdn_scr], blk_rows=blk_rows,
                                    n_tile=GDN_WIDTH // MXU_TILE)
    r_chains, r_finish = _rwkv_parts(*rwkv_in, o_rwkv, *scratch[n_gdn_scr:], blk_rows=blk_rows,
                                     n_tile=RWKV_WIDTH // MXU_TILE)

    def body(c, carry):
        _run_lockstep([ch for pair in zip(g_chains(c), r_chains(c)) for ch in pair])
        return carry

    lax.fori_loop(0, blk_rows // CHUNK, body, 0)
    g_finish()
    r_finish()


def _gdn_rwkv(proj, gdn_args, rwkv_p, layer, batch, seq):
    blk = min(SEQ_BLOCK, seq)
    nsb = seq // blk
    g_ops, g_specs, g_scr = _gdn_specs(proj, *gdn_args, layer, blk, nsb)
    r_ops, r_specs, r_scr = _rwkv_specs(proj, rwkv_p, layer, blk, nsb)
    out_spec = pl.BlockSpec((blk, GDN_WIDTH), lambda b, g, s: (b * nsb + s, 0))
    out_shape = jax.ShapeDtypeStruct((batch * seq, GDN_WIDTH), BF16)
    return pl.pallas_call(
        functools.partial(_mixers_kernel, n_gdn=len(g_ops), n_rwkv=len(r_ops), n_gdn_scr=len(g_scr),
                          blk_rows=blk),
        grid=(batch, 1, nsb),
        in_specs=g_specs + r_specs,
        out_specs=(out_spec, out_spec),
        out_shape=(out_shape, out_shape),
        scratch_shapes=g_scr + r_scr,
        compiler_params=pltpu.CompilerParams(
            dimension_semantics=("arbitrary", "arbitrary", "arbitrary"), vmem_limit_bytes=VMEM_LIMIT),
        name="gdn_rwkv",
    )(*g_ops, *r_ops)


def _s5_kernel(*refs, seq, n_lb):
    u_refs = refs[:n_lb]
    (lre_ref, lim_ref, ldt_ref, bre_ref, bim_ref, cre_ref, cim_ref, d_ref,
     o_ref, pre_ref, pim_ref, qre_ref, qim_ref, w_ref) = refs[n_lb:]
    tile = S5_TILE
    n_pair = LANE // (2 * S5_GROUP)
    n_chain = n_lb * n_pair
    sub = HALO
    n_sub = tile // sub

    def cmul(ar, ai, br, bi):
        return ar * br - ai * bi, ar * bi + ai * br

    def block_scan(sr, si, ar, ai, out):
        in_blk = _iota2(sr.shape, 0) % sub
        d = 1
        while d < sub:
            keep = in_blk >= d
            tr, ti = cmul(ar, ai, jnp.where(keep, pltpu.roll(sr, d, 0), 0.0),
                          jnp.where(keep, pltpu.roll(si, d, 0), 0.0))
            sr, si = sr + tr, si + ti
            ar, ai = cmul(ar, ai, ar, ai)
            d *= 2
            yield
        out.extend((sr, si))

    first = _iota2((sub, LANE), 0) == 0
    ab = []
    for q in range(n_chain):
        lb, p = divmod(q, n_pair)
        lre = lre_ref[lb, p:p + 1, :]
        lim = lim_ref[lb, p:p + 1, :]
        dt = jnp.exp(ldt_ref[lb, p:p + 1, :])
        mag = jnp.exp(lre * dt)
        ab_re = mag * jnp.cos(lim * dt)
        ab_im = mag * jnp.sin(lim * dt)
        den = lre * lre + lim * lim
        coef_re = ((ab_re - 1.0) * lre + ab_im * lim) / den
        coef_im = (ab_im * lre - (ab_re - 1.0) * lim) / den
        b_re = bre_ref[lb, p]
        b_im = bim_ref[lb, p]

        def place(m, p=p):
            top, n = p * 2 * S5_GROUP, m.shape[0]
            parts = [jnp.zeros((top, LANE), BF16)] if top else []
            parts.append(m.astype(BF16))
            if LANE - top - n:
                parts.append(jnp.zeros((LANE - top - n, LANE), BF16))
            return jnp.concatenate(parts, axis=0)

        w_ref[q, 0] = place(coef_re * b_re - coef_im * b_im)
        w_ref[q, 1] = place(coef_re * b_im + coef_im * b_re)
        w_ref[q, 2] = place(cre_ref[lb, p])
        w_ref[q, 3] = place(cim_ref[lb, p])
        pr, pi = [ab_re], [ab_im]
        for _ in range(1, n_sub):
            nr, ni = cmul(pr[-1], pi[-1], ab_re, ab_im)
            pr.append(nr)
            pi.append(ni)
        pre_ref[q] = jnp.concatenate(pr, axis=0)
        pim_ref[q] = jnp.concatenate(pi, axis=0)
        imp_re = jnp.where(first, jnp.broadcast_to(pr[-1], (sub, LANE)), 0.0)
        imp_im = jnp.where(first, jnp.broadcast_to(pi[-1], (sub, LANE)), 0.0)
        pw = []
        for _ in block_scan(imp_re, imp_im, pr[-1], pi[-1], pw):
            pass
        qre_ref[q] = pw[0]
        qim_ref[q] = pw[1]
        ab.append((ab_re, ab_im, pr[-1], pi[-1]))

    def tile_body(i, st):
        t0 = pl.multiple_of(i * tile, tile)
        us = [jnp.concatenate([u_refs[lb][pl.ds(t0 + j, sub, stride=n_sub), :] for j in range(n_sub)], axis=0)
              for lb in range(n_lb)]
        ubs = [u.astype(BF16) for u in us]
        ys, new_st = [None] * n_chain, [None] * n_chain

        def chain(p):
            cr, ci = st[p]
            ar, ai, a_run_r, a_run_i = ab[p]
            ub = ubs[p // n_pair]
            sr = jnp.dot(ub, w_ref[p, 0], preferred_element_type=F32)
            si = jnp.dot(ub, w_ref[p, 1], preferred_element_type=F32)
            yield
            loc_r, loc_i = [sr[:sub, :]], [si[:sub, :]]
            for j in range(1, n_sub):
                tr, ti = cmul(ar, ai, loc_r[-1], loc_i[-1])
                loc_r.append(sr[j * sub:(j + 1) * sub, :] + tr)
                loc_i.append(si[j * sub:(j + 1) * sub, :] + ti)
                if j % 4 == 0:
                    yield
            res = []
            yield from block_scan(loc_r[-1], loc_i[-1], a_run_r, a_run_i, res)
            tr, ti = cmul(qre_ref[p], qim_ref[p], cr, ci)
            end_r, end_i = res[0] + tr, res[1] + ti
            new_st[p] = (end_r[sub - 1:sub, :], end_i[sub - 1:sub, :])
            in_r = jnp.where(first, cr, pltpu.roll(end_r, 1, 0))
            in_i = jnp.where(first, ci, pltpu.roll(end_i, 1, 0))
            out_r, out_i = [], []
            for j in range(n_sub):
                tr, ti = cmul(pre_ref[p, j:j + 1, :], pim_ref[p, j:j + 1, :], in_r, in_i)
                out_r.append(loc_r[j] + tr)
                out_i.append(loc_i[j] + ti)
            sr, si = jnp.concatenate(out_r, axis=0), jnp.concatenate(out_i, axis=0)
            nt = (((1,), (1,)), ((), ()))
            ys[p] = (lax.dot_general(sr.astype(BF16), w_ref[p, 2], nt, preferred_element_type=F32)
                     - lax.dot_general(si.astype(BF16), w_ref[p, 3], nt, preferred_element_type=F32))

        _run_lockstep([chain(p) for p in range(n_chain)])
        for lb in range(n_lb):
            y = d_ref[lb] * us[lb]
            for p in range(n_pair):
                y = y + ys[lb * n_pair + p]
            inner = math.sqrt(2.0 / math.pi) * (y + 0.044715 * (y * y * y))
            y = 0.5 * y * (1.0 + jnp.tanh(inner))
            for j in range(n_sub):
                o_ref[lb, pl.ds(t0 + j, sub, stride=n_sub), :] = y[j * sub:(j + 1) * sub, :]
        return tuple(new_st)

    zero = jnp.zeros((1, LANE), F32)
    lax.fori_loop(0, seq // tile, tile_body, tuple((zero, zero) for _ in range(n_chain)))


def _s5(proj, p, layer, batch, seq, n_lb=4):
    n_blk = S5_WIDTH // (n_lb * LANE)
    n_pair = LANE // (2 * S5_GROUP)
    n_chain = n_lb * n_pair
    rowspec = pl.BlockSpec((None, n_lb, n_pair, LANE), lambda b, j: (layer, j, 0, 0))
    matspec = pl.BlockSpec((None, n_lb, n_pair, 2 * S5_GROUP, LANE), lambda b, j: (layer, j, 0, 0, 0))
    return pl.pallas_call(
        functools.partial(_s5_kernel, seq=seq, n_lb=n_lb),
        grid=(batch, n_blk),
        in_specs=[pl.BlockSpec((seq, LANE), lambda b, j, lb=lb: (b, COL_SU + n_lb * j + lb)) for lb in range(n_lb)]
                 + [rowspec, rowspec, rowspec, matspec, matspec, matspec, matspec,
                    pl.BlockSpec((None, n_lb, 1, LANE), lambda b, j: (layer, j, 0, 0))],
        out_specs=pl.BlockSpec((n_lb, seq, LANE), lambda b, j: (j, b, 0)),
        out_shape=jax.ShapeDtypeStruct((S5_WIDTH // LANE, batch * seq, LANE), F32),
        scratch_shapes=[pltpu.VMEM((n_chain, S5_TILE // HALO, LANE), F32)] * 2
                       + [pltpu.VMEM((n_chain, HALO, LANE), F32)] * 2
                       + [pltpu.VMEM((n_chain, 4, LANE, LANE), BF16)],
        compiler_params=pltpu.CompilerParams(
            dimension_semantics=("arbitrary", "arbitrary"), vmem_limit_bytes=VMEM_LIMIT),
        name="s5_scan",
    )(*([proj] * n_lb), p["lam_re"], p["lam_im"], p["log_dt"], p["b_re"], p["b_im"], p["c_re"], p["c_im"], p["d"])


def _s5_glu_kernel(y_ref, w_ref, b_ref, z_ref, o_ref):
    y = jnp.concatenate([y_ref[j] for j in range(S5_WIDTH // LANE)], axis=1)
    gate = _sigmoid(_mm(y, w_ref[...]) + b_ref[...])
    o_ref[...] = (y * gate * _silu(z_ref[...])).astype(o_ref.dtype)


def _s5_glu(yc, proj, glu_w, glu_b, layer, tm=512):
    t = yc.shape[1]
    tm = min(tm, t)
    return pl.pallas_call(
        _s5_glu_kernel,
        grid=(t // tm,),
        in_specs=[pl.BlockSpec((S5_WIDTH // LANE, tm, LANE), lambda i: (0, i, 0)),
                  pl.BlockSpec((None, S5_WIDTH, S5_WIDTH), lambda i: (layer, 0, 0)),
                  pl.BlockSpec((None, 1, S5_WIDTH), lambda i: (layer, 0, 0)),
                  pl.BlockSpec((tm, S5_WIDTH), lambda i: (i, COL_SZ * LANE // S5_WIDTH))],
        out_specs=pl.BlockSpec((tm, S5_WIDTH), lambda i: (i, 0)),
        out_shape=jax.ShapeDtypeStruct((t, S5_WIDTH), BF16),
        compiler_params=pltpu.CompilerParams(
            dimension_semantics=("arbitrary",), vmem_limit_bytes=VMEM_LIMIT),
        name="s5_glu",
    )(yc, glu_w, glu_b, proj)


def _merge_kernel(oa_ref, ob_ref, oc_ref, wb_ref, ga_ref, gb_ref, gc_ref, gbias_ref, o_ref):
    acc = None
    for i, (o_r, g_r) in enumerate(((oa_ref, ga_ref), (ob_ref, gb_ref), (oc_ref, gc_ref))):
        proj = jnp.dot(o_r[...], wb_ref[i], preferred_element_type=F32)
        term = _sigmoid(g_r[...] + gbias_ref[i]) * proj
        acc = term if acc is None else acc + term
    o_ref[...] = acc.astype(o_ref.dtype)


def _merge(oa, ob, oc, proj, w_branch, gate_b, layer, tm=1024, tn=512):
    t = oa.shape[0]
    tm = min(tm, t)
    gbase = COL_GATE * LANE // tn
    per = D_MODEL // tn
    ospec = pl.BlockSpec((tm, 1024), lambda i, j: (i, 0))
    gspec = lambda br: pl.BlockSpec((tm, tn), lambda i, j, br=br: (i, gbase + br * per + j))
    return pl.pallas_call(
        _merge_kernel,
        grid=(t // tm, per),
        in_specs=[ospec, ospec, ospec,
                  pl.BlockSpec((None, N_BRANCH, 1024, tn), lambda i, j: (layer, 0, 0, j)),
                  gspec(0), gspec(1), gspec(2),
                  pl.BlockSpec((None, N_BRANCH, 1, tn), lambda i, j: (layer, 0, 0, j))],
        out_specs=pl.BlockSpec((tm, tn), lambda i, j: (i, j)),
        out_shape=jax.ShapeDtypeStruct((t, D_MODEL), BF16),
        compiler_params=pltpu.CompilerParams(
            dimension_semantics=("arbitrary", "arbitrary"), vmem_limit_bytes=VMEM_LIMIT),
        name="merge",
    )(oa, ob, oc, w_branch, proj, proj, proj, gate_b)


def _outproj_kernel(m_ref, w_ref, x_ref, nw_ref, *o_refs):
    x = x_ref[...] + jnp.dot(m_ref[...], w_ref[...], preferred_element_type=F32)
    normed = _rms(x, nw_ref[...])
    if len(o_refs) == 2:
        o_refs[0][...] = x
    o_refs[-1][...] = normed.astype(o_refs[-1].dtype)


def _outproj(merged, w_out, x2, next_norm_rows, layer, last, tm=512):
    t = x2.shape[0]
    tm = min(tm, t)
    row_blk = pl.BlockSpec((tm, D_MODEL), lambda i: (i, 0))
    if last:
        out_specs, out_shape = row_blk, jax.ShapeDtypeStruct((t, D_MODEL), F32)
    else:
        out_specs = (row_blk, row_blk)
        out_shape = (jax.ShapeDtypeStruct((t, D_MODEL), F32), jax.ShapeDtypeStruct((t, D_MODEL), BF16))
    return pl.pallas_call(
        _outproj_kernel,
        grid=(t // tm,),
        in_specs=[row_blk,
                  pl.BlockSpec((None, D_MODEL, D_MODEL), lambda i: (layer, 0, 0)),
                  row_blk,
                  pl.BlockSpec((None, 1, D_MODEL), lambda i: (layer, 0, 0))],
        out_specs=out_specs,
        out_shape=out_shape,
        compiler_params=pltpu.CompilerParams(
            dimension_semantics=("arbitrary",), vmem_limit_bytes=VMEM_LIMIT),
        name="outproj",
    )(merged, w_out, x2, next_norm_rows)


def _pad_cols(a, width):
    return jnp.pad(a, [(0, 0)] * (a.ndim - 1) + [(0, width - a.shape[-1])])


def _pair_s5(a, rows_are_channels):
    n_blk, n_pair = S5_WIDTH // LANE, LANE // (2 * S5_GROUP)
    cs = a if rows_are_channels else jnp.swapaxes(a, 1, 2)
    cs = cs.reshape(n_blk, n_pair, 2, S5_GROUP, S5_STATE)
    same = jnp.eye(2, dtype=bool)[None, None, :, None, :, None]
    paired = jnp.where(same, cs[:, :, :, :, None, :], 0.0)
    return paired.reshape(n_blk, n_pair, 2 * S5_GROUP, 2 * S5_STATE)


def _s5_rows(a):
    return a.reshape(S5_WIDTH // LANE, LANE // (2 * S5_GROUP), 2 * S5_STATE)


def _rows(a):
    return a.reshape(a.shape[0], 1, -1)


def _gdn_head_rows(a):
    return _pad_cols(_rows(jnp.pad(a, ((0, 0), (GDN_HEADS, 0)))), LANE)


def _rwkv_params(mu, w0, w_up, a0, a_up, k_k, k_a, r_k, lnx_w, lnx_b):
    lora = lambda a: jnp.pad(a, ((0, 0), (0, LANE - RWKV_LORA), (0, 0))).astype(BF16)
    w3 = 3 * RWKV_WIDTH
    return dict(mu_r=_rows(mu[:, :RWKV_WIDTH]), mu_k=_rows(mu[:, RWKV_WIDTH:2 * RWKV_WIDTH]),
                mu_v=_rows(mu[:, 2 * RWKV_WIDTH:w3]),
                mu_wl=_pad_cols(_rows(mu[:, w3:w3 + RWKV_LORA]), LANE),
                mu_al=_pad_cols(_rows(mu[:, w3 + RWKV_LORA:]), LANE),
                w0=_rows(w0), w_up=lora(w_up), a0=_rows(a0), a_up=lora(a_up), k_k=_rows(k_k), k_a=_rows(k_a),
                r_k=_rows(r_k), lnx_w=_rows(lnx_w), lnx_b=_rows(lnx_b))


def _s5_params(a_re, a_im, log_dt, b_re, b_im, c_re, c_im, d):
    depth = a_re.shape[0]
    rows = jax.vmap(_s5_rows)
    place = lambda a, t: jax.vmap(lambda m: _pair_s5(m, t))(a)
    return dict(lam_re=rows(a_re), lam_im=rows(a_im),
                log_dt=rows(jnp.broadcast_to(log_dt[:, :, None], (depth, S5_WIDTH // S5_GROUP, S5_STATE))),
                b_re=place(b_re, False), b_im=place(b_im, False),
                c_re=place(c_re, True), c_im=place(c_im, True),
                d=d.reshape(depth, S5_WIDTH // LANE, 1, LANE))


def kernel(x, norm_w, w_in, gdn_conv_w, gdn_a_log, gdn_dt_bias, gdn_norm_w, rwkv_mu, rwkv_w0, rwkv_w_up,
           rwkv_a0, rwkv_a_up, rwkv_k_k, rwkv_k_a, rwkv_r_k, rwkv_lnx_w, rwkv_lnx_b, s5_a_re, s5_a_im,
           s5_log_dt, s5_b_re, s5_b_im, s5_c_re, s5_c_im, s5_d, s5_glu_w, s5_glu_b, gate_b, w_branch,
           w_out, final_norm_w):
    batch, seq, _ = x.shape
    depth = w_in.shape[0]
    x2 = x.reshape(batch * seq, D_MODEL)
    h = _rmsnorm(x2, norm_w[0].reshape(1, D_MODEL))

    gdn_alog, gdn_dtb, gdn_nw = _gdn_head_rows(gdn_a_log), _gdn_head_rows(gdn_dt_bias), _rows(gdn_norm_w)
    rp = _rwkv_params(rwkv_mu, rwkv_w0, rwkv_w_up, rwkv_a0, rwkv_a_up, rwkv_k_k, rwkv_k_a, rwkv_r_k,
                      rwkv_lnx_w, rwkv_lnx_b)
    sp = _s5_params(s5_a_re, s5_a_im, s5_log_dt, s5_b_re, s5_b_im, s5_c_re, s5_c_im, s5_d)
    glu_w, glu_b = s5_glu_w.astype(BF16), _rows(s5_glu_b)
    wb, gb = w_branch.astype(BF16), gate_b.reshape(depth, N_BRANCH, 1, D_MODEL)
    wo = w_out.astype(BF16)
    next_norm = _rows(jnp.concatenate([norm_w[1:], final_norm_w[None, :]], axis=0))
    w_in_t = jnp.swapaxes(w_in, 1, 2)

    for i in range(depth):
        proj = _inproj_t(h, w_in_t, i)
        o_a, o_b = _gdn_rwkv(proj, (gdn_conv_w, gdn_alog, gdn_dtb, gdn_nw), rp, i, batch, seq)
        o_c = _s5_glu(_s5(proj, sp, i, batch, seq), proj, glu_w, glu_b, i)
        merged = _merge(o_a, o_b, o_c, proj, wb, gb, i)
        if i == depth - 1:
            return _outproj(merged, wo, x2, next_norm, i, True).reshape(batch, seq, D_MODEL)
        x2, h = _outproj(merged, wo, x2, next_norm, i, False)
```

```python
import functools
import math

import jax
import jax.numpy as jnp
from jax import lax
from jax.experimental import pallas as pl
from jax.experimental.pallas import tpu as pltpu

F32 = jnp.float32
BF16 = jnp.bfloat16

D_MODEL = 2048
GDN_HEADS = 8
GDN_HEAD_DIM = 128
GDN_WIDTH = 1024
RWKV_HEAD_DIM = 64
RWKV_WIDTH = 1024
RWKV_LORA = 96
RWKV_LN_EPS = 64e-5
S5_GROUP = 16
S5_STATE = 64
S5_WIDTH = 1024
N_BRANCH = 3
NORM_EPS = 1e-6

LANE = 128
MXU_TILE = 256
HALO = 8
CHUNK = 64
SEQ_BLOCK = 512
S5_TILE = 256
VMEM_LIMIT = 52 * 1024 * 1024

COL_QKV = 0
COL_GZ = 24
COL_RKV = 32
COL_RZ = 56
COL_SU = 64
COL_SZ = 72
COL_GATE = 80
COL_GBA = 128
COL_WL = 129
COL_AL = 130
N_COLBLK = 132
NP = N_COLBLK * LANE


def _sigmoid(x):
    return 1.0 / (1.0 + jnp.exp(-x))


def _silu(x):
    return x * _sigmoid(x)


def _softplus(x):
    return jnp.maximum(x, 0.0) + jnp.log1p(jnp.exp(-jnp.abs(x)))


def _mm(a, b):
    return jnp.dot(a.astype(BF16), b.astype(BF16), preferred_element_type=F32)


def _mm_tn(a, b):
    return lax.dot_general(a.astype(BF16), b.astype(BF16), (((0,), (0,)), ((), ())),
                           preferred_element_type=F32)


def _split_bf16(x, parts):
    out = []
    for _ in range(parts - 1):
        hi = x.astype(BF16)
        out.append(hi)
        x = x - hi.astype(F32)
    out.append(x.astype(BF16))
    return out


def _mm_sel_lhs(sel, x, parts=2):
    cols = x.shape[1]
    res = jnp.dot(sel.astype(BF16), jnp.concatenate(_split_bf16(x, parts), axis=1), preferred_element_type=F32)
    out = res[:, :cols]
    for i in range(1, parts):
        out = out + res[:, i * cols:(i + 1) * cols]
    return out


def _mm_sel_rhs(x, sel, parts=2):
    rows = x.shape[0]
    res = jnp.dot(jnp.concatenate(_split_bf16(x, parts), axis=0), sel.astype(BF16), preferred_element_type=F32)
    out = res[:rows, :]
    for i in range(1, parts):
        out = out + res[i * rows:(i + 1) * rows, :]
    return out


def _unit_lower_inverse(x, eye, size, stack, out):
    rows = x.shape[0]
    y = x
    q = eye + x
    y = jnp.dot(x.astype(BF16), stack(x.astype(BF16)), preferred_element_type=F32)
    yield
    span = 4
    while span < size:
        res = jnp.dot(jnp.concatenate([y, q], axis=0).astype(BF16), stack(y.astype(BF16)),
                      preferred_element_type=F32)
        yield
        y, q = res[:rows, :], q + res[rows:, :]
        span *= 2
    q = q + jnp.dot(q.astype(BF16), stack(y.astype(BF16)), preferred_element_type=F32)
    yield
    out.append(q)


def _run_lockstep(chains):
    live = list(chains)
    while live:
        for g in list(live):
            try:
                next(g)
            except StopIteration:
                live.remove(g)


def _iota2(shape, dim):
    return lax.broadcasted_iota(jnp.int32, shape, dim)


def _chunk_and_shifts(ref, halo_ref, c, rows, shift):
    r0 = pl.multiple_of(c * rows, rows)
    h0 = pl.multiple_of(jnp.maximum(r0 - HALO, 0), HALO)
    halo = jnp.where(c == 0, halo_ref[...], ref[pl.ds(h0, HALO), :])
    cur = ref[pl.ds(r0, rows), :]
    xc = jnp.concatenate([halo, cur], axis=0)
    return cur, [pltpu.roll(xc, s, 0)[HALO:, :] for s in range(1, shift + 1)]


def _recurrence_grid_setup(state_ref, halo_refs):
    @pl.when(pl.program_id(2) == 0)
    def _():
        state_ref[...] = jnp.zeros_like(state_ref)
        for h in halo_refs:
            h[...] = jnp.zeros_like(h)


def _save_halos(pairs, rows):
    for src, dst in pairs:
        dst[...] = src[rows - HALO:rows, :]


def _rms(x, w_row):
    return x * lax.rsqrt(jnp.mean(x * x, axis=-1, keepdims=True) + NORM_EPS) * w_row


def _rmsnorm_kernel(x_ref, nw_ref, o_ref):
    o_ref[...] = _rms(x_ref[...], nw_ref[...]).astype(o_ref.dtype)


def _rmsnorm(x2, norm_w_row, tm=512):
    t = x2.shape[0]
    tm = min(tm, t)
    return pl.pallas_call(
        _rmsnorm_kernel,
        grid=(t // tm,),
        in_specs=[pl.BlockSpec((tm, D_MODEL), lambda i: (i, 0)),
                  pl.BlockSpec((1, D_MODEL), lambda i: (0, 0))],
        out_specs=pl.BlockSpec((tm, D_MODEL), lambda i: (i, 0)),
        out_shape=jax.ShapeDtypeStruct((t, D_MODEL), BF16),
        compiler_params=pltpu.CompilerParams(dimension_semantics=("arbitrary",), vmem_limit_bytes=VMEM_LIMIT),
        name="rmsnorm",
    )(x2, norm_w_row)


_W_SRC_GBA = 4096
_W_SRC_RKV = _W_SRC_GBA + 2 * GDN_HEADS
_W_SRC_WL = _W_SRC_RKV + 3 * RWKV_WIDTH
_W_SRC_RZ = _W_SRC_WL + 2 * RWKV_LORA
_W_BLK = 512


def _inproj_t_kernel(h_ref, main_ref, gba_ref, wl_ref, al_ref, o_ref):
    tail = COL_GBA * LANE // _W_BLK
    nt = (((1,), (1,)), ((), ()))

    @pl.when(pl.program_id(1) < tail)
    def _():
        o_ref[...] = lax.dot_general(h_ref[...], main_ref[0].astype(BF16), nt, preferred_element_type=F32)

    @pl.when(pl.program_id(1) == tail)
    def _():
        k = main_ref.shape[2]
        zeros = lambda n: jnp.zeros((n, k), BF16)
        rows = jnp.concatenate([gba_ref[0].astype(BF16), zeros(LANE - 2 * GDN_HEADS),
                                wl_ref[0].astype(BF16), zeros(LANE - RWKV_LORA),
                                al_ref[0].astype(BF16), zeros(LANE - RWKV_LORA), zeros(LANE)], axis=0)
        o_ref[...] = lax.dot_general(h_ref[...], rows, nt, preferred_element_type=F32)


def _inproj_t(h, w_t, layer, tm=2048):
    t, k = h.shape
    tm = min(tm, t)
    first_rkv, first_rz, tail = COL_RKV * LANE // _W_BLK, COL_RZ * LANE // _W_BLK, COL_GBA * LANE // _W_BLK
    unit = 2 * GDN_HEADS

    def start(j):
        shift = jnp.where(j < first_rkv, 0, jnp.where(j < first_rz, (_W_SRC_RKV - COL_RKV * LANE) // unit,
                                                       (_W_SRC_RZ - COL_RZ * LANE) // unit))
        return jnp.where(j < tail, j * (_W_BLK // unit) + shift, 0) * unit

    rows_at = lambda n, row: pl.BlockSpec((pl.Element(1), pl.Element(n), pl.Element(k)),
                                          lambda i, j: (layer, row(j), 0))
    fixed = lambda n, row: rows_at(n, lambda j: row)
    return pl.pallas_call(
        _inproj_t_kernel,
        grid=(t // tm, NP // _W_BLK),
        in_specs=[pl.BlockSpec((tm, k), lambda i, j: (i, 0)),
                  rows_at(_W_BLK, start),
                  fixed(2 * GDN_HEADS, _W_SRC_GBA), fixed(RWKV_LORA, _W_SRC_WL),
                  fixed(RWKV_LORA, _W_SRC_WL + RWKV_LORA)],
        out_specs=pl.BlockSpec((tm, _W_BLK), lambda i, j: (i, j)),
        out_shape=jax.ShapeDtypeStruct((t, NP), F32),
        compiler_params=pltpu.CompilerParams(
            dimension_semantics=("arbitrary", "arbitrary"), vmem_limit_bytes=VMEM_LIMIT),
        name="inproj",
    )(h, w_t, w_t, w_t, w_t)


def _gdn_parts(q_ref, k_ref, v_ref, z_ref, ba_ref, cq_ref, ck_ref, cv_ref,
               alog_ref, dtb_ref, nw_ref, o_ref, s_ref, hq_ref, hk_ref, hv_ref, *, blk_rows, n_tile):
    c_len = CHUNK
    rows = 2 * c_len
    width = MXU_TILE
    tile0 = pl.program_id(1) * n_tile

    lane_w = _iota2((1, width), 1)
    lane_b = _iota2((1, LANE), 1)
    t_w = _iota2((c_len, rows), 0)
    s_w = _iota2((c_len, rows), 1) % c_len
    strict_w = s_w < t_w
    incl_w = s_w <= t_w
    eye_w = jnp.where(s_w == t_w, 1.0, 0.0)
    triu_w = jnp.where(t_w <= s_w, 1.0, 0.0)
    ti = _iota2((c_len, c_len), 0)
    tj = _iota2((c_len, c_len), 1)
    tril_t = jnp.where(tj <= ti, 1.0, 0.0)
    ones_t = jnp.ones((c_len, c_len), F32)
    same_head = (_iota2((width, width), 0) // GDN_HEAD_DIM) == (_iota2((width, width), 1) // GDN_HEAD_DIM)
    head0_lane = lane_w < GDN_HEAD_DIM
    head0_wide = _iota2((1, rows), 1) < c_len
    zero_b = jnp.zeros((), BF16)

    _recurrence_grid_setup(s_ref, (hq_ref, hk_ref, hv_ref))
    alog_row = alog_ref[...]
    dtb_row = dtb_ref[...]
    nw_row = nw_ref[...]

    def stack(xb):
        return jnp.concatenate([jnp.where(head0_lane, xb, zero_b), jnp.where(head0_lane, zero_b, xb)], axis=0)

    def stack_w(xb):
        return jnp.concatenate([jnp.where(head0_wide, xb, zero_b), jnp.where(head0_wide, zero_b, xb)], axis=0)

    def per_head(x, fn):
        return jnp.concatenate([fn(x[:, :GDN_HEAD_DIM]), fn(x[:, GDN_HEAD_DIM:])], axis=1)

    def l2n(xh):
        return xh * lax.rsqrt(jnp.sum(xh * xh, axis=-1, keepdims=True) + 1e-6)

    def rms(oh):
        return oh * lax.rsqrt(jnp.mean(oh * oh, axis=-1, keepdims=True) + NORM_EPS) * nw_row

    def chains(c):
        r0 = pl.multiple_of(c * c_len, c_len)

        def conv(x_ref, h_ref, cw_ref):
            cur, sh = _chunk_and_shifts(x_ref, h_ref, c, c_len, 3)
            cw = cw_ref[...]
            acc = cur * cw[3:4, :]
            for s in range(1, 4):
                acc = acc + sh[s - 1] * cw[3 - s:4 - s, :]
            return _silu(acc)

        q_all = conv(q_ref, hq_ref, cq_ref)
        k_all = conv(k_ref, hk_ref, ck_ref)
        v_all = conv(v_ref, hv_ref, cv_ref)
        z_all = z_ref[pl.ds(r0, c_len), :]
        ba = ba_ref[pl.ds(r0, c_len), :]
        g_all = -jnp.exp(alog_row) * _softplus(ba + dtb_row)

        def chain(t):
            sl = slice(t * width, (t + 1) * width)
            q = per_head(q_all[:, sl], l2n) * (GDN_HEAD_DIM ** -0.5)
            k = per_head(k_all[:, sl], l2n)
            v = v_all[:, sl]
            betas, gs = [], []
            for h in range(2):
                hid = 2 * (tile0 + t) + h
                betas.append(_sigmoid(jnp.sum(jnp.where(lane_b == hid, ba, 0.0), axis=-1, keepdims=True)))
                gs.append(jnp.sum(jnp.where(lane_b == GDN_HEADS + hid, g_all, 0.0), axis=-1, keepdims=True))
            beta = jnp.where(head0_lane, betas[0], betas[1])
            g_n = jnp.where(head0_lane, gs[0], gs[1])
            g_w = jnp.where(head0_wide, gs[0], gs[1])

            gcol = _mm_sel_lhs(tril_t, g_n)
            grow = _mm_sel_lhs(ones_t, g_w * triu_w)
            yield
            gcol_w = jnp.where(head0_wide, gcol[:, :rows], gcol[:, GDN_HEAD_DIM:GDN_HEAD_DIM + rows])
            diff = gcol_w - grow
            d_strict = jnp.where(strict_w, jnp.exp(jnp.where(strict_w, diff, 0.0)), 0.0)
            d_incl = jnp.where(incl_w, jnp.exp(jnp.where(incl_w, diff, 0.0)), 0.0)
            egc = jnp.exp(gcol)
            gl_lane = gcol[c_len - 1:c_len, :]

            kb = k * beta
            prod = lax.dot_general(jnp.concatenate([kb, q], axis=0).astype(BF16), stack(k.astype(BF16)),
                                   (((1,), (1,)), ((), ())), preferred_element_type=F32)
            yield
            lower = prod[:c_len, :] * d_strict
            attn = prod[c_len:, :] * d_incl
            inv = []
            yield from _unit_lower_inverse(-lower, eye_w, c_len, stack_w, inv)
            rhs = jnp.concatenate([stack((v * beta).astype(BF16)), stack((kb * egc).astype(BF16))], axis=1)
            uw = jnp.dot(inv[0].astype(BF16), rhs, preferred_element_type=F32)
            yield
            u, w = uw[:, :width], uw[:, width:]

            state = s_ref[t]
            x0 = _mm(jnp.concatenate([w, q * egc], axis=0), state)
            yield
            v_new = u - x0[:c_len, :]
            o = x0[c_len:, :] + jnp.dot(attn.astype(BF16), stack(v_new.astype(BF16)), preferred_element_type=F32)
            ds = _mm_tn(k * jnp.exp(gl_lane - gcol), v_new)
            s_ref[t] = state * jnp.exp(gl_lane) + jnp.where(same_head, ds, 0.0)
            yield

            o_ref[pl.ds(r0, c_len), sl] = (per_head(o, rms) * _silu(z_all[:, sl])).astype(o_ref.dtype)

        return [chain(t) for t in range(n_tile)]

    def finish():
        _save_halos(((q_ref, hq_ref), (k_ref, hk_ref), (v_ref, hv_ref)), blk_rows)

    return chains, finish


def _gdn_specs(proj, conv_w, alog_row, dtb_row, nw_row, layer, blk, nsb):
    w = GDN_WIDTH
    colspec = lambda base: pl.BlockSpec((blk, w), lambda b, g, s, base=base: (b * nsb + s, base))
    cwspec = lambda base: pl.BlockSpec((None, 4, w), lambda b, g, s, base=base: (layer, 0, base))
    rowspec = pl.BlockSpec((None, 1, LANE), lambda b, g, s: (layer, 0, 0))
    in_specs = [colspec(0), colspec(1), colspec(2), colspec(COL_GZ * LANE // GDN_WIDTH),
                pl.BlockSpec((blk, LANE), lambda b, g, s: (b * nsb + s, COL_GBA)),
                cwspec(0), cwspec(1), cwspec(2), rowspec, rowspec, rowspec]
    scratch = [pltpu.VMEM((w // MXU_TILE, MXU_TILE, MXU_TILE), F32)] + [pltpu.VMEM((HALO, w), F32)] * 3
    return (proj, proj, proj, proj, proj, conv_w, conv_w, conv_w, alog_row, dtb_row, nw_row), in_specs, scratch


def _rwkv_parts(r_ref, k_ref, v_ref, wl_ref, al_ref, z_ref,
                mur_ref, muk_ref, muv_ref, muwl_ref, mual_ref,
                w0_ref, wup_ref, a0_ref, aup_ref, kk_ref, ka_ref, rk_ref, lnw_ref, lnb_ref,
                o_ref, n_ref, hr_ref, hk_ref, hv_ref, hwl_ref, hal_ref, *, blk_rows, n_tile):
    c_len = CHUNK
    nh = MXU_TILE // RWKV_HEAD_DIM
    width = MXU_TILE
    assert c_len == RWKV_HEAD_DIM

    same_head = (_iota2((width, width), 0) // RWKV_HEAD_DIM) == (_iota2((width, width), 1) // RWKV_HEAD_DIM)
    head_sum = jnp.where(same_head, 1.0, 0.0)
    t_w = _iota2((c_len, width), 0)
    s_w = _iota2((c_len, width), 1) % c_len
    strict_w = s_w < t_w
    incl_w = s_w <= t_w
    eye_w = jnp.where(s_w == t_w, 1.0, 0.0)
    ti = _iota2((c_len, c_len), 0)
    tj = _iota2((c_len, c_len), 1)
    tril_t = jnp.where(tj <= ti, 1.0, 0.0)
    lane_head = _iota2((1, width), 1) // RWKV_HEAD_DIM
    zero_b = jnp.zeros((), BF16)

    _recurrence_grid_setup(n_ref, (hr_ref, hk_ref, hv_ref, hwl_ref, hal_ref))

    def stack(xb):
        return jnp.concatenate([jnp.where(lane_head == h, xb, zero_b) for h in range(nh)], axis=0)

    def chains(c):
        r0 = pl.multiple_of(c * c_len, c_len)

        def shifted(x_ref, h_ref, mu_ref):
            cur, sh = _chunk_and_shifts(x_ref, h_ref, c, c_len, 1)
            return cur + (sh[0] - cur) * mu_ref[...]

        r_all = shifted(r_ref, hr_ref, mur_ref)
        k_all = shifted(k_ref, hk_ref, muk_ref)
        v_all = shifted(v_ref, hv_ref, muv_ref)
        wl = shifted(wl_ref, hwl_ref, muwl_ref)
        al = shifted(al_ref, hal_ref, mual_ref)
        z_all = z_ref[pl.ds(r0, c_len), :]

        w_pre = w0_ref[...] + _mm(jnp.tanh(wl), wup_ref[...])
        logw_all = -jnp.exp(-_softplus(-w_pre) - 0.5)
        a_all = _sigmoid(a0_ref[...] + _mm(al, aup_ref[...]))
        kkx_all = k_all * kk_ref[...]
        k2_all = k_all * (1.0 + (a_all - 1.0) * ka_ref[...])
        lc_all = _mm_sel_lhs(tril_t, logw_all)
        bonus_all = r_all * k2_all * rk_ref[...]

        def chain(t):
            sl = slice(t * width, (t + 1) * width)
            r, v, a, k2, logw, lc = r_all[:, sl], v_all[:, sl], a_all[:, sl], k2_all[:, sl], logw_all[:, sl], lc_all[:, sl]
            kkx = kkx_all[:, sl]
            sums = _mm_sel_rhs(jnp.concatenate([kkx * kkx, bonus_all[:, sl]], axis=0), head_sum)
            yield
            kk = kkx * lax.rsqrt(sums[:c_len, :] + 1e-6)
            bonus = sums[c_len:, :]

            lc_last = lc[c_len - 1:c_len, :]
            e_neg = jnp.exp(-lc)
            e_rem = jnp.exp(lc_last - lc)
            kka = kk * a
            a_n = -kk * jnp.exp(lc - logw)
            r_n = r * jnp.exp(lc)
            sv = stack(v.astype(BF16))

            prod = lax.dot_general(
                jnp.concatenate([a_n, r_n], axis=0).astype(BF16),
                jnp.concatenate([stack((kka * e_neg).astype(BF16)), stack((k2 * e_neg).astype(BF16))], axis=0),
                (((1,), (1,)), ((), ())), preferred_element_type=F32)
            yield
            a_ab = jnp.where(strict_w, prod[:c_len, :width], 0.0)
            a_ak = jnp.where(strict_w, prod[:c_len, width:], 0.0)
            a_rb = jnp.where(incl_w, prod[c_len:, :width], 0.0)
            a_rk = jnp.where(incl_w, prod[c_len:, width:], 0.0)

            inv = []
            yield from _unit_lower_inverse(a_ab, eye_w, c_len, stack, inv)
            t_mat = inv[0].astype(BF16)
            w_p = jnp.dot(t_mat, stack(a_n.astype(BF16)), preferred_element_type=F32)
            aks = jnp.dot(a_ak.astype(BF16), sv, preferred_element_type=F32)
            yield
            v_p = jnp.dot(t_mat, stack(aks.astype(BF16)), preferred_element_type=F32)
            yield

            state = n_ref[t]
            x0 = _mm(jnp.concatenate([w_p, r_n], axis=0), state)
            yield
            u = x0[:c_len, :] + v_p
            y = x0[c_len:, :] + jnp.dot(jnp.concatenate([a_rb, a_rk], axis=1).astype(BF16),
                                        jnp.concatenate([stack(u.astype(BF16)), sv], axis=0),
                                        preferred_element_type=F32)
            dn = _mm_tn(jnp.concatenate([kka * e_rem, k2 * e_rem], axis=0), jnp.concatenate([u, v], axis=0))
            gam_col = jnp.transpose(jnp.broadcast_to(jnp.exp(lc_last), (width, width)))
            n_ref[t] = gam_col * state + jnp.where(same_head, dn, 0.0)
            yield

            inv_n = 1.0 / RWKV_HEAD_DIM
            mean = _mm_sel_rhs(y, head_sum) * inv_n
            yield
            dlt = y - mean
            var = _mm_sel_rhs(dlt * dlt, head_sum) * inv_n
            yield
            y = dlt * lax.rsqrt(var + RWKV_LN_EPS) * lnw_ref[:, sl] + lnb_ref[:, sl]
            y = y + bonus * v
            o_ref[pl.ds(r0, c_len), sl] = (y * _silu(z_all[:, sl])).astype(o_ref.dtype)

        return [chain(t) for t in range(n_tile)]

    def finish():
        _save_halos(((r_ref, hr_ref), (k_ref, hk_ref), (v_ref, hv_ref), (wl_ref, hwl_ref), (al_ref, hal_ref)),
                    blk_rows)

    return chains, finish


def _rwkv_specs(proj, p, layer, blk, nsb):
    w = RWKV_WIDTH
    colspec = lambda base: pl.BlockSpec((blk, w), lambda b, g, s, base=base: (b * nsb + s, base))
    lspec = lambda col: pl.BlockSpec((blk, LANE), lambda b, g, s, col=col: (b * nsb + s, col))
    grow = pl.BlockSpec((None, 1, w), lambda b, g, s: (layer, 0, 0))
    lrow = pl.BlockSpec((None, 1, LANE), lambda b, g, s: (layer, 0, 0))
    upspec = pl.BlockSpec((None, LANE, w), lambda b, g, s: (layer, 0, 0))
    base = COL_RKV * LANE // RWKV_WIDTH
    in_specs = [colspec(base), colspec(base + 1), colspec(base + 2), lspec(COL_WL), lspec(COL_AL),
                colspec(COL_RZ * LANE // RWKV_WIDTH),
                grow, grow, grow, lrow, lrow,
                grow, upspec, grow, upspec, grow, grow, grow, grow, grow]
    scratch = ([pltpu.VMEM((w // MXU_TILE, MXU_TILE, MXU_TILE), F32)] + [pltpu.VMEM((HALO, w), F32)] * 3
               + [pltpu.VMEM((HALO, LANE), F32)] * 2)
    operands = (proj, proj, proj, proj, proj, proj,
                p["mu_r"], p["mu_k"], p["mu_v"], p["mu_wl"], p["mu_al"],
                p["w0"], p["w_up"], p["a0"], p["a_up"], p["k_k"], p["k_a"], p["r_k"], p["lnx_w"], p["lnx_b"])
    return operands, in_specs, scratch


def _mixers_kernel(*refs, n_gdn, n_rwkv, n_gdn_scr, blk_rows):
    gdn_in, rwkv_in = refs[:n_gdn], refs[n_gdn:n_gdn + n_rwkv]
    o_gdn, o_rwkv = refs[n_gdn + n_rwkv:n_gdn + n_rwkv + 2]
    scratch = refs[n_gdn + n_rwkv + 2:]
    g_chains, g_finish = _gdn_parts(*gdn_in, o_gdn, *scratch[:n_gdn_scr], blk_rows=blk_rows,
                                    n_tile=GDN_WIDTH // MXU_TILE)
    r_chains, r_finish = _rwkv_parts(*rwkv_in, o_rwkv, *scratch[n_gdn_scr:], blk_rows=blk_rows,
                                     n_tile=RWKV_WIDTH // MXU_TILE)

    def body(c, carry):
        _run_lockstep(r_chains(c) + g_chains(c))
        return carry

    lax.fori_loop(0, blk_rows // CHUNK, body, 0)
    g_finish()
    r_finish()


def _gdn_rwkv(proj, gdn_args, rwkv_p, layer, batch, seq):
    blk = min(SEQ_BLOCK, seq)
    nsb = seq // blk
    g_ops, g_specs, g_scr = _gdn_specs(proj, *gdn_args, layer, blk, nsb)
    r_ops, r_specs, r_scr = _rwkv_specs(proj, rwkv_p, layer, blk, nsb)
    out_spec = pl.BlockSpec((blk, GDN_WIDTH), lambda b, g, s: (b * nsb + s, 0))
    out_shape = jax.ShapeDtypeStruct((batch * seq, GDN_WIDTH), BF16)
    return pl.pallas_call(
        functools.partial(_mixers_kernel, n_gdn=len(g_ops), n_rwkv=len(r_ops), n_gdn_scr=len(g_scr),
                          blk_rows=blk),
        grid=(batch, 1, nsb),
        in_specs=g_specs + r_specs,
        out_specs=(out_spec, out_spec),
        out_shape=(out_shape, out_shape),
        scratch_shapes=g_scr + r_scr,
        compiler_params=pltpu.CompilerParams(
            dimension_semantics=("arbitrary", "arbitrary", "arbitrary"), vmem_limit_bytes=VMEM_LIMIT),
        name="gdn_rwkv",
    )(*g_ops, *r_ops)


def _s5_kernel(*refs, seq, n_lb):
    u_refs = refs[:n_lb]
    (lre_ref, lim_ref, ldt_ref, bre_ref, bim_ref, cre_ref, cim_ref, d_ref,
     o_ref, pre_ref, pim_ref, qre_ref, qim_ref, w_ref) = refs[n_lb:]
    tile = S5_TILE
    n_pair = LANE // (2 * S5_GROUP)
    n_chain = n_lb * n_pair
    sub = HALO
    n_sub = tile // sub

    def cmul(ar, ai, br, bi):
        return ar * br - ai * bi, ar * bi + ai * br

    def block_scan(sr, si, ar, ai, out):
        in_blk = _iota2(sr.shape, 0) % sub
        d = 1
        while d < sub:
            keep = in_blk >= d
            tr, ti = cmul(ar, ai, jnp.where(keep, pltpu.roll(sr, d, 0), 0.0),
                          jnp.where(keep, pltpu.roll(si, d, 0), 0.0))
            sr, si = sr + tr, si + ti
            ar, ai = cmul(ar, ai, ar, ai)
            d *= 2
            yield
        out.extend((sr, si))

    first = _iota2((sub, LANE), 0) == 0
    ab = []
    for q in range(n_chain):
        lb, p = divmod(q, n_pair)
        lre = lre_ref[lb, p:p + 1, :]
        lim = lim_ref[lb, p:p + 1, :]
        dt = jnp.exp(ldt_ref[lb, p:p + 1, :])
        mag = jnp.exp(lre * dt)
        ab_re = mag * jnp.cos(lim * dt)
        ab_im = mag * jnp.sin(lim * dt)
        den = lre * lre + lim * lim
        coef_re = ((ab_re - 1.0) * lre + ab_im * lim) / den
        coef_im = (ab_im * lre - (ab_re - 1.0) * lim) / den
        b_re = bre_ref[lb, p]
        b_im = bim_ref[lb, p]

        def place(m, p=p):
            top, n = p * 2 * S5_GROUP, m.shape[0]
            parts = [jnp.zeros((top, LANE), BF16)] if top else []
            parts.append(m.astype(BF16))
            if LANE - top - n:
                parts.append(jnp.zeros((LANE - top - n, LANE), BF16))
            return jnp.concatenate(parts, axis=0)

        w_ref[q, 0] = place(coef_re * b_re - coef_im * b_im)
        w_ref[q, 1] = place(coef_re * b_im + coef_im * b_re)
        w_ref[q, 2] = place(cre_ref[lb, p])
        w_ref[q, 3] = place(cim_ref[lb, p])
        pr, pi = [ab_re], [ab_im]
        for _ in range(1, n_sub):
            nr, ni = cmul(pr[-1], pi[-1], ab_re, ab_im)
            pr.append(nr)
            pi.append(ni)
        pre_ref[q] = jnp.concatenate(pr, axis=0)
        pim_ref[q] = jnp.concatenate(pi, axis=0)
        imp_re = jnp.where(first, jnp.broadcast_to(pr[-1], (sub, LANE)), 0.0)
        imp_im = jnp.where(first, jnp.broadcast_to(pi[-1], (sub, LANE)), 0.0)
        pw = []
        for _ in block_scan(imp_re, imp_im, pr[-1], pi[-1], pw):
            pass
        qre_ref[q] = pw[0]
        qim_ref[q] = pw[1]
        ab.append((ab_re, ab_im, pr[-1], pi[-1]))

    def tile_body(i, st):
        t0 = pl.multiple_of(i * tile, tile)
        us = [jnp.concatenate([u_refs[lb][pl.ds(t0 + j, sub, stride=n_sub), :] for j in range(n_sub)], axis=0)
              for lb in range(n_lb)]
        ubs = [u.astype(BF16) for u in us]
        ys, new_st = [None] * n_chain, [None] * n_chain

        def chain(p):
            cr, ci = st[p]
            ar, ai, a_run_r, a_run_i = ab[p]
            ub = ubs[p // n_pair]
            sr = jnp.dot(ub, w_ref[p, 0], preferred_element_type=F32)
            si = jnp.dot(ub, w_ref[p, 1], preferred_element_type=F32)
            yield
            loc_r, loc_i = [sr[:sub, :]], [si[:sub, :]]
            for j in range(1, n_sub):
                tr, ti = cmul(ar, ai, loc_r[-1], loc_i[-1])
                loc_r.append(sr[j * sub:(j + 1) * sub, :] + tr)
                loc_i.append(si[j * sub:(j + 1) * sub, :] + ti)
                if j % 4 == 0:
                    yield
            res = []
            yield from block_scan(loc_r[-1], loc_i[-1], a_run_r, a_run_i, res)
            tr, ti = cmul(qre_ref[p], qim_ref[p], cr, ci)
            end_r, end_i = res[0] + tr, res[1] + ti
            new_st[p] = (end_r[sub - 1:sub, :], end_i[sub - 1:sub, :])
            in_r = jnp.where(first, cr, pltpu.roll(end_r, 1, 0))
            in_i = jnp.where(first, ci, pltpu.roll(end_i, 1, 0))
            out_r, out_i = [], []
            for j in range(n_sub):
                tr, ti = cmul(pre_ref[p, j:j + 1, :], pim_ref[p, j:j + 1, :], in_r, in_i)
                out_r.append(loc_r[j] + tr)
                out_i.append(loc_i[j] + ti)
            sr, si = jnp.concatenate(out_r, axis=0), jnp.concatenate(out_i, axis=0)
            nt = (((1,), (1,)), ((), ()))
            ys[p] = (lax.dot_general(sr.astype(BF16), w_ref[p, 2], nt, preferred_element_type=F32)
                     - lax.dot_general(si.astype(BF16), w_ref[p, 3], nt, preferred_element_type=F32))

        _run_lockstep([chain(p) for p in range(n_chain)])
        for lb in range(n_lb):
            y = d_ref[lb] * us[lb]
            for p in range(n_pair):
                y = y + ys[lb * n_pair + p]
            inner = math.sqrt(2.0 / math.pi) * (y + 0.044715 * (y * y * y))
            y = 0.5 * y * (1.0 + jnp.tanh(inner))
            for j in range(n_sub):
                o_ref[lb, pl.ds(t0 + j, sub, stride=n_sub), :] = y[j * sub:(j + 1) * sub, :]
        return tuple(new_st)

    zero = jnp.zeros((1, LANE), F32)
    lax.fori_loop(0, seq // tile, tile_body, tuple((zero, zero) for _ in range(n_chain)))


def _s5(proj, p, layer, batch, seq, n_lb=4):
    n_blk = S5_WIDTH // (n_lb * LANE)
    n_pair = LANE // (2 * S5_GROUP)
    n_chain = n_lb * n_pair
    rowspec = pl.BlockSpec((None, n_lb, n_pair, LANE), lambda b, j: (layer, j, 0, 0))
    matspec = pl.BlockSpec((None, n_lb, n_pair, 2 * S5_GROUP, LANE), lambda b, j: (layer, j, 0, 0, 0))
    return pl.pallas_call(
        functools.partial(_s5_kernel, seq=seq, n_lb=n_lb),
        grid=(batch, n_blk),
        in_specs=[pl.BlockSpec((seq, LANE), lambda b, j, lb=lb: (b, COL_SU + n_lb * j + lb)) for lb in range(n_lb)]
                 + [rowspec, rowspec, rowspec, matspec, matspec, matspec, matspec,
                    pl.BlockSpec((None, n_lb, 1, LANE), lambda b, j: (layer, j, 0, 0))],
        out_specs=pl.BlockSpec((n_lb, seq, LANE), lambda b, j: (j, b, 0)),
        out_shape=jax.ShapeDtypeStruct((S5_WIDTH // LANE, batch * seq, LANE), F32),
        scratch_shapes=[pltpu.VMEM((n_chain, S5_TILE // HALO, LANE), F32)] * 2
                       + [pltpu.VMEM((n_chain, HALO, LANE), F32)] * 2
                       + [pltpu.VMEM((n_chain, 4, LANE, LANE), BF16)],
        compiler_params=pltpu.CompilerParams(
            dimension_semantics=("arbitrary", "arbitrary"), vmem_limit_bytes=VMEM_LIMIT),
        name="s5_scan",
    )(*([proj] * n_lb), p["lam_re"], p["lam_im"], p["log_dt"], p["b_re"], p["b_im"], p["c_re"], p["c_im"], p["d"])


def _s5_glu_kernel(y_ref, w_ref, b_ref, z_ref, o_ref):
    y = jnp.concatenate([y_ref[j] for j in range(S5_WIDTH // LANE)], axis=1)
    gate = _sigmoid(_mm(y, w_ref[...]) + b_ref[...])
    o_ref[...] = (y * gate * _silu(z_ref[...])).astype(o_ref.dtype)


def _s5_glu(yc, proj, glu_w, glu_b, layer, tm=512):
    t = yc.shape[1]
    tm = min(tm, t)
    return pl.pallas_call(
        _s5_glu_kernel,
        grid=(t // tm,),
        in_specs=[pl.BlockSpec((S5_WIDTH // LANE, tm, LANE), lambda i: (0, i, 0)),
                  pl.BlockSpec((None, S5_WIDTH, S5_WIDTH), lambda i: (layer, 0, 0)),
                  pl.BlockSpec((None, 1, S5_WIDTH), lambda i: (layer, 0, 0)),
                  pl.BlockSpec((tm, S5_WIDTH), lambda i: (i, COL_SZ * LANE // S5_WIDTH))],
        out_specs=pl.BlockSpec((tm, S5_WIDTH), lambda i: (i, 0)),
        out_shape=jax.ShapeDtypeStruct((t, S5_WIDTH), BF16),
        compiler_params=pltpu.CompilerParams(
            dimension_semantics=("arbitrary",), vmem_limit_bytes=VMEM_LIMIT),
        name="s5_glu",
    )(yc, glu_w, glu_b, proj)


def _merge_kernel(oa_ref, ob_ref, oc_ref, wb_ref, ga_ref, gb_ref, gc_ref, gbias_ref, o_ref):
    acc = None
    for i, (o_r, g_r) in enumerate(((oa_ref, ga_ref), (ob_ref, gb_ref), (oc_ref, gc_ref))):
        proj = jnp.dot(o_r[...], wb_ref[i], preferred_element_type=F32)
        term = _sigmoid(g_r[...] + gbias_ref[i]) * proj
        acc = term if acc is None else acc + term
    o_ref[...] = acc.astype(o_ref.dtype)


def _merge(oa, ob, oc, proj, w_branch, gate_b, layer, tm=1024, tn=512):
    t = oa.shape[0]
    tm = min(tm, t)
    gbase = COL_GATE * LANE // tn
    per = D_MODEL // tn
    ospec = pl.BlockSpec((tm, 1024), lambda i, j: (i, 0))
    gspec = lambda br: pl.BlockSpec((tm, tn), lambda i, j, br=br: (i, gbase + br * per + j))
    return pl.pallas_call(
        _merge_kernel,
        grid=(t // tm, per),
        in_specs=[ospec, ospec, ospec,
                  pl.BlockSpec((None, N_BRANCH, 1024, tn), lambda i, j: (layer, 0, 0, j)),
                  gspec(0), gspec(1), gspec(2),
                  pl.BlockSpec((None, N_BRANCH, 1, tn), lambda i, j: (layer, 0, 0, j))],
        out_specs=pl.BlockSpec((tm, tn), lambda i, j: (i, j)),
        out_shape=jax.ShapeDtypeStruct((t, D_MODEL), BF16),
        compiler_params=pltpu.CompilerParams(
            dimension_semantics=("arbitrary", "arbitrary"), vmem_limit_bytes=VMEM_LIMIT),
        name="merge",
    )(oa, ob, oc, w_branch, proj, proj, proj, gate_b)


def _outproj_kernel(m_ref, w_ref, x_ref, nw_ref, *o_refs):
    x = x_ref[...] + jnp.dot(m_ref[...], w_ref[...], preferred_element_type=F32)
    normed = _rms(x, nw_ref[...])
    if len(o_refs) == 2:
        o_refs[0][...] = x
    o_refs[-1][...] = normed.astype(o_refs[-1].dtype)


def _outproj(merged, w_out, x2, next_norm_rows, layer, last, tm=512):
    t = x2.shape[0]
    tm = min(tm, t)
    row_blk = pl.BlockSpec((tm, D_MODEL), lambda i: (i, 0))
    if last:
        out_specs, out_shape = row_blk, jax.ShapeDtypeStruct((t, D_MODEL), F32)
    else:
        out_specs = (row_blk, row_blk)
        out_shape = (jax.ShapeDtypeStruct((t, D_MODEL), F32), jax.ShapeDtypeStruct((t, D_MODEL), BF16))
    return pl.pallas_call(
        _outproj_kernel,
        grid=(t // tm,),
        in_specs=[row_blk,
                  pl.BlockSpec((None, D_MODEL, D_MODEL), lambda i: (layer, 0, 0)),
                  row_blk,
                  pl.BlockSpec((None, 1, D_MODEL), lambda i: (layer, 0, 0))],
        out_specs=out_specs,
        out_shape=out_shape,
        compiler_params=pltpu.CompilerParams(
            dimension_semantics=("arbitrary",), vmem_limit_bytes=VMEM_LIMIT),
        name="outproj",
    )(merged, w_out, x2, next_norm_rows)


def _pad_cols(a, width):
    return jnp.pad(a, [(0, 0)] * (a.ndim - 1) + [(0, width - a.shape[-1])])


def _pair_s5(a, rows_are_channels):
    n_blk, n_pair = S5_WIDTH // LANE, LANE // (2 * S5_GROUP)
    cs = a if rows_are_channels else jnp.swapaxes(a, 1, 2)
    cs = cs.reshape(n_blk, n_pair, 2, S5_GROUP, S5_STATE)
    same = jnp.eye(2, dtype=bool)[None, None, :, None, :, None]
    paired = jnp.where(same, cs[:, :, :, :, None, :], 0.0)
    return paired.reshape(n_blk, n_pair, 2 * S5_GROUP, 2 * S5_STATE)


def _s5_rows(a):
    return a.reshape(S5_WIDTH // LANE, LANE // (2 * S5_GROUP), 2 * S5_STATE)


def _rows(a):
    return a.reshape(a.shape[0], 1, -1)


def _gdn_head_rows(a):
    return _pad_cols(_rows(jnp.pad(a, ((0, 0), (GDN_HEADS, 0)))), LANE)


def _rwkv_params(mu, w0, w_up, a0, a_up, k_k, k_a, r_k, lnx_w, lnx_b):
    lora = lambda a: jnp.pad(a, ((0, 0), (0, LANE - RWKV_LORA), (0, 0))).astype(BF16)
    w3 = 3 * RWKV_WIDTH
    return dict(mu_r=_rows(mu[:, :RWKV_WIDTH]), mu_k=_rows(mu[:, RWKV_WIDTH:2 * RWKV_WIDTH]),
                mu_v=_rows(mu[:, 2 * RWKV_WIDTH:w3]),
                mu_wl=_pad_cols(_rows(mu[:, w3:w3 + RWKV_LORA]), LANE),
                mu_al=_pad_cols(_rows(mu[:, w3 + RWKV_LORA:]), LANE),
                w0=_rows(w0), w_up=lora(w_up), a0=_rows(a0), a_up=lora(a_up), k_k=_rows(k_k), k_a=_rows(k_a),
                r_k=_rows(r_k), lnx_w=_rows(lnx_w), lnx_b=_rows(lnx_b))


def _s5_params(a_re, a_im, log_dt, b_re, b_im, c_re, c_im, d):
    depth = a_re.shape[0]
    rows = jax.vmap(_s5_rows)
    place = lambda a, t: jax.vmap(lambda m: _pair_s5(m, t))(a)
    return dict(lam_re=rows(a_re), lam_im=rows(a_im),
                log_dt=rows(jnp.broadcast_to(log_dt[:, :, None], (depth, S5_WIDTH // S5_GROUP, S5_STATE))),
                b_re=place(b_re, False), b_im=place(b_im, False),
                c_re=place(c_re, True), c_im=place(c_im, True),
                d=d.reshape(depth, S5_WIDTH // LANE, 1, LANE))


def kernel(x, norm_w, w_in, gdn_conv_w, gdn_a_log, gdn_dt_bias, gdn_norm_w, rwkv_mu, rwkv_w0, rwkv_w_up,
           rwkv_a0, rwkv_a_up, rwkv_k_k, rwkv_k_a, rwkv_r_k, rwkv_lnx_w, rwkv_lnx_b, s5_a_re, s5_a_im,
           s5_log_dt, s5_b_re, s5_b_im, s5_c_re, s5_c_im, s5_d, s5_glu_w, s5_glu_b, gate_b, w_branch,
           w_out, final_norm_w):
    batch, seq, _ = x.shape
    depth = w_in.shape[0]
    x2 = x.reshape(batch * seq, D_MODEL)
    h = _rmsnorm(x2, norm_w[0].reshape(1, D_MODEL))

    gdn_alog, gdn_dtb, gdn_nw = _gdn_head_rows(gdn_a_log), _gdn_head_rows(gdn_dt_bias), _rows(gdn_norm_w)
    rp = _rwkv_params(rwkv_mu, rwkv_w0, rwkv_w_up, rwkv_a0, rwkv_a_up, rwkv_k_k, rwkv_k_a, rwkv_r_k,
                      rwkv_lnx_w, rwkv_lnx_b)
    sp = _s5_params(s5_a_re, s5_a_im, s5_log_dt, s5_b_re, s5_b_im, s5_c_re, s5_c_im, s5_d)
    glu_w, glu_b = s5_glu_w.astype(BF16), _rows(s5_glu_b)
    wb, gb = w_branch.astype(BF16), gate_b.reshape(depth, N_BRANCH, 1, D_MODEL)
    wo = w_out.astype(BF16)
    next_norm = _rows(jnp.concatenate([norm_w[1:], final_norm_w[None, :]], axis=0))
    w_in_t = jnp.swapaxes(w_in, 1, 2)

    for i in range(depth):
        proj = _inproj_t(h, w_in_t, i)
        o_a, o_b = _gdn_rwkv(proj, (gdn_conv_w, gdn_alog, gdn_dtb, gdn_nw), rp, i, batch, seq)
        o_c = _s5_glu(_s5(proj, sp, i, batch, seq), proj, glu_w, glu_b, i)
        merged = _merge(o_a, o_b, o_c, proj, wb, gb, i)
        if i == depth - 1:
            return _outproj(merged, wo, x2, next_norm, i, True).reshape(batch, seq, D_MODEL)
        x2, h = _outproj(merged, wo, x2, next_norm, i, False)
```

```python
import functools
import math

import jax
import jax.numpy as jnp
from jax import lax
from jax.experimental import pallas as pl
from jax.experimental.pallas import tpu as pltpu

F32 = jnp.float32
BF16 = jnp.bfloat16

D_MODEL = 2048
GDN_HEADS = 8
GDN_HEAD_DIM = 128
GDN_WIDTH = 1024
RWKV_HEAD_DIM = 64
RWKV_WIDTH = 1024
RWKV_LORA = 96
RWKV_LN_EPS = 64e-5
S5_GROUP = 16
S5_STATE = 64
S5_WIDTH = 1024
N_BRANCH = 3
NORM_EPS = 1e-6

LANE = 128
MXU_TILE = 256
HALO = 8
CHUNK = 64
SEQ_BLOCK = 512
S5_TILE = 256
VMEM_LIMIT = 52 * 1024 * 1024

COL_QKV = 0
COL_GZ = 24
COL_RKV = 32
COL_RZ = 56
COL_SU = 64
COL_SZ = 72
COL_GATE = 80
COL_GBA = 128
COL_WL = 129
COL_AL = 130
N_COLBLK = 132
NP = N_COLBLK * LANE


def _sigmoid(x):
    return 1.0 / (1.0 + jnp.exp(-x))


def _silu(x):
    return x * _sigmoid(x)


def _softplus(x):
    return jnp.maximum(x, 0.0) + jnp.log1p(jnp.exp(-jnp.abs(x)))


def _mm(a, b):
    return jnp.dot(a.astype(BF16), b.astype(BF16), preferred_element_type=F32)


def _mm_nt(a, b):
    return lax.dot_general(a.astype(BF16), b.astype(BF16), (((1,), (1,)), ((), ())),
                           preferred_element_type=F32)


def _mm_tn(a, b):
    return lax.dot_general(a.astype(BF16), b.astype(BF16), (((0,), (0,)), ((), ())),
                           preferred_element_type=F32)


def _split_bf16(x, parts):
    out = []
    for _ in range(parts - 1):
        hi = x.astype(BF16)
        out.append(hi)
        x = x - hi.astype(F32)
    out.append(x.astype(BF16))
    return out


def _mm_sel_lhs(sel, x, parts=2):
    cols = x.shape[1]
    res = jnp.dot(sel.astype(BF16), jnp.concatenate(_split_bf16(x, parts), axis=1), preferred_element_type=F32)
    out = res[:, :cols]
    for i in range(1, parts):
        out = out + res[:, i * cols:(i + 1) * cols]
    return out


def _mm_sel_rhs(x, sel, parts=2):
    rows = x.shape[0]
    res = jnp.dot(jnp.concatenate(_split_bf16(x, parts), axis=0), sel.astype(BF16), preferred_element_type=F32)
    out = res[:rows, :]
    for i in range(1, parts):
        out = out + res[i * rows:(i + 1) * rows, :]
    return out


def _unit_lower_inverse(x, eye, size, stack, out):
    rows = x.shape[0]
    y = x
    q = eye + x
    y = jnp.dot(x.astype(BF16), stack(x.astype(BF16)), preferred_element_type=F32)
    yield
    span = 4
    while span < size:
        res = jnp.dot(jnp.concatenate([y, q], axis=0).astype(BF16), stack(y.astype(BF16)),
                      preferred_element_type=F32)
        yield
        y, q = res[:rows, :], q + res[rows:, :]
        span *= 2
    q = q + jnp.dot(q.astype(BF16), stack(y.astype(BF16)), preferred_element_type=F32)
    yield
    out.append(q)


def _run_lockstep(chains):
    live = list(chains)
    while live:
        for g in list(live):
            try:
                next(g)
            except StopIteration:
                live.remove(g)


def _iota2(shape, dim):
    return lax.broadcasted_iota(jnp.int32, shape, dim)


def _chunk_and_shifts(ref, halo_ref, c, rows, shift):
    r0 = pl.multiple_of(c * rows, rows)
    h0 = pl.multiple_of(jnp.maximum(r0 - HALO, 0), HALO)
    halo = jnp.where(c == 0, halo_ref[...], ref[pl.ds(h0, HALO), :])
    cur = ref[pl.ds(r0, rows), :]
    xc = jnp.concatenate([halo, cur], axis=0)
    return cur, [pltpu.roll(xc, s, 0)[HALO:, :] for s in range(1, shift + 1)]


def _recurrence_grid_setup(state_ref, halo_refs):
    @pl.when(pl.program_id(2) == 0)
    def _():
        state_ref[...] = jnp.zeros_like(state_ref)
        for h in halo_refs:
            h[...] = jnp.zeros_like(h)


def _save_halos(pairs, rows):
    for src, dst in pairs:
        dst[...] = src[rows - HALO:rows, :]


def _rms(x, w_row):
    return x * lax.rsqrt(jnp.mean(x * x, axis=-1, keepdims=True) + NORM_EPS) * w_row


def _rmsnorm_kernel(x_ref, nw_ref, o_ref):
    o_ref[...] = _rms(x_ref[...], nw_ref[...]).astype(o_ref.dtype)


def _rmsnorm(x2, norm_w_row, tm=512):
    t = x2.shape[0]
    tm = min(tm, t)
    return pl.pallas_call(
        _rmsnorm_kernel,
        grid=(t // tm,),
        in_specs=[pl.BlockSpec((tm, D_MODEL), lambda i: (i, 0)),
                  pl.BlockSpec((1, D_MODEL), lambda i: (0, 0))],
        out_specs=pl.BlockSpec((tm, D_MODEL), lambda i: (i, 0)),
        out_shape=jax.ShapeDtypeStruct((t, D_MODEL), BF16),
        compiler_params=pltpu.CompilerParams(dimension_semantics=("arbitrary",), vmem_limit_bytes=VMEM_LIMIT),
        name="rmsnorm",
    )(x2, norm_w_row)


_W_SRC_GBA = 4096
_W_SRC_RKV = _W_SRC_GBA + 2 * GDN_HEADS
_W_SRC_WL = _W_SRC_RKV + 3 * RWKV_WIDTH
_W_SRC_RZ = _W_SRC_WL + 2 * RWKV_LORA
_W_BLK = 512


def _inproj_t_kernel(h_ref, main_ref, gba_ref, wl_ref, al_ref, o_ref):
    tail = COL_GBA * LANE // _W_BLK
    nt = (((1,), (1,)), ((), ()))

    @pl.when(pl.program_id(1) < tail)
    def _():
        o_ref[...] = lax.dot_general(h_ref[...], main_ref[0].astype(BF16), nt, preferred_element_type=F32)

    @pl.when(pl.program_id(1) == tail)
    def _():
        k = main_ref.shape[2]
        zeros = lambda n: jnp.zeros((n, k), BF16)
        rows = jnp.concatenate([gba_ref[0].astype(BF16), zeros(LANE - 2 * GDN_HEADS),
                                wl_ref[0].astype(BF16), zeros(LANE - RWKV_LORA),
                                al_ref[0].astype(BF16), zeros(LANE - RWKV_LORA), zeros(LANE)], axis=0)
        o_ref[...] = lax.dot_general(h_ref[...], rows, nt, preferred_element_type=F32)


def _inproj_t(h, w_t, layer, tm=2048):
    t, k = h.shape
    tm = min(tm, t)
    first_rkv, first_rz, tail = COL_RKV * LANE // _W_BLK, COL_RZ * LANE // _W_BLK, COL_GBA * LANE // _W_BLK
    unit = 2 * GDN_HEADS

    def start(j):
        shift = jnp.where(j < first_rkv, 0, jnp.where(j < first_rz, (_W_SRC_RKV - COL_RKV * LANE) // unit,
                                                       (_W_SRC_RZ - COL_RZ * LANE) // unit))
        return jnp.where(j < tail, j * (_W_BLK // unit) + shift, 0) * unit

    rows_at = lambda n, row: pl.BlockSpec((pl.Element(1), pl.Element(n), pl.Element(k)),
                                          lambda i, j: (layer, row(j), 0))
    fixed = lambda n, row: rows_at(n, lambda j: row)
    return pl.pallas_call(
        _inproj_t_kernel,
        grid=(t // tm, NP // _W_BLK),
        in_specs=[pl.BlockSpec((tm, k), lambda i, j: (i, 0)),
                  rows_at(_W_BLK, start),
                  fixed(2 * GDN_HEADS, _W_SRC_GBA), fixed(RWKV_LORA, _W_SRC_WL),
                  fixed(RWKV_LORA, _W_SRC_WL + RWKV_LORA)],
        out_specs=pl.BlockSpec((tm, _W_BLK), lambda i, j: (i, j)),
        out_shape=jax.ShapeDtypeStruct((t, NP), F32),
        compiler_params=pltpu.CompilerParams(
            dimension_semantics=("arbitrary", "arbitrary"), vmem_limit_bytes=VMEM_LIMIT),
        name="inproj",
    )(h, w_t, w_t, w_t, w_t)


def _gdn_parts(q_ref, k_ref, v_ref, z_ref, ba_ref, cq_ref, ck_ref, cv_ref,
               alog_ref, dtb_ref, nw_ref, o_ref, s_ref, hq_ref, hk_ref, hv_ref, *, blk_rows, n_tile):
    c_len = CHUNK
    rows = 2 * c_len
    width = MXU_TILE
    tile0 = pl.program_id(1) * n_tile

    lane_w = _iota2((1, width), 1)
    lane_b = _iota2((1, LANE), 1)
    t_w = _iota2((c_len, rows), 0)
    s_w = _iota2((c_len, rows), 1) % c_len
    strict_w = s_w < t_w
    incl_w = s_w <= t_w
    eye_w = jnp.where(s_w == t_w, 1.0, 0.0)
    triu_w = jnp.where(t_w <= s_w, 1.0, 0.0)
    ti = _iota2((c_len, c_len), 0)
    tj = _iota2((c_len, c_len), 1)
    tril_t = jnp.where(tj <= ti, 1.0, 0.0)
    ones_t = jnp.ones((c_len, c_len), F32)
    same_head = (_iota2((width, width), 0) // GDN_HEAD_DIM) == (_iota2((width, width), 1) // GDN_HEAD_DIM)
    head0_lane = lane_w < GDN_HEAD_DIM
    head0_wide = _iota2((1, rows), 1) < c_len
    zero_b = jnp.zeros((), BF16)

    _recurrence_grid_setup(s_ref, (hq_ref, hk_ref, hv_ref))
    alog_row = alog_ref[...]
    dtb_row = dtb_ref[...]
    nw_row = nw_ref[...]

    def stack(xb):
        return jnp.concatenate([jnp.where(head0_lane, xb, zero_b), jnp.where(head0_lane, zero_b, xb)], axis=0)

    def stack_w(xb):
        return jnp.concatenate([jnp.where(head0_wide, xb, zero_b), jnp.where(head0_wide, zero_b, xb)], axis=0)

    def per_head(x, fn):
        return jnp.concatenate([fn(x[:, :GDN_HEAD_DIM]), fn(x[:, GDN_HEAD_DIM:])], axis=1)

    def l2n(xh):
        return xh * lax.rsqrt(jnp.sum(xh * xh, axis=-1, keepdims=True) + 1e-6)

    def rms(oh):
        return oh * lax.rsqrt(jnp.mean(oh * oh, axis=-1, keepdims=True) + NORM_EPS) * nw_row

    def chains(c):
        r0 = pl.multiple_of(c * c_len, c_len)

        def conv(x_ref, h_ref, cw_ref):
            cur, sh = _chunk_and_shifts(x_ref, h_ref, c, c_len, 3)
            cw = cw_ref[...]
            acc = cur * cw[3:4, :]
            for s in range(1, 4):
                acc = acc + sh[s - 1] * cw[3 - s:4 - s, :]
            return _silu(acc)

        q_all = conv(q_ref, hq_ref, cq_ref)
        k_all = conv(k_ref, hk_ref, ck_ref)
        v_all = conv(v_ref, hv_ref, cv_ref)
        z_all = z_ref[pl.ds(r0, c_len), :]
        ba = ba_ref[pl.ds(r0, c_len), :]
        g_all = -jnp.exp(alog_row) * _softplus(ba + dtb_row)

        def chain(t):
            sl = slice(t * width, (t + 1) * width)
            q = per_head(q_all[:, sl], l2n) * (GDN_HEAD_DIM ** -0.5)
            k = per_head(k_all[:, sl], l2n)
            v = v_all[:, sl]
            betas, gs = [], []
            for h in range(2):
                hid = 2 * (tile0 + t) + h
                betas.append(_sigmoid(jnp.sum(jnp.where(lane_b == hid, ba, 0.0), axis=-1, keepdims=True)))
                gs.append(jnp.sum(jnp.where(lane_b == GDN_HEADS + hid, g_all, 0.0), axis=-1, keepdims=True))
            beta = jnp.where(head0_lane, betas[0], betas[1])
            g_n = jnp.where(head0_lane, gs[0], gs[1])
            g_w = jnp.where(head0_wide, gs[0], gs[1])

            gcol = _mm_sel_lhs(tril_t, g_n)
            grow = _mm_sel_lhs(ones_t, g_w * triu_w)
            yield
            gcol_w = jnp.where(head0_wide, gcol[:, :rows], gcol[:, GDN_HEAD_DIM:GDN_HEAD_DIM + rows])
            diff = gcol_w - grow
            d_strict = jnp.where(strict_w, jnp.exp(jnp.where(strict_w, diff, 0.0)), 0.0)
            d_incl = jnp.where(incl_w, jnp.exp(jnp.where(incl_w, diff, 0.0)), 0.0)
            egc = jnp.exp(gcol)
            gl_lane = gcol[c_len - 1:c_len, :]

            kb = k * beta
            prod = lax.dot_general(jnp.concatenate([kb, q], axis=0).astype(BF16), stack(k.astype(BF16)),
                                   (((1,), (1,)), ((), ())), preferred_element_type=F32)
            yield
            lower = prod[:c_len, :] * d_strict
            attn = prod[c_len:, :] * d_incl
            inv = []
            yield from _unit_lower_inverse(-lower, eye_w, c_len, stack_w, inv)
            rhs = jnp.concatenate([stack((v * beta).astype(BF16)), stack((kb * egc).astype(BF16))], axis=1)
            uw = jnp.dot(inv[0].astype(BF16), rhs, preferred_element_type=F32)
            yield
            u, w = uw[:, :width], uw[:, width:]

            state = s_ref[t]
            x0 = _mm(jnp.concatenate([w, q * egc], axis=0), state)
            yield
            v_new = u - x0[:c_len, :]
            o = x0[c_len:, :] + jnp.dot(attn.astype(BF16), stack(v_new.astype(BF16)), preferred_element_type=F32)
            ds = _mm_tn(k * jnp.exp(gl_lane - gcol), v_new)
            s_ref[t] = state * jnp.exp(gl_lane) + jnp.where(same_head, ds, 0.0)
            yield

            o_ref[pl.ds(r0, c_len), sl] = (per_head(o, rms) * _silu(z_all[:, sl])).astype(o_ref.dtype)

        return [chain(t) for t in range(n_tile)]

    def finish():
        _save_halos(((q_ref, hq_ref), (k_ref, hk_ref), (v_ref, hv_ref)), blk_rows)

    return chains, finish


def _gdn_specs(proj, conv_w, alog_row, dtb_row, nw_row, layer, blk, nsb):
    w = GDN_WIDTH
    colspec = lambda base: pl.BlockSpec((blk, w), lambda b, g, s, base=base: (b * nsb + s, base))
    cwspec = lambda base: pl.BlockSpec((None, 4, w), lambda b, g, s, base=base: (layer, 0, base))
    rowspec = pl.BlockSpec((None, 1, LANE), lambda b, g, s: (layer, 0, 0))
    in_specs = [colspec(0), colspec(1), colspec(2), colspec(COL_GZ * LANE // GDN_WIDTH),
                pl.BlockSpec((blk, LANE), lambda b, g, s: (b * nsb + s, COL_GBA)),
                cwspec(0), cwspec(1), cwspec(2), rowspec, rowspec, rowspec]
    scratch = [pltpu.VMEM((w // MXU_TILE, MXU_TILE, MXU_TILE), F32)] + [pltpu.VMEM((HALO, w), F32)] * 3
    return (proj, proj, proj, proj, proj, conv_w, conv_w, conv_w, alog_row, dtb_row, nw_row), in_specs, scratch


def _rwkv_parts(r_ref, k_ref, v_ref, wl_ref, al_ref, z_ref,
                mur_ref, muk_ref, muv_ref, muwl_ref, mual_ref,
                w0_ref, wup_ref, a0_ref, aup_ref, kk_ref, ka_ref, rk_ref, lnw_ref, lnb_ref,
                o_ref, n_ref, hr_ref, hk_ref, hv_ref, hwl_ref, hal_ref, *, blk_rows, n_tile):
    c_len = CHUNK
    nh = MXU_TILE // RWKV_HEAD_DIM
    width = MXU_TILE
    assert c_len == RWKV_HEAD_DIM

    same_head = (_iota2((width, width), 0) // RWKV_HEAD_DIM) == (_iota2((width, width), 1) // RWKV_HEAD_DIM)
    head_sum = jnp.where(same_head, 1.0, 0.0)
    t_w = _iota2((c_len, width), 0)
    s_w = _iota2((c_len, width), 1) % c_len
    strict_w = s_w < t_w
    incl_w = s_w <= t_w
    eye_w = jnp.where(s_w == t_w, 1.0, 0.0)
    ti = _iota2((c_len, c_len), 0)
    tj = _iota2((c_len, c_len), 1)
    tril_t = jnp.where(tj <= ti, 1.0, 0.0)
    lane_head = _iota2((1, width), 1) // RWKV_HEAD_DIM
    zero_b = jnp.zeros((), BF16)

    _recurrence_grid_setup(n_ref, (hr_ref, hk_ref, hv_ref, hwl_ref, hal_ref))

    def stack(xb):
        return jnp.concatenate([jnp.where(lane_head == h, xb, zero_b) for h in range(nh)], axis=0)

    def chains(c):
        r0 = pl.multiple_of(c * c_len, c_len)

        def shifted(x_ref, h_ref, mu_ref):
            cur, sh = _chunk_and_shifts(x_ref, h_ref, c, c_len, 1)
            return cur + (sh[0] - cur) * mu_ref[...]

        r_all = shifted(r_ref, hr_ref, mur_ref)
        k_all = shifted(k_ref, hk_ref, muk_ref)
        v_all = shifted(v_ref, hv_ref, muv_ref)
        wl = shifted(wl_ref, hwl_ref, muwl_ref)
        al = shifted(al_ref, hal_ref, mual_ref)
        z_all = z_ref[pl.ds(r0, c_len), :]

        w_pre = w0_ref[...] + _mm(jnp.tanh(wl), wup_ref[...])
        logw_all = -jnp.exp(-_softplus(-w_pre) - 0.5)
        a_all = _sigmoid(a0_ref[...] + _mm(al, aup_ref[...]))
        kkx_all = k_all * kk_ref[...]
        k2_all = k_all * (1.0 + (a_all - 1.0) * ka_ref[...])
        lc_all = _mm_sel_lhs(tril_t, logw_all)
        bonus_all = r_all * k2_all * rk_ref[...]

        def chain(t):
            sl = slice(t * width, (t + 1) * width)
            r, v, a, k2, logw, lc = r_all[:, sl], v_all[:, sl], a_all[:, sl], k2_all[:, sl], logw_all[:, sl], lc_all[:, sl]
            kkx = kkx_all[:, sl]
            sums = _mm_sel_rhs(jnp.concatenate([kkx * kkx, bonus_all[:, sl]], axis=0), head_sum)
            yield
            kk = kkx * lax.rsqrt(sums[:c_len, :] + 1e-6)
            bonus = sums[c_len:, :]

            lc_last = lc[c_len - 1:c_len, :]
            e_neg = jnp.exp(-lc)
            e_rem = jnp.exp(lc_last - lc)
            kka = kk * a
            a_n = -kk * jnp.exp(lc - logw)
            r_n = r * jnp.exp(lc)
            sv = stack(v.astype(BF16))

            prod = lax.dot_general(
                jnp.concatenate([a_n, r_n], axis=0).astype(BF16),
                jnp.concatenate([stack((kka * e_neg).astype(BF16)), stack((k2 * e_neg).astype(BF16))], axis=0),
                (((1,), (1,)), ((), ())), preferred_element_type=F32)
            yield
            a_ab = jnp.where(strict_w, prod[:c_len, :width], 0.0)
            a_ak = jnp.where(strict_w, prod[:c_len, width:], 0.0)
            a_rb = jnp.where(incl_w, prod[c_len:, :width], 0.0)
            a_rk = jnp.where(incl_w, prod[c_len:, width:], 0.0)

            inv = []
            yield from _unit_lower_inverse(a_ab, eye_w, c_len, stack, inv)
            t_mat = inv[0].astype(BF16)
            w_p = jnp.dot(t_mat, stack(a_n.astype(BF16)), preferred_element_type=F32)
            aks = jnp.dot(a_ak.astype(BF16), sv, preferred_element_type=F32)
            yield
            v_p = jnp.dot(t_mat, stack(aks.astype(BF16)), preferred_element_type=F32)
            yield

            state = n_ref[t]
            x0 = _mm(jnp.concatenate([w_p, r_n], axis=0), state)
            yield
            u = x0[:c_len, :] + v_p
            y = x0[c_len:, :] + jnp.dot(jnp.concatenate([a_rb, a_rk], axis=1).astype(BF16),
                                        jnp.concatenate([stack(u.astype(BF16)), sv], axis=0),
                                        preferred_element_type=F32)
            dn = _mm_tn(jnp.concatenate([kka * e_rem, k2 * e_rem], axis=0), jnp.concatenate([u, v], axis=0))
            gam_col = jnp.transpose(jnp.broadcast_to(jnp.exp(lc_last), (width, width)))
            n_ref[t] = gam_col * state + jnp.where(same_head, dn, 0.0)
            yield

            inv_n = 1.0 / RWKV_HEAD_DIM
            mean = _mm_sel_rhs(y, head_sum) * inv_n
            yield
            dlt = y - mean
            var = _mm_sel_rhs(dlt * dlt, head_sum) * inv_n
            yield
            y = dlt * lax.rsqrt(var + RWKV_LN_EPS) * lnw_ref[:, sl] + lnb_ref[:, sl]
            y = y + bonus * v
            o_ref[pl.ds(r0, c_len), sl] = (y * _silu(z_all[:, sl])).astype(o_ref.dtype)

        return [chain(t) for t in range(n_tile)]

    def finish():
        _save_halos(((r_ref, hr_ref), (k_ref, hk_ref), (v_ref, hv_ref), (wl_ref, hwl_ref), (al_ref, hal_ref)),
                    blk_rows)

    return chains, finish


def _rwkv_specs(proj, p, layer, blk, nsb):
    w = RWKV_WIDTH
    colspec = lambda base: pl.BlockSpec((blk, w), lambda b, g, s, base=base: (b * nsb + s, base))
    lspec = lambda col: pl.BlockSpec((blk, LANE), lambda b, g, s, col=col: (b * nsb + s, col))
    grow = pl.BlockSpec((None, 1, w), lambda b, g, s: (layer, 0, 0))
    lrow = pl.BlockSpec((None, 1, LANE), lambda b, g, s: (layer, 0, 0))
    upspec = pl.BlockSpec((None, LANE, w), lambda b, g, s: (layer, 0, 0))
    base = COL_RKV * LANE // RWKV_WIDTH
    in_specs = [colspec(base), colspec(base + 1), colspec(base + 2), lspec(COL_WL), lspec(COL_AL),
                colspec(COL_RZ * LANE // RWKV_WIDTH),
                grow, grow, grow, lrow, lrow,
                grow, upspec, grow, upspec, grow, grow, grow, grow, grow]
    scratch = ([pltpu.VMEM((w // MXU_TILE, MXU_TILE, MXU_TILE), F32)] + [pltpu.VMEM((HALO, w), F32)] * 3
               + [pltpu.VMEM((HALO, LANE), F32)] * 2)
    operands = (proj, proj, proj, proj, proj, proj,
                p["mu_r"], p["mu_k"], p["mu_v"], p["mu_wl"], p["mu_al"],
                p["w0"], p["w_up"], p["a0"], p["a_up"], p["k_k"], p["k_a"], p["r_k"], p["lnx_w"], p["lnx_b"])
    return operands, in_specs, scratch


def _mixers_kernel(*refs, n_gdn, n_rwkv, n_gdn_scr, blk_rows):
    gdn_in, rwkv_in = refs[:n_gdn], refs[n_gdn:n_gdn + n_rwkv]
    o_gdn, o_rwkv = refs[n_gdn + n_rwkv:n_gdn + n_rwkv + 2]
    scratch = refs[n_gdn + n_rwkv + 2:]
    g_chains, g_finish = _gdn_parts(*gdn_in, o_gdn, *scratch[:n_gdn_scr], blk_rows=blk_rows,
                                    n_tile=GDN_WIDTH // MXU_TILE)
    r_chains, r_finish = _rwkv_parts(*rwkv_in, o_rwkv, *scratch[n_gdn_scr:], blk_rows=blk_rows,
                                     n_tile=RWKV_WIDTH // MXU_TILE)

    def body(c, carry):
        _run_lockstep([ch for pair in zip(g_chains(c), r_chains(c)) for ch in pair])
        return carry

    lax.fori_loop(0, blk_rows // CHUNK, body, 0)
    g_finish()
    r_finish()


def _gdn_rwkv(proj, gdn_args, rwkv_p, layer, batch, seq):
    blk = min(SEQ_BLOCK, seq)
    nsb = seq // blk
    g_ops, g_specs, g_scr = _gdn_specs(proj, *gdn_args, layer, blk, nsb)
    r_ops, r_specs, r_scr = _rwkv_specs(proj, rwkv_p, layer, blk, nsb)
    out_spec = pl.BlockSpec((blk, GDN_WIDTH), lambda b, g, s: (b * nsb + s, 0))
    out_shape = jax.ShapeDtypeStruct((batch * seq, GDN_WIDTH), BF16)
    return pl.pallas_call(
        functools.partial(_mixers_kernel, n_gdn=len(g_ops), n_rwkv=len(r_ops), n_gdn_scr=len(g_scr),
                          blk_rows=blk),
        grid=(batch, 1, nsb),
        in_specs=g_specs + r_specs,
        out_specs=(out_spec, out_spec),
        out_shape=(out_shape, out_shape),
        scratch_shapes=g_scr + r_scr,
        compiler_params=pltpu.CompilerParams(
            dimension_semantics=("arbitrary", "arbitrary", "arbitrary"), vmem_limit_bytes=VMEM_LIMIT),
        name="gdn_rwkv",
    )(*g_ops, *r_ops)


def _s5_kernel(*refs, seq, n_lb):
    u_refs = refs[:n_lb]
    (lre_ref, lim_ref, ldt_ref, bre_ref, bim_ref, cre_ref, cim_ref, d_ref,
     o_ref, pre_ref, pim_ref, qre_ref, qim_ref, w_ref) = refs[n_lb:]
    tile = S5_TILE
    n_pair = LANE // (2 * S5_GROUP)
    n_chain = n_lb * n_pair
    sub = HALO
    n_sub = tile // sub

    def cmul(ar, ai, br, bi):
        return ar * br - ai * bi, ar * bi + ai * br

    def block_scan(sr, si, ar, ai, out):
        in_blk = _iota2(sr.shape, 0) % sub
        d = 1
        while d < sub:
            keep = in_blk >= d
            tr, ti = cmul(ar, ai, jnp.where(keep, pltpu.roll(sr, d, 0), 0.0),
                          jnp.where(keep, pltpu.roll(si, d, 0), 0.0))
            sr, si = sr + tr, si + ti
            ar, ai = cmul(ar, ai, ar, ai)
            d *= 2
            yield
        out.extend((sr, si))

    first = _iota2((sub, LANE), 0) == 0
    ab = []
    for q in range(n_chain):
        lb, p = divmod(q, n_pair)
        lre = lre_ref[lb, p:p + 1, :]
        lim = lim_ref[lb, p:p + 1, :]
        dt = jnp.exp(ldt_ref[lb, p:p + 1, :])
        mag = jnp.exp(lre * dt)
        ab_re = mag * jnp.cos(lim * dt)
        ab_im = mag * jnp.sin(lim * dt)
        den = lre * lre + lim * lim
        coef_re = ((ab_re - 1.0) * lre + ab_im * lim) / den
        coef_im = (ab_im * lre - (ab_re - 1.0) * lim) / den
        b_re = bre_ref[lb, p]
        b_im = bim_ref[lb, p]

        def place(m, p=p):
            top, n = p * 2 * S5_GROUP, m.shape[0]
            parts = [jnp.zeros((top, LANE), BF16)] if top else []
            parts.append(m.astype(BF16))
            if LANE - top - n:
                parts.append(jnp.zeros((LANE - top - n, LANE), BF16))
            return jnp.concatenate(parts, axis=0)

        w_ref[q, 0] = place(coef_re * b_re - coef_im * b_im)
        w_ref[q, 1] = place(coef_re * b_im + coef_im * b_re)
        w_ref[q, 2] = place(cre_ref[lb, p])
        w_ref[q, 3] = place(cim_ref[lb, p])
        pr, pi = [ab_re], [ab_im]
        for _ in range(1, n_sub):
            nr, ni = cmul(pr[-1], pi[-1], ab_re, ab_im)
            pr.append(nr)
            pi.append(ni)
        pre_ref[q] = jnp.concatenate(pr, axis=0)
        pim_ref[q] = jnp.concatenate(pi, axis=0)
        imp_re = jnp.where(first, jnp.broadcast_to(pr[-1], (sub, LANE)), 0.0)
        imp_im = jnp.where(first, jnp.broadcast_to(pi[-1], (sub, LANE)), 0.0)
        pw = []
        for _ in block_scan(imp_re, imp_im, pr[-1], pi[-1], pw):
            pass
        qre_ref[q] = pw[0]
        qim_ref[q] = pw[1]
        ab.append((ab_re, ab_im, pr[-1], pi[-1]))

    def tile_body(i, st):
        t0 = pl.multiple_of(i * tile, tile)
        us = [jnp.concatenate([u_refs[lb][pl.ds(t0 + j, sub, stride=n_sub), :] for j in range(n_sub)], axis=0)
              for lb in range(n_lb)]
        ubs = [u.astype(BF16) for u in us]
        ys, new_st = [None] * n_chain, [None] * n_chain

        def chain(p):
            cr, ci = st[p]
            ar, ai, a_run_r, a_run_i = ab[p]
            ub = ubs[p // n_pair]
            sr = jnp.dot(ub, w_ref[p, 0], preferred_element_type=F32)
            si = jnp.dot(ub, w_ref[p, 1], preferred_element_type=F32)
            yield
            loc_r, loc_i = [sr[:sub, :]], [si[:sub, :]]
            for j in range(1, n_sub):
                tr, ti = cmul(ar, ai, loc_r[-1], loc_i[-1])
                loc_r.append(sr[j * sub:(j + 1) * sub, :] + tr)
                loc_i.append(si[j * sub:(j + 1) * sub, :] + ti)
                if j % 4 == 0:
                    yield
            res = []
            yield from block_scan(loc_r[-1], loc_i[-1], a_run_r, a_run_i, res)
            tr, ti = cmul(qre_ref[p], qim_ref[p], cr, ci)
            end_r, end_i = res[0] + tr, res[1] + ti
            new_st[p] = (end_r[sub - 1:sub, :], end_i[sub - 1:sub, :])
            in_r = jnp.where(first, cr, pltpu.roll(end_r, 1, 0))
            in_i = jnp.where(first, ci, pltpu.roll(end_i, 1, 0))
            out_r, out_i = [], []
            for j in range(n_sub):
                tr, ti = cmul(pre_ref[p, j:j + 1, :], pim_ref[p, j:j + 1, :], in_r, in_i)
                out_r.append(loc_r[j] + tr)
                out_i.append(loc_i[j] + ti)
            sr, si = jnp.concatenate(out_r, axis=0), jnp.concatenate(out_i, axis=0)
            nt = (((1,), (1,)), ((), ()))
            ys[p] = (lax.dot_general(sr.astype(BF16), w_ref[p, 2], nt, preferred_element_type=F32)
                     - lax.dot_general(si.astype(BF16), w_ref[p, 3], nt, preferred_element_type=F32))

        _run_lockstep([chain(p) for p in range(n_chain)])
        for lb in range(n_lb):
            y = d_ref[lb] * us[lb]
            for p in range(n_pair):
                y = y + ys[lb * n_pair + p]
            inner = math.sqrt(2.0 / math.pi) * (y + 0.044715 * (y * y * y))
            y = 0.5 * y * (1.0 + jnp.tanh(inner))
            for j in range(n_sub):
                o_ref[lb, pl.ds(t0 + j, sub, stride=n_sub), :] = y[j * sub:(j + 1) * sub, :]
        return tuple(new_st)

    zero = jnp.zeros((1, LANE), F32)
    lax.fori_loop(0, seq // tile, tile_body, tuple((zero, zero) for _ in range(n_chain)))


def _s5(proj, p, layer, batch, seq, n_lb=4):
    n_blk = S5_WIDTH // (n_lb * LANE)
    n_pair = LANE // (2 * S5_GROUP)
    n_chain = n_lb * n_pair
    rowspec = pl.BlockSpec((None, n_lb, n_pair, LANE), lambda b, j: (layer, j, 0, 0))
    matspec = pl.BlockSpec((None, n_lb, n_pair, 2 * S5_GROUP, LANE), lambda b, j: (layer, j, 0, 0, 0))
    return pl.pallas_call(
        functools.partial(_s5_kernel, seq=seq, n_lb=n_lb),
        grid=(batch, n_blk),
        in_specs=[pl.BlockSpec((seq, LANE), lambda b, j, lb=lb: (b, COL_SU + n_lb * j + lb)) for lb in range(n_lb)]
                 + [rowspec, rowspec, rowspec, matspec, matspec, matspec, matspec,
                    pl.BlockSpec((None, n_lb, 1, LANE), lambda b, j: (layer, j, 0, 0))],
        out_specs=pl.BlockSpec((n_lb, seq, LANE), lambda b, j: (j, b, 0)),
        out_shape=jax.ShapeDtypeStruct((S5_WIDTH // LANE, batch * seq, LANE), F32),
        scratch_shapes=[pltpu.VMEM((n_chain, S5_TILE // HALO, LANE), F32)] * 2
                       + [pltpu.VMEM((n_chain, HALO, LANE), F32)] * 2
                       + [pltpu.VMEM((n_chain, 4, LANE, LANE), BF16)],
        compiler_params=pltpu.CompilerParams(
            dimension_semantics=("arbitrary", "arbitrary"), vmem_limit_bytes=VMEM_LIMIT),
        name="s5_scan",
    )(*([proj] * n_lb), p["lam_re"], p["lam_im"], p["log_dt"], p["b_re"], p["b_im"], p["c_re"], p["c_im"], p["d"])


def _s5_glu_kernel(y_ref, w_ref, b_ref, z_ref, o_ref):
    y = jnp.concatenate([y_ref[j] for j in range(S5_WIDTH // LANE)], axis=1)
    gate = _sigmoid(_mm(y, w_ref[...]) + b_ref[...])
    o_ref[...] = (y * gate * _silu(z_ref[...])).astype(o_ref.dtype)


def _s5_glu(yc, proj, glu_w, glu_b, layer, tm=512):
    t = yc.shape[1]
    tm = min(tm, t)
    return pl.pallas_call(
        _s5_glu_kernel,
        grid=(t // tm,),
        in_specs=[pl.BlockSpec((S5_WIDTH // LANE, tm, LANE), lambda i: (0, i, 0)),
                  pl.BlockSpec((None, S5_WIDTH, S5_WIDTH), lambda i: (layer, 0, 0)),
                  pl.BlockSpec((None, 1, S5_WIDTH), lambda i: (layer, 0, 0)),
                  pl.BlockSpec((tm, S5_WIDTH), lambda i: (i, COL_SZ * LANE // S5_WIDTH))],
        out_specs=pl.BlockSpec((tm, S5_WIDTH), lambda i: (i, 0)),
        out_shape=jax.ShapeDtypeStruct((t, S5_WIDTH), BF16),
        compiler_params=pltpu.CompilerParams(
            dimension_semantics=("arbitrary",), vmem_limit_bytes=VMEM_LIMIT),
        name="s5_glu",
    )(yc, glu_w, glu_b, proj)


def _merge_kernel(oa_ref, ob_ref, oc_ref, wb_ref, ga_ref, gb_ref, gc_ref, gbias_ref, o_ref):
    acc = None
    for i, (o_r, g_r) in enumerate(((oa_ref, ga_ref), (ob_ref, gb_ref), (oc_ref, gc_ref))):
        proj = jnp.dot(o_r[...], wb_ref[i, pl.program_id(1)], preferred_element_type=F32)
        term = _sigmoid(g_r[...] + gbias_ref[i]) * proj
        acc = term if acc is None else acc + term
    o_ref[...] = acc.astype(o_ref.dtype)


def _merge(oa, ob, oc, proj, w_branch, gate_b, layer, tm=1024, tn=512):
    t = oa.shape[0]
    tm = min(tm, t)
    gbase = COL_GATE * LANE // tn
    per = D_MODEL // tn
    ospec = pl.BlockSpec((tm, 1024), lambda i, j: (i, 0))
    gspec = lambda br: pl.BlockSpec((tm, tn), lambda i, j, br=br: (i, gbase + br * per + j))
    return pl.pallas_call(
        _merge_kernel,
        grid=(t // tm, per),
        in_specs=[ospec, ospec, ospec,
                  pl.BlockSpec((None, N_BRANCH, per, 1024, tn), lambda i, j: (layer, 0, 0, 0, 0),
                               pipeline_mode=pl.Buffered(1)),
                  gspec(0), gspec(1), gspec(2),
                  pl.BlockSpec((None, N_BRANCH, 1, tn), lambda i, j: (layer, 0, 0, j))],
        out_specs=pl.BlockSpec((tm, tn), lambda i, j: (i, j)),
        out_shape=jax.ShapeDtypeStruct((t, D_MODEL), BF16),
        compiler_params=pltpu.CompilerParams(
            dimension_semantics=("arbitrary", "arbitrary"), vmem_limit_bytes=VMEM_LIMIT),
        name="merge",
    )(oa, ob, oc, w_branch, proj, proj, proj, gate_b)


def _outproj_kernel(m_ref, w_ref, x_ref, nw_ref, *o_refs):
    x = x_ref[...] + jnp.dot(m_ref[...], w_ref[...], preferred_element_type=F32)
    normed = _rms(x, nw_ref[...])
    if len(o_refs) == 2:
        o_refs[0][...] = x
    o_refs[-1][...] = normed.astype(o_refs[-1].dtype)


def _outproj(merged, w_out, x2, next_norm_rows, layer, last, tm=512):
    t = x2.shape[0]
    tm = min(tm, t)
    row_blk = pl.BlockSpec((tm, D_MODEL), lambda i: (i, 0))
    if last:
        out_specs, out_shape = row_blk, jax.ShapeDtypeStruct((t, D_MODEL), F32)
    else:
        out_specs = (row_blk, row_blk)
        out_shape = (jax.ShapeDtypeStruct((t, D_MODEL), F32), jax.ShapeDtypeStruct((t, D_MODEL), BF16))
    return pl.pallas_call(
        _outproj_kernel,
        grid=(t // tm,),
        in_specs=[row_blk,
                  pl.BlockSpec((None, D_MODEL, D_MODEL), lambda i: (layer, 0, 0)),
                  row_blk,
                  pl.BlockSpec((None, 1, D_MODEL), lambda i: (layer, 0, 0))],
        out_specs=out_specs,
        out_shape=out_shape,
        compiler_params=pltpu.CompilerParams(
            dimension_semantics=("arbitrary",), vmem_limit_bytes=VMEM_LIMIT),
        name="outproj",
    )(merged, w_out, x2, next_norm_rows)


def _pad_cols(a, width):
    return jnp.pad(a, [(0, 0)] * (a.ndim - 1) + [(0, width - a.shape[-1])])


def _pair_s5(a, rows_are_channels):
    n_blk, n_pair = S5_WIDTH // LANE, LANE // (2 * S5_GROUP)
    cs = a if rows_are_channels else jnp.swapaxes(a, 1, 2)
    cs = cs.reshape(n_blk, n_pair, 2, S5_GROUP, S5_STATE)
    same = jnp.eye(2, dtype=bool)[None, None, :, None, :, None]
    paired = jnp.where(same, cs[:, :, :, :, None, :], 0.0)
    return paired.reshape(n_blk, n_pair, 2 * S5_GROUP, 2 * S5_STATE)


def _s5_rows(a):
    return a.reshape(S5_WIDTH // LANE, LANE // (2 * S5_GROUP), 2 * S5_STATE)


def _rows(a):
    return a.reshape(a.shape[0], 1, -1)


def _gdn_head_rows(a):
    return _pad_cols(_rows(jnp.pad(a, ((0, 0), (GDN_HEADS, 0)))), LANE)


def _rwkv_params(mu, w0, w_up, a0, a_up, k_k, k_a, r_k, lnx_w, lnx_b):
    lora = lambda a: jnp.pad(a, ((0, 0), (0, LANE - RWKV_LORA), (0, 0))).astype(BF16)
    w3 = 3 * RWKV_WIDTH
    return dict(mu_r=_rows(mu[:, :RWKV_WIDTH]), mu_k=_rows(mu[:, RWKV_WIDTH:2 * RWKV_WIDTH]),
                mu_v=_rows(mu[:, 2 * RWKV_WIDTH:w3]),
                mu_wl=_pad_cols(_rows(mu[:, w3:w3 + RWKV_LORA]), LANE),
                mu_al=_pad_cols(_rows(mu[:, w3 + RWKV_LORA:]), LANE),
                w0=_rows(w0), w_up=lora(w_up), a0=_rows(a0), a_up=lora(a_up), k_k=_rows(k_k), k_a=_rows(k_a),
                r_k=_rows(r_k), lnx_w=_rows(lnx_w), lnx_b=_rows(lnx_b))


def _s5_params(a_re, a_im, log_dt, b_re, b_im, c_re, c_im, d):
    depth = a_re.shape[0]
    rows = jax.vmap(_s5_rows)
    place = lambda a, t: jax.vmap(lambda m: _pair_s5(m, t))(a)
    return dict(lam_re=rows(a_re), lam_im=rows(a_im),
                log_dt=rows(jnp.broadcast_to(log_dt[:, :, None], (depth, S5_WIDTH // S5_GROUP, S5_STATE))),
                b_re=place(b_re, False), b_im=place(b_im, False),
                c_re=place(c_re, True), c_im=place(c_im, True),
                d=d.reshape(depth, S5_WIDTH // LANE, 1, LANE))


def kernel(x, norm_w, w_in, gdn_conv_w, gdn_a_log, gdn_dt_bias, gdn_norm_w, rwkv_mu, rwkv_w0, rwkv_w_up,
           rwkv_a0, rwkv_a_up, rwkv_k_k, rwkv_k_a, rwkv_r_k, rwkv_lnx_w, rwkv_lnx_b, s5_a_re, s5_a_im,
           s5_log_dt, s5_b_re, s5_b_im, s5_c_re, s5_c_im, s5_d, s5_glu_w, s5_glu_b, gate_b, w_branch,
           w_out, final_norm_w):
    batch, seq, _ = x.shape
    depth = w_in.shape[0]
    x2 = x.reshape(batch * seq, D_MODEL)
    h = _rmsnorm(x2, norm_w[0].reshape(1, D_MODEL))

    gdn_alog, gdn_dtb, gdn_nw = _gdn_head_rows(gdn_a_log), _gdn_head_rows(gdn_dt_bias), _rows(gdn_norm_w)
    rp = _rwkv_params(rwkv_mu, rwkv_w0, rwkv_w_up, rwkv_a0, rwkv_a_up, rwkv_k_k, rwkv_k_a, rwkv_r_k,
                      rwkv_lnx_w, rwkv_lnx_b)
    sp = _s5_params(s5_a_re, s5_a_im, s5_log_dt, s5_b_re, s5_b_im, s5_c_re, s5_c_im, s5_d)
    glu_w, glu_b = s5_glu_w.astype(BF16), _rows(s5_glu_b)
    wb = jnp.swapaxes(w_branch.astype(BF16).reshape(depth, N_BRANCH, 1024, D_MODEL // 512, 512), 2, 3)
    gb = gate_b.reshape(depth, N_BRANCH, 1, D_MODEL)
    wo = w_out.astype(BF16)
    next_norm = _rows(jnp.concatenate([norm_w[1:], final_norm_w[None, :]], axis=0))
    w_in_t = jnp.swapaxes(w_in, 1, 2)

    for i in range(depth):
        proj = _inproj_t(h, w_in_t, i)
        o_a, o_b = _gdn_rwkv(proj, (gdn_conv_w, gdn_alog, gdn_dtb, gdn_nw), rp, i, batch, seq)
        o_c = _s5_glu(_s5(proj, sp, i, batch, seq), proj, glu_w, glu_b, i)
        merged = _merge(o_a, o_b, o_c, proj, wb, gb, i)
        if i == depth - 1:
            return _outproj(merged, wo, x2, next_norm, i, True).reshape(batch, seq, D_MODEL)
        x2, h = _outproj(merged, wo, x2, next_norm, i, False)
```

```python
import functools
import math

import jax
import jax.numpy as jnp
from jax import lax
from jax.experimental import pallas as pl
from jax.experimental.pallas import tpu as pltpu

F32 = jnp.float32
BF16 = jnp.bfloat16

D_MODEL = 2048
GDN_HEADS = 8
GDN_HEAD_DIM = 128
GDN_WIDTH = 1024
RWKV_HEAD_DIM = 64
RWKV_WIDTH = 1024
RWKV_LORA = 96
RWKV_LN_EPS = 64e-5
S5_GROUP = 16
S5_STATE = 64
S5_WIDTH = 1024
N_BRANCH = 3
NORM_EPS = 1e-6

LANE = 128
MXU_TILE = 256
HALO = 8
CHUNK = 64
SEQ_BLOCK = 512
S5_TILE = 256
VMEM_LIMIT = 52 * 1024 * 1024

COL_QKV = 0
COL_GZ = 24
COL_RKV = 32
COL_RZ = 56
COL_SU = 64
COL_SZ = 72
COL_GATE = 80
COL_GBA = 128
COL_WL = 129
COL_AL = 130
N_COLBLK = 132
NP = N_COLBLK * LANE


def _sigmoid(x):
    return 1.0 / (1.0 + jnp.exp(-x))


def _silu(x):
    return x * _sigmoid(x)


def _softplus(x):
    return jnp.maximum(x, 0.0) + jnp.log1p(jnp.exp(-jnp.abs(x)))


def _mm(a, b):
    return jnp.dot(a.astype(BF16), b.astype(BF16), preferred_element_type=F32)


def _mm_nt(a, b):
    return lax.dot_general(a.astype(BF16), b.astype(BF16), (((1,), (1,)), ((), ())),
                           preferred_element_type=F32)


def _mm_tn(a, b):
    return lax.dot_general(a.astype(BF16), b.astype(BF16), (((0,), (0,)), ((), ())),
                           preferred_element_type=F32)


def _split_bf16(x, parts):
    out = []
    for _ in range(parts - 1):
        hi = x.astype(BF16)
        out.append(hi)
        x = x - hi.astype(F32)
    out.append(x.astype(BF16))
    return out


def _mm_sel_lhs(sel, x, parts=2):
    cols = x.shape[1]
    res = jnp.dot(sel.astype(BF16), jnp.concatenate(_split_bf16(x, parts), axis=1), preferred_element_type=F32)
    out = res[:, :cols]
    for i in range(1, parts):
        out = out + res[:, i * cols:(i + 1) * cols]
    return out


def _mm_sel_rhs(x, sel, parts=2):
    rows = x.shape[0]
    res = jnp.dot(jnp.concatenate(_split_bf16(x, parts), axis=0), sel.astype(BF16), preferred_element_type=F32)
    out = res[:rows, :]
    for i in range(1, parts):
        out = out + res[i * rows:(i + 1) * rows, :]
    return out


def _unit_lower_inverse(x, eye, size, stack, out):
    rows = x.shape[0]
    y = x
    q = eye + x
    y = jnp.dot(x.astype(BF16), stack(x.astype(BF16)), preferred_element_type=F32)
    yield
    span = 4
    while span < size:
        res = jnp.dot(jnp.concatenate([y, q], axis=0).astype(BF16), stack(y.astype(BF16)),
                      preferred_element_type=F32)
        yield
        y, q = res[:rows, :], q + res[rows:, :]
        span *= 2
    q = q + jnp.dot(q.astype(BF16), stack(y.astype(BF16)), preferred_element_type=F32)
    yield
    out.append(q)


def _run_lockstep(chains):
    live = list(chains)
    while live:
        for g in list(live):
            try:
                next(g)
            except StopIteration:
                live.remove(g)


def _iota2(shape, dim):
    return lax.broadcasted_iota(jnp.int32, shape, dim)


def _chunk_and_shifts(ref, halo_ref, c, rows, shift):
    r0 = pl.multiple_of(c * rows, rows)
    h0 = pl.multiple_of(jnp.maximum(r0 - HALO, 0), HALO)
    halo = jnp.where(c == 0, halo_ref[...], ref[pl.ds(h0, HALO), :])
    cur = ref[pl.ds(r0, rows), :]
    xc = jnp.concatenate([halo, cur], axis=0)
    return cur, [pltpu.roll(xc, s, 0)[HALO:, :] for s in range(1, shift + 1)]


def _recurrence_grid_setup(state_ref, halo_refs):
    @pl.when(pl.program_id(2) == 0)
    def _():
        state_ref[...] = jnp.zeros_like(state_ref)
        for h in halo_refs:
            h[...] = jnp.zeros_like(h)


def _save_halos(pairs, rows):
    for src, dst in pairs:
        dst[...] = src[rows - HALO:rows, :]


def _rms(x, w_row):
    return x * lax.rsqrt(jnp.mean(x * x, axis=-1, keepdims=True) + NORM_EPS) * w_row


def _rmsnorm_kernel(x_ref, nw_ref, o_ref):
    o_ref[...] = _rms(x_ref[...], nw_ref[...]).astype(o_ref.dtype)


def _rmsnorm(x2, norm_w_row, tm=512):
    t = x2.shape[0]
    tm = min(tm, t)
    return pl.pallas_call(
        _rmsnorm_kernel,
        grid=(t // tm,),
        in_specs=[pl.BlockSpec((tm, D_MODEL), lambda i: (i, 0)),
                  pl.BlockSpec((1, D_MODEL), lambda i: (0, 0))],
        out_specs=pl.BlockSpec((tm, D_MODEL), lambda i: (i, 0)),
        out_shape=jax.ShapeDtypeStruct((t, D_MODEL), BF16),
        compiler_params=pltpu.CompilerParams(dimension_semantics=("arbitrary",), vmem_limit_bytes=VMEM_LIMIT),
        name="rmsnorm",
    )(x2, norm_w_row)


_W_SRC_GBA = 4096
_W_SRC_RKV = _W_SRC_GBA + 2 * GDN_HEADS
_W_SRC_WL = _W_SRC_RKV + 3 * RWKV_WIDTH
_W_SRC_RZ = _W_SRC_WL + 2 * RWKV_LORA
_W_BLK = 512


def _inproj_t_kernel(h_ref, main_ref, gba_ref, wl_ref, al_ref, o_ref):
    tail = COL_GBA * LANE // _W_BLK
    nt = (((1,), (1,)), ((), ()))

    @pl.when(pl.program_id(1) < tail)
    def _():
        o_ref[...] = lax.dot_general(h_ref[...], main_ref[0].astype(BF16), nt, preferred_element_type=F32)

    @pl.when(pl.program_id(1) == tail)
    def _():
        k = main_ref.shape[2]
        zeros = lambda n: jnp.zeros((n, k), BF16)
        rows = jnp.concatenate([gba_ref[0].astype(BF16), zeros(LANE - 2 * GDN_HEADS),
                                wl_ref[0].astype(BF16), zeros(LANE - RWKV_LORA),
                                al_ref[0].astype(BF16), zeros(LANE - RWKV_LORA), zeros(LANE)], axis=0)
        o_ref[...] = lax.dot_general(h_ref[...], rows, nt, preferred_element_type=F32)


def _inproj_t(h, w_t, layer, tm=2048):
    t, k = h.shape
    tm = min(tm, t)
    first_rkv, first_rz, tail = COL_RKV * LANE // _W_BLK, COL_RZ * LANE // _W_BLK, COL_GBA * LANE // _W_BLK
    unit = 2 * GDN_HEADS

    def start(j):
        shift = jnp.where(j < first_rkv, 0, jnp.where(j < first_rz, (_W_SRC_RKV - COL_RKV * LANE) // unit,
                                                       (_W_SRC_RZ - COL_RZ * LANE) // unit))
        return jnp.where(j < tail, j * (_W_BLK // unit) + shift, 0) * unit

    rows_at = lambda n, row: pl.BlockSpec((pl.Element(1), pl.Element(n), pl.Element(k)),
                                          lambda i, j: (layer, row(j), 0))
    fixed = lambda n, row: rows_at(n, lambda j: row)
    return pl.pallas_call(
        _inproj_t_kernel,
        grid=(t // tm, NP // _W_BLK),
        in_specs=[pl.BlockSpec((tm, k), lambda i, j: (i, 0)),
                  rows_at(_W_BLK, start),
                  fixed(2 * GDN_HEADS, _W_SRC_GBA), fixed(RWKV_LORA, _W_SRC_WL),
                  fixed(RWKV_LORA, _W_SRC_WL + RWKV_LORA)],
        out_specs=pl.BlockSpec((tm, _W_BLK), lambda i, j: (i, j)),
        out_shape=jax.ShapeDtypeStruct((t, NP), F32),
        compiler_params=pltpu.CompilerParams(
            dimension_semantics=("arbitrary", "arbitrary"), vmem_limit_bytes=VMEM_LIMIT),
        name="inproj",
    )(h, w_t, w_t, w_t, w_t)


def _gdn_parts(q_ref, k_ref, v_ref, z_ref, ba_ref, cq_ref, ck_ref, cv_ref,
               alog_ref, dtb_ref, nw_ref, o_ref, s_ref, hq_ref, hk_ref, hv_ref, *, blk_rows, n_tile):
    c_len = CHUNK
    rows = 2 * c_len
    width = MXU_TILE
    tile0 = pl.program_id(1) * n_tile

    lane_w = _iota2((1, width), 1)
    lane_b = _iota2((1, LANE), 1)
    t_w = _iota2((c_len, rows), 0)
    s_w = _iota2((c_len, rows), 1) % c_len
    strict_w = s_w < t_w
    incl_w = s_w <= t_w
    eye_w = jnp.where(s_w == t_w, 1.0, 0.0)
    triu_w = jnp.where(t_w <= s_w, 1.0, 0.0)
    ti = _iota2((c_len, c_len), 0)
    tj = _iota2((c_len, c_len), 1)
    tril_t = jnp.where(tj <= ti, 1.0, 0.0)
    ones_t = jnp.ones((c_len, c_len), F32)
    same_head = (_iota2((width, width), 0) // GDN_HEAD_DIM) == (_iota2((width, width), 1) // GDN_HEAD_DIM)
    head0_lane = lane_w < GDN_HEAD_DIM
    head0_wide = _iota2((1, rows), 1) < c_len
    zero_b = jnp.zeros((), BF16)

    _recurrence_grid_setup(s_ref, (hq_ref, hk_ref, hv_ref))
    alog_row = alog_ref[...]
    dtb_row = dtb_ref[...]
    nw_row = nw_ref[...]

    def stack(xb):
        return jnp.concatenate([jnp.where(head0_lane, xb, zero_b), jnp.where(head0_lane, zero_b, xb)], axis=0)

    def stack_w(xb):
        return jnp.concatenate([jnp.where(head0_wide, xb, zero_b), jnp.where(head0_wide, zero_b, xb)], axis=0)

    def per_head(x, fn):
        return jnp.concatenate([fn(x[:, :GDN_HEAD_DIM]), fn(x[:, GDN_HEAD_DIM:])], axis=1)

    def l2n(xh):
        return xh * lax.rsqrt(jnp.sum(xh * xh, axis=-1, keepdims=True) + 1e-6)

    def rms(oh):
        return oh * lax.rsqrt(jnp.mean(oh * oh, axis=-1, keepdims=True) + NORM_EPS) * nw_row

    def chains(c):
        r0 = pl.multiple_of(c * c_len, c_len)

        def conv(x_ref, h_ref, cw_ref):
            cur, sh = _chunk_and_shifts(x_ref, h_ref, c, c_len, 3)
            cw = cw_ref[...]
            acc = cur * cw[3:4, :]
            for s in range(1, 4):
                acc = acc + sh[s - 1] * cw[3 - s:4 - s, :]
            return _silu(acc)

        q_all = conv(q_ref, hq_ref, cq_ref)
        k_all = conv(k_ref, hk_ref, ck_ref)
        v_all = conv(v_ref, hv_ref, cv_ref)
        z_all = z_ref[pl.ds(r0, c_len), :]
        ba = ba_ref[pl.ds(r0, c_len), :]
        g_all = -jnp.exp(alog_row) * _softplus(ba + dtb_row)

        def chain(t):
            sl = slice(t * width, (t + 1) * width)
            q = per_head(q_all[:, sl], l2n) * (GDN_HEAD_DIM ** -0.5)
            k = per_head(k_all[:, sl], l2n)
            v = v_all[:, sl]
            betas, gs = [], []
            for h in range(2):
                hid = 2 * (tile0 + t) + h
                betas.append(_sigmoid(jnp.sum(jnp.where(lane_b == hid, ba, 0.0), axis=-1, keepdims=True)))
                gs.append(jnp.sum(jnp.where(lane_b == GDN_HEADS + hid, g_all, 0.0), axis=-1, keepdims=True))
            beta = jnp.where(head0_lane, betas[0], betas[1])
            g_n = jnp.where(head0_lane, gs[0], gs[1])
            g_w = jnp.where(head0_wide, gs[0], gs[1])

            gcol = _mm_sel_lhs(tril_t, g_n)
            grow = _mm_sel_lhs(ones_t, g_w * triu_w)
            yield
            gcol_w = jnp.where(head0_wide, gcol[:, :rows], gcol[:, GDN_HEAD_DIM:GDN_HEAD_DIM + rows])
            diff = gcol_w - grow
            d_strict = jnp.where(strict_w, jnp.exp(jnp.where(strict_w, diff, 0.0)), 0.0)
            d_incl = jnp.where(incl_w, jnp.exp(jnp.where(incl_w, diff, 0.0)), 0.0)
            egc = jnp.exp(gcol)
            gl_lane = gcol[c_len - 1:c_len, :]

            kb = k * beta
            prod = lax.dot_general(jnp.concatenate([kb, q], axis=0).astype(BF16), stack(k.astype(BF16)),
                                   (((1,), (1,)), ((), ())), preferred_element_type=F32)
            yield
            lower = prod[:c_len, :] * d_strict
            attn = prod[c_len:, :] * d_incl
            inv = []
            yield from _unit_lower_inverse(-lower, eye_w, c_len, stack_w, inv)
            rhs = jnp.concatenate([stack((v * beta).astype(BF16)), stack((kb * egc).astype(BF16))], axis=1)
            uw = jnp.dot(inv[0].astype(BF16), rhs, preferred_element_type=F32)
            yield
            u, w = uw[:, :width], uw[:, width:]

            state = s_ref[t]
            x0 = _mm(jnp.concatenate([w, q * egc], axis=0), state)
            yield
            v_new = u - x0[:c_len, :]
            o = x0[c_len:, :] + jnp.dot(attn.astype(BF16), stack(v_new.astype(BF16)), preferred_element_type=F32)
            ds = _mm_tn(k * jnp.exp(gl_lane - gcol), v_new)
            s_ref[t] = state * jnp.exp(gl_lane) + jnp.where(same_head, ds, 0.0)
            yield

            o_ref[pl.ds(r0, c_len), sl] = (per_head(o, rms) * _silu(z_all[:, sl])).astype(o_ref.dtype)

        return [chain(t) for t in range(n_tile)]

    def finish():
        _save_halos(((q_ref, hq_ref), (k_ref, hk_ref), (v_ref, hv_ref)), blk_rows)

    return chains, finish


def _gdn_specs(proj, conv_w, alog_row, dtb_row, nw_row, layer, blk, nsb):
    w = GDN_WIDTH
    colspec = lambda base: pl.BlockSpec((blk, w), lambda b, g, s, base=base: (b * nsb + s, base))
    cwspec = lambda base: pl.BlockSpec((None, 4, w), lambda b, g, s, base=base: (layer, 0, base))
    rowspec = pl.BlockSpec((None, 1, LANE), lambda b, g, s: (layer, 0, 0))
    in_specs = [colspec(0), colspec(1), colspec(2), colspec(COL_GZ * LANE // GDN_WIDTH),
                pl.BlockSpec((blk, LANE), lambda b, g, s: (b * nsb + s, COL_GBA)),
                cwspec(0), cwspec(1), cwspec(2), rowspec, rowspec, rowspec]
    scratch = [pltpu.VMEM((w // MXU_TILE, MXU_TILE, MXU_TILE), F32)] + [pltpu.VMEM((HALO, w), F32)] * 3
    return (proj, proj, proj, proj, proj, conv_w, conv_w, conv_w, alog_row, dtb_row, nw_row), in_specs, scratch


def _rwkv_parts(r_ref, k_ref, v_ref, wl_ref, al_ref, z_ref,
                mur_ref, muk_ref, muv_ref, muwl_ref, mual_ref,
                w0_ref, wup_ref, a0_ref, aup_ref, kk_ref, ka_ref, rk_ref, lnw_ref, lnb_ref,
                o_ref, n_ref, hr_ref, hk_ref, hv_ref, hwl_ref, hal_ref, *, blk_rows, n_tile):
    c_len = CHUNK
    nh = MXU_TILE // RWKV_HEAD_DIM
    width = MXU_TILE
    assert c_len == RWKV_HEAD_DIM

    same_head = (_iota2((width, width), 0) // RWKV_HEAD_DIM) == (_iota2((width, width), 1) // RWKV_HEAD_DIM)
    head_sum = jnp.where(same_head, 1.0, 0.0)
    t_w = _iota2((c_len, width), 0)
    s_w = _iota2((c_len, width), 1) % c_len
    strict_w = s_w < t_w
    incl_w = s_w <= t_w
    eye_w = jnp.where(s_w == t_w, 1.0, 0.0)
    ti = _iota2((c_len, c_len), 0)
    tj = _iota2((c_len, c_len), 1)
    tril_t = jnp.where(tj <= ti, 1.0, 0.0)
    lane_head = _iota2((1, width), 1) // RWKV_HEAD_DIM
    zero_b = jnp.zeros((), BF16)

    _recurrence_grid_setup(n_ref, (hr_ref, hk_ref, hv_ref, hwl_ref, hal_ref))

    def stack(xb):
        return jnp.concatenate([jnp.where(lane_head == h, xb, zero_b) for h in range(nh)], axis=0)

    def chains(c):
        r0 = pl.multiple_of(c * c_len, c_len)

        def shifted(x_ref, h_ref, mu_ref):
            cur, sh = _chunk_and_shifts(x_ref, h_ref, c, c_len, 1)
            return cur + (sh[0] - cur) * mu_ref[...]

        r_all = shifted(r_ref, hr_ref, mur_ref)
        k_all = shifted(k_ref, hk_ref, muk_ref)
        v_all = shifted(v_ref, hv_ref, muv_ref)
        wl = shifted(wl_ref, hwl_ref, muwl_ref)
        al = shifted(al_ref, hal_ref, mual_ref)
        z_all = z_ref[pl.ds(r0, c_len), :]

        w_pre = w0_ref[...] + _mm(jnp.tanh(wl), wup_ref[...])
        logw_all = -jnp.exp(-_softplus(-w_pre) - 0.5)
        a_all = _sigmoid(a0_ref[...] + _mm(al, aup_ref[...]))
        kkx_all = k_all * kk_ref[...]
        k2_all = k_all * (1.0 + (a_all - 1.0) * ka_ref[...])
        lc_all = _mm_sel_lhs(tril_t, logw_all)
        bonus_all = r_all * k2_all * rk_ref[...]

        def chain(t):
            sl = slice(t * width, (t + 1) * width)
            r, v, a, k2, logw, lc = r_all[:, sl], v_all[:, sl], a_all[:, sl], k2_all[:, sl], logw_all[:, sl], lc_all[:, sl]
            kkx = kkx_all[:, sl]
            sums = _mm_sel_rhs(jnp.concatenate([kkx * kkx, bonus_all[:, sl]], axis=0), head_sum)
            yield
            kk = kkx * lax.rsqrt(sums[:c_len, :] + 1e-6)
            bonus = sums[c_len:, :]

            lc_last = lc[c_len - 1:c_len, :]
            e_neg = jnp.exp(-lc)
            e_rem = jnp.exp(lc_last - lc)
            kka = kk * a
            a_n = -kk * jnp.exp(lc - logw)
            r_n = r * jnp.exp(lc)
            sv = stack(v.astype(BF16))

            prod = lax.dot_general(
                jnp.concatenate([a_n, r_n], axis=0).astype(BF16),
                jnp.concatenate([stack((kka * e_neg).astype(BF16)), stack((k2 * e_neg).astype(BF16))], axis=0),
                (((1,), (1,)), ((), ())), preferred_element_type=F32)
            yield
            a_ab = jnp.where(strict_w, prod[:c_len, :width], 0.0)
            a_ak = jnp.where(strict_w, prod[:c_len, width:], 0.0)
            a_rb = jnp.where(incl_w, prod[c_len:, :width], 0.0)
            a_rk = jnp.where(incl_w, prod[c_len:, width:], 0.0)

            inv = []
            yield from _unit_lower_inverse(a_ab, eye_w, c_len, stack, inv)
            t_mat = inv[0].astype(BF16)
            w_p = jnp.dot(t_mat, stack(a_n.astype(BF16)), preferred_element_type=F32)
            aks = jnp.dot(a_ak.astype(BF16), sv, preferred_element_type=F32)
            yield
            v_p = jnp.dot(t_mat, stack(aks.astype(BF16)), preferred_element_type=F32)
            yield

            state = n_ref[t]
            x0 = _mm(jnp.concatenate([w_p, r_n], axis=0), state)
            yield
            u = x0[:c_len, :] + v_p
            y = x0[c_len:, :] + jnp.dot(jnp.concatenate([a_rb, a_rk], axis=1).astype(BF16),
                                        jnp.concatenate([stack(u.astype(BF16)), sv], axis=0),
                                        preferred_element_type=F32)
            dn = _mm_tn(jnp.concatenate([kka * e_rem, k2 * e_rem], axis=0), jnp.concatenate([u, v], axis=0))
            gam_col = jnp.transpose(jnp.broadcast_to(jnp.exp(lc_last), (width, width)))
            n_ref[t] = gam_col * state + jnp.where(same_head, dn, 0.0)
            yield

            inv_n = 1.0 / RWKV_HEAD_DIM
            mean = _mm_sel_rhs(y, head_sum) * inv_n
            yield
            dlt = y - mean
            var = _mm_sel_rhs(dlt * dlt, head_sum) * inv_n
            yield
            y = dlt * lax.rsqrt(var + RWKV_LN_EPS) * lnw_ref[:, sl] + lnb_ref[:, sl]
            y = y + bonus * v
            o_ref[pl.ds(r0, c_len), sl] = (y * _silu(z_all[:, sl])).astype(o_ref.dtype)

        return [chain(t) for t in range(n_tile)]

    def finish():
        _save_halos(((r_ref, hr_ref), (k_ref, hk_ref), (v_ref, hv_ref), (wl_ref, hwl_ref), (al_ref, hal_ref)),
                    blk_rows)

    return chains, finish


def _rwkv_specs(proj, p, layer, blk, nsb):
    w = RWKV_WIDTH
    colspec = lambda base: pl.BlockSpec((blk, w), lambda b, g, s, base=base: (b * nsb + s, base))
    lspec = lambda col: pl.BlockSpec((blk, LANE), lambda b, g, s, col=col: (b * nsb + s, col))
    grow = pl.BlockSpec((None, 1, w), lambda b, g, s: (layer, 0, 0))
    lrow = pl.BlockSpec((None, 1, LANE), lambda b, g, s: (layer, 0, 0))
    upspec = pl.BlockSpec((None, LANE, w), lambda b, g, s: (layer, 0, 0))
    base = COL_RKV * LANE // RWKV_WIDTH
    in_specs = [colspec(base), colspec(base + 1), colspec(base + 2), lspec(COL_WL), lspec(COL_AL),
                colspec(COL_RZ * LANE // RWKV_WIDTH),
                grow, grow, grow, lrow, lrow,
                grow, upspec, grow, upspec, grow, grow, grow, grow, grow]
    scratch = ([pltpu.VMEM((w // MXU_TILE, MXU_TILE, MXU_TILE), F32)] + [pltpu.VMEM((HALO, w), F32)] * 3
               + [pltpu.VMEM((HALO, LANE), F32)] * 2)
    operands = (proj, proj, proj, proj, proj, proj,
                p["mu_r"], p["mu_k"], p["mu_v"], p["mu_wl"], p["mu_al"],
                p["w0"], p["w_up"], p["a0"], p["a_up"], p["k_k"], p["k_a"], p["r_k"], p["lnx_w"], p["lnx_b"])
    return operands, in_specs, scratch


def _mixers_kernel(*refs, n_gdn, n_rwkv, n_gdn_scr, blk_rows):
    gdn_in, rwkv_in = refs[:n_gdn], refs[n_gdn:n_gdn + n_rwkv]
    o_gdn, o_rwkv = refs[n_gdn + n_rwkv:n_gdn + n_rwkv + 2]
    scratch = refs[n_gdn + n_rwkv + 2:]
    g_chains, g_finish = _gdn_parts(*gdn_in, o_gdn, *scratch[:n_gdn_scr], blk_rows=blk_rows,
                                    n_tile=GDN_WIDTH // MXU_TILE)
    r_chains, r_finish = _rwkv_parts(*rwkv_in, o_rwkv, *scratch[n_gdn_scr:], blk_rows=blk_rows,
                                     n_tile=RWKV_WIDTH // MXU_TILE)

    def body(c, carry):
        _run_lockstep([ch for pair in zip(g_chains(c), r_chains(c)) for ch in pair])
        return carry

    lax.fori_loop(0, blk_rows // CHUNK, body, 0)
    g_finish()
    r_finish()


def _gdn_rwkv(proj, gdn_args, rwkv_p, layer, batch, seq):
    blk = min(SEQ_BLOCK, seq)
    nsb = seq // blk
    g_ops, g_specs, g_scr = _gdn_specs(proj, *gdn_args, layer, blk, nsb)
    r_ops, r_specs, r_scr = _rwkv_specs(proj, rwkv_p, layer, blk, nsb)
    out_spec = pl.BlockSpec((blk, GDN_WIDTH), lambda b, g, s: (b * nsb + s, 0))
    out_shape = jax.ShapeDtypeStruct((batch * seq, GDN_WIDTH), BF16)
    return pl.pallas_call(
        functools.partial(_mixers_kernel, n_gdn=len(g_ops), n_rwkv=len(r_ops), n_gdn_scr=len(g_scr),
                          blk_rows=blk),
        grid=(batch, 1, nsb),
        in_specs=g_specs + r_specs,
        out_specs=(out_spec, out_spec),
        out_shape=(out_shape, out_shape),
        scratch_shapes=g_scr + r_scr,
        compiler_params=pltpu.CompilerParams(
            dimension_semantics=("arbitrary", "arbitrary", "arbitrary"), vmem_limit_bytes=VMEM_LIMIT),
        name="gdn_rwkv",
    )(*g_ops, *r_ops)


def _s5_kernel(*refs, seq, n_lb):
    u_refs = refs[:n_lb]
    (lre_ref, lim_ref, ldt_ref, bre_ref, bim_ref, cre_ref, cim_ref, d_ref,
     o_ref, pre_ref, pim_ref, qre_ref, qim_ref, w_ref) = refs[n_lb:]
    tile = S5_TILE
    n_pair = LANE // (2 * S5_GROUP)
    n_chain = n_lb * n_pair
    sub = HALO
    n_sub = tile // sub

    def cmul(ar, ai, br, bi):
        return ar * br - ai * bi, ar * bi + ai * br

    def block_scan(sr, si, ar, ai, out):
        in_blk = _iota2(sr.shape, 0) % sub
        d = 1
        while d < sub:
            keep = in_blk >= d
            tr, ti = cmul(ar, ai, jnp.where(keep, pltpu.roll(sr, d, 0), 0.0),
                          jnp.where(keep, pltpu.roll(si, d, 0), 0.0))
            sr, si = sr + tr, si + ti
            ar, ai = cmul(ar, ai, ar, ai)
            d *= 2
            yield
        out.extend((sr, si))

    first = _iota2((sub, LANE), 0) == 0
    ab = []
    for q in range(n_chain):
        lb, p = divmod(q, n_pair)
        lre = lre_ref[lb, p:p + 1, :]
        lim = lim_ref[lb, p:p + 1, :]
        dt = jnp.exp(ldt_ref[lb, p:p + 1, :])
        mag = jnp.exp(lre * dt)
        ab_re = mag * jnp.cos(lim * dt)
        ab_im = mag * jnp.sin(lim * dt)
        den = lre * lre + lim * lim
        coef_re = ((ab_re - 1.0) * lre + ab_im * lim) / den
        coef_im = (ab_im * lre - (ab_re - 1.0) * lim) / den
        b_re = bre_ref[lb, p]
        b_im = bim_ref[lb, p]

        def place(m, p=p):
            top, n = p * 2 * S5_GROUP, m.shape[0]
            parts = [jnp.zeros((top, LANE), BF16)] if top else []
            parts.append(m.astype(BF16))
            if LANE - top - n:
                parts.append(jnp.zeros((LANE - top - n, LANE), BF16))
            return jnp.concatenate(parts, axis=0)

        w_ref[q, 0] = place(coef_re * b_re - coef_im * b_im)
        w_ref[q, 1] = place(coef_re * b_im + coef_im * b_re)
        w_ref[q, 2] = place(cre_ref[lb, p])
        w_ref[q, 3] = place(cim_ref[lb, p])
        pr, pi = [ab_re], [ab_im]
        for _ in range(1, n_sub):
            nr, ni = cmul(pr[-1], pi[-1], ab_re, ab_im)
            pr.append(nr)
            pi.append(ni)
        pre_ref[q] = jnp.concatenate(pr, axis=0)
        pim_ref[q] = jnp.concatenate(pi, axis=0)
        imp_re = jnp.where(first, jnp.broadcast_to(pr[-1], (sub, LANE)), 0.0)
        imp_im = jnp.where(first, jnp.broadcast_to(pi[-1], (sub, LANE)), 0.0)
        pw = []
        for _ in block_scan(imp_re, imp_im, pr[-1], pi[-1], pw):
            pass
        qre_ref[q] = pw[0]
        qim_ref[q] = pw[1]
        ab.append((ab_re, ab_im, pr[-1], pi[-1]))

    def tile_body(i, st):
        t0 = pl.multiple_of(i * tile, tile)
        us = [jnp.concatenate([u_refs[lb][pl.ds(t0 + j, sub, stride=n_sub), :] for j in range(n_sub)], axis=0)
              for lb in range(n_lb)]
        ubs = [u.astype(BF16) for u in us]
        ys, new_st = [None] * n_chain, [None] * n_chain

        def chain(p):
            cr, ci = st[p]
            ar, ai, a_run_r, a_run_i = ab[p]
            ub = ubs[p // n_pair]
            sr = jnp.dot(ub, w_ref[p, 0], preferred_element_type=F32)
            si = jnp.dot(ub, w_ref[p, 1], preferred_element_type=F32)
            yield
            loc_r, loc_i = [sr[:sub, :]], [si[:sub, :]]
            for j in range(1, n_sub):
                tr, ti = cmul(ar, ai, loc_r[-1], loc_i[-1])
                loc_r.append(sr[j * sub:(j + 1) * sub, :] + tr)
                loc_i.append(si[j * sub:(j + 1) * sub, :] + ti)
                if j % 4 == 0:
                    yield
            res = []
            yield from block_scan(loc_r[-1], loc_i[-1], a_run_r, a_run_i, res)
            tr, ti = cmul(qre_ref[p], qim_ref[p], cr, ci)
            end_r, end_i = res[0] + tr, res[1] + ti
            new_st[p] = (end_r[sub - 1:sub, :], end_i[sub - 1:sub, :])
            in_r = jnp.where(first, cr, pltpu.roll(end_r, 1, 0))
            in_i = jnp.where(first, ci, pltpu.roll(end_i, 1, 0))
            out_r, out_i = [], []
            for j in range(n_sub):
                tr, ti = cmul(pre_ref[p, j:j + 1, :], pim_ref[p, j:j + 1, :], in_r, in_i)
                out_r.append(loc_r[j] + tr)
                out_i.append(loc_i[j] + ti)
            sr, si = jnp.concatenate(out_r, axis=0), jnp.concatenate(out_i, axis=0)
            nt = (((1,), (1,)), ((), ()))
            ys[p] = (lax.dot_general(sr.astype(BF16), w_ref[p, 2], nt, preferred_element_type=F32)
                     - lax.dot_general(si.astype(BF16), w_ref[p, 3], nt, preferred_element_type=F32))

        _run_lockstep([chain(p) for p in range(n_chain)])
        for lb in range(n_lb):
            y = d_ref[lb] * us[lb]
            for p in range(n_pair):
                y = y + ys[lb * n_pair + p]
            inner = math.sqrt(2.0 / math.pi) * (y + 0.044715 * (y * y * y))
            y = 0.5 * y * (1.0 + jnp.tanh(inner))
            for j in range(n_sub):
                o_ref[lb, pl.ds(t0 + j, sub, stride=n_sub), :] = y[j * sub:(j + 1) * sub, :]
        return tuple(new_st)

    zero = jnp.zeros((1, LANE), F32)
    lax.fori_loop(0, seq // tile, tile_body, tuple((zero, zero) for _ in range(n_chain)))


def _s5(proj, p, layer, batch, seq, n_lb=4):
    n_blk = S5_WIDTH // (n_lb * LANE)
    n_pair = LANE // (2 * S5_GROUP)
    n_chain = n_lb * n_pair
    rowspec = pl.BlockSpec((None, n_lb, n_pair, LANE), lambda b, j: (layer, j, 0, 0))
    matspec = pl.BlockSpec((None, n_lb, n_pair, 2 * S5_GROUP, LANE), lambda b, j: (layer, j, 0, 0, 0))
    return pl.pallas_call(
        functools.partial(_s5_kernel, seq=seq, n_lb=n_lb),
        grid=(batch, n_blk),
        in_specs=[pl.BlockSpec((seq, LANE), lambda b, j, lb=lb: (b, COL_SU + n_lb * j + lb)) for lb in range(n_lb)]
                 + [rowspec, rowspec, rowspec, matspec, matspec, matspec, matspec,
                    pl.BlockSpec((None, n_lb, 1, LANE), lambda b, j: (layer, j, 0, 0))],
        out_specs=pl.BlockSpec((n_lb, seq, LANE), lambda b, j: (j, b, 0)),
        out_shape=jax.ShapeDtypeStruct((S5_WIDTH // LANE, batch * seq, LANE), F32),
        scratch_shapes=[pltpu.VMEM((n_chain, S5_TILE // HALO, LANE), F32)] * 2
                       + [pltpu.VMEM((n_chain, HALO, LANE), F32)] * 2
                       + [pltpu.VMEM((n_chain, 4, LANE, LANE), BF16)],
        compiler_params=pltpu.CompilerParams(
            dimension_semantics=("arbitrary", "arbitrary"), vmem_limit_bytes=VMEM_LIMIT),
        name="s5_scan",
    )(*([proj] * n_lb), p["lam_re"], p["lam_im"], p["log_dt"], p["b_re"], p["b_im"], p["c_re"], p["c_im"], p["d"])


def _s5_glu_kernel(y_ref, w_ref, b_ref, z_ref, o_ref):
    y = jnp.concatenate([y_ref[j] for j in range(S5_WIDTH // LANE)], axis=1)
    gate = _sigmoid(_mm(y, w_ref[...]) + b_ref[...])
    o_ref[...] = (y * gate * _silu(z_ref[...])).astype(o_ref.dtype)


def _s5_glu(yc, proj, glu_w, glu_b, layer, tm=512):
    t = yc.shape[1]
    tm = min(tm, t)
    return pl.pallas_call(
        _s5_glu_kernel,
        grid=(t // tm,),
        in_specs=[pl.BlockSpec((S5_WIDTH // LANE, tm, LANE), lambda i: (0, i, 0)),
                  pl.BlockSpec((None, S5_WIDTH, S5_WIDTH), lambda i: (layer, 0, 0)),
                  pl.BlockSpec((None, 1, S5_WIDTH), lambda i: (layer, 0, 0)),
                  pl.BlockSpec((tm, S5_WIDTH), lambda i: (i, COL_SZ * LANE // S5_WIDTH))],
        out_specs=pl.BlockSpec((tm, S5_WIDTH), lambda i: (i, 0)),
        out_shape=jax.ShapeDtypeStruct((t, S5_WIDTH), BF16),
        compiler_params=pltpu.CompilerParams(
            dimension_semantics=("arbitrary",), vmem_limit_bytes=VMEM_LIMIT),
        name="s5_glu",
    )(yc, glu_w, glu_b, proj)


def _merge_kernel(oa_ref, ob_ref, oc_ref, wb_ref, ga_ref, gb_ref, gc_ref, gbias_ref, o_ref):
    acc = None
    for i, (o_r, g_r) in enumerate(((oa_ref, ga_ref), (ob_ref, gb_ref), (oc_ref, gc_ref))):
        tn = o_ref.shape[1]
        col = pl.multiple_of(pl.program_id(1) * tn, tn)
        proj = jnp.dot(o_r[...], wb_ref[i, :, pl.ds(col, tn)], preferred_element_type=F32)
        term = _sigmoid(g_r[...] + gbias_ref[i]) * proj
        acc = term if acc is None else acc + term
    o_ref[...] = acc.astype(o_ref.dtype)


def _merge(oa, ob, oc, proj, w_branch, gate_b, layer, tm=1024, tn=512):
    t = oa.shape[0]
    tm = min(tm, t)
    gbase = COL_GATE * LANE // tn
    per = D_MODEL // tn
    ospec = pl.BlockSpec((tm, 1024), lambda i, j: (i, 0))
    gspec = lambda br: pl.BlockSpec((tm, tn), lambda i, j, br=br: (i, gbase + br * per + j))
    return pl.pallas_call(
        _merge_kernel,
        grid=(t // tm, per),
        in_specs=[ospec, ospec, ospec,
                  pl.BlockSpec((None, N_BRANCH, 1024, D_MODEL), lambda i, j: (layer, 0, 0, 0),
                               pipeline_mode=pl.Buffered(1)),
                  gspec(0), gspec(1), gspec(2),
                  pl.BlockSpec((None, N_BRANCH, 1, tn), lambda i, j: (layer, 0, 0, j))],
        out_specs=pl.BlockSpec((tm, tn), lambda i, j: (i, j)),
        out_shape=jax.ShapeDtypeStruct((t, D_MODEL), BF16),
        compiler_params=pltpu.CompilerParams(
            dimension_semantics=("arbitrary", "arbitrary"), vmem_limit_bytes=VMEM_LIMIT),
        name="merge",
    )(oa, ob, oc, w_branch, proj, proj, proj, gate_b)


def _outproj_kernel(m_ref, w_ref, x_ref, nw_ref, *o_refs):
    x = x_ref[...] + jnp.dot(m_ref[...], w_ref[...], preferred_element_type=F32)
    normed = _rms(x, nw_ref[...])
    if len(o_refs) == 2:
        o_refs[0][...] = x
    o_refs[-1][...] = normed.astype(o_refs[-1].dtype)


def _outproj(merged, w_out, x2, next_norm_rows, layer, last, tm=512):
    t = x2.shape[0]
    tm = min(tm, t)
    row_blk = pl.BlockSpec((tm, D_MODEL), lambda i: (i, 0))
    if last:
        out_specs, out_shape = row_blk, jax.ShapeDtypeStruct((t, D_MODEL), F32)
    else:
        out_specs = (row_blk, row_blk)
        out_shape = (jax.ShapeDtypeStruct((t, D_MODEL), F32), jax.ShapeDtypeStruct((t, D_MODEL), BF16))
    return pl.pallas_call(
        _outproj_kernel,
        grid=(t // tm,),
        in_specs=[row_blk,
                  pl.BlockSpec((None, D_MODEL, D_MODEL), lambda i: (layer, 0, 0)),
                  row_blk,
                  pl.BlockSpec((None, 1, D_MODEL), lambda i: (layer, 0, 0))],
        out_specs=out_specs,
        out_shape=out_shape,
        compiler_params=pltpu.CompilerParams(
            dimension_semantics=("arbitrary",), vmem_limit_bytes=VMEM_LIMIT),
        name="outproj",
    )(merged, w_out, x2, next_norm_rows)


def _pad_cols(a, width):
    return jnp.pad(a, [(0, 0)] * (a.ndim - 1) + [(0, width - a.shape[-1])])


def _pair_s5(a, rows_are_channels):
    n_blk, n_pair = S5_WIDTH // LANE, LANE // (2 * S5_GROUP)
    cs = a if rows_are_channels else jnp.swapaxes(a, 1, 2)
    cs = cs.reshape(n_blk, n_pair, 2, S5_GROUP, S5_STATE)
    same = jnp.eye(2, dtype=bool)[None, None, :, None, :, None]
    paired = jnp.where(same, cs[:, :, :, :, None, :], 0.0)
    return paired.reshape(n_blk, n_pair, 2 * S5_GROUP, 2 * S5_STATE)


def _s5_rows(a):
    return a.reshape(S5_WIDTH // LANE, LANE // (2 * S5_GROUP), 2 * S5_STATE)


def _rows(a):
    return a.reshape(a.shape[0], 1, -1)


def _gdn_head_rows(a):
    return _pad_cols(_rows(jnp.pad(a, ((0, 0), (GDN_HEADS, 0)))), LANE)


def _rwkv_params(mu, w0, w_up, a0, a_up, k_k, k_a, r_k, lnx_w, lnx_b):
    lora = lambda a: jnp.pad(a, ((0, 0), (0, LANE - RWKV_LORA), (0, 0))).astype(BF16)
    w3 = 3 * RWKV_WIDTH
    return dict(mu_r=_rows(mu[:, :RWKV_WIDTH]), mu_k=_rows(mu[:, RWKV_WIDTH:2 * RWKV_WIDTH]),
                mu_v=_rows(mu[:, 2 * RWKV_WIDTH:w3]),
                mu_wl=_pad_cols(_rows(mu[:, w3:w3 + RWKV_LORA]), LANE),
                mu_al=_pad_cols(_rows(mu[:, w3 + RWKV_LORA:]), LANE),
                w0=_rows(w0), w_up=lora(w_up), a0=_rows(a0), a_up=lora(a_up), k_k=_rows(k_k), k_a=_rows(k_a),
                r_k=_rows(r_k), lnx_w=_rows(lnx_w), lnx_b=_rows(lnx_b))


def _s5_params(a_re, a_im, log_dt, b_re, b_im, c_re, c_im, d):
    depth = a_re.shape[0]
    rows = jax.vmap(_s5_rows)
    place = lambda a, t: jax.vmap(lambda m: _pair_s5(m, t))(a)
    return dict(lam_re=rows(a_re), lam_im=rows(a_im),
                log_dt=rows(jnp.broadcast_to(log_dt[:, :, None], (depth, S5_WIDTH // S5_GROUP, S5_STATE))),
                b_re=place(b_re, False), b_im=place(b_im, False),
                c_re=place(c_re, True), c_im=place(c_im, True),
                d=d.reshape(depth, S5_WIDTH // LANE, 1, LANE))


def kernel(x, norm_w, w_in, gdn_conv_w, gdn_a_log, gdn_dt_bias, gdn_norm_w, rwkv_mu, rwkv_w0, rwkv_w_up,
           rwkv_a0, rwkv_a_up, rwkv_k_k, rwkv_k_a, rwkv_r_k, rwkv_lnx_w, rwkv_lnx_b, s5_a_re, s5_a_im,
           s5_log_dt, s5_b_re, s5_b_im, s5_c_re, s5_c_im, s5_d, s5_glu_w, s5_glu_b, gate_b, w_branch,
           w_out, final_norm_w):
    batch, seq, _ = x.shape
    depth = w_in.shape[0]
    x2 = x.reshape(batch * seq, D_MODEL)
    h = _rmsnorm(x2, norm_w[0].reshape(1, D_MODEL))

    gdn_alog, gdn_dtb, gdn_nw = _gdn_head_rows(gdn_a_log), _gdn_head_rows(gdn_dt_bias), _rows(gdn_norm_w)
    rp = _rwkv_params(rwkv_mu, rwkv_w0, rwkv_w_up, rwkv_a0, rwkv_a_up, rwkv_k_k, rwkv_k_a, rwkv_r_k,
                      rwkv_lnx_w, rwkv_lnx_b)
    sp = _s5_params(s5_a_re, s5_a_im, s5_log_dt, s5_b_re, s5_b_im, s5_c_re, s5_c_im, s5_d)
    glu_w, glu_b = s5_glu_w.astype(BF16), _rows(s5_glu_b)
    wb, gb = w_branch.astype(BF16), gate_b.reshape(depth, N_BRANCH, 1, D_MODEL)
    wo = w_out.astype(BF16)
    next_norm = _rows(jnp.concatenate([norm_w[1:], final_norm_w[None, :]], axis=0))
    w_in_t = jnp.swapaxes(w_in, 1, 2)

    for i in range(depth):
        proj = _inproj_t(h, w_in_t, i)
        o_a, o_b = _gdn_rwkv(proj, (gdn_conv_w, gdn_alog, gdn_dtb, gdn_nw), rp, i, batch, seq)
        o_c = _s5_glu(_s5(proj, sp, i, batch, seq), proj, glu_w, glu_b, i)
        merged = _merge(o_a, o_b, o_c, proj, wb, gb, i)
        if i == depth - 1:
            return _outproj(merged, wo, x2, next_norm, i, True).reshape(batch, seq, D_MODEL)
        x2, h = _outproj(merged, wo, x2, next_norm, i, False)
```
